```python
import jax, jax.numpy as jnp
from jax import lax
import numpy as np

D_MODEL = 2048
BATCH = 4
SEQ = 4096
DEPTH = 2

GRID_W = 64
NA_HEADS = 16
NA_HEAD_DIM = 64
NA_WIDTH = NA_HEADS * NA_HEAD_DIM
NA_KH_MAX = 8
NA_KW = 16
MLA_HEADS = 8
MLA_NOPE = 128
MLA_ROPE = 64
MLA_V = 128
MLA_Q_LORA = 448
MLA_KV_LORA = 160
MLA_WIDTH = MLA_HEADS * MLA_V
ROPE_THETA = 10000.0
Q_BLOCK = 128
FNET_GROUPS = 4
FNET_GROUP_DIM = 256
FNET_WIDTH = FNET_GROUPS * FNET_GROUP_DIM
N_BRANCHES = 3
BRANCH_WIDTH = NA_WIDTH
IN_SPLITS = (3 * NA_WIDTH, MLA_Q_LORA, MLA_KV_LORA, MLA_ROPE, FNET_WIDTH, N_BRANCHES * D_MODEL)
IN_COLS = sum(IN_SPLITS)
N_EXPERTS = 16
EXPERT_FF = 2048
CAPACITY_FACTOR = 2
RMS_EPS = 1e-6
NEG_INF = -1e30

kernel_name = "hybrid_na_mla_fnet_ecmoe_encoder"


def rmsnorm(x, g):
    xf = x.astype(jnp.float32)
    y = xf * lax.rsqrt(jnp.mean(xf * xf, axis=-1, keepdims=True) + RMS_EPS)
    return (y * g.astype(jnp.float32)).astype(x.dtype)


def rope_tables(seq):
    pos = jnp.arange(seq, dtype=jnp.float32)
    inv = 1.0 / (ROPE_THETA ** (jnp.arange(0, MLA_ROPE, 2, dtype=jnp.float32) / MLA_ROPE))
    ang = pos[:, None] * inv[None, :]
    return jnp.cos(ang), jnp.sin(ang)


def apply_rope(x, cos, sin):
    xf = x.astype(jnp.float32)
    x1, x2 = jnp.split(xf, 2, axis=-1)
    return jnp.concatenate([x1 * cos - x2 * sin, x2 * cos + x1 * sin], axis=-1).astype(x.dtype)


def neighbourhood_attention(q, k, v, rpb):
    B, S, H, dh = q.shape
    rows = S // GRID_W
    kh = min(NA_KH_MAX, rows)
    qg = q.reshape(B, rows, GRID_W, H, dh)
    kg = k.reshape(B, rows, GRID_W, H, dh)
    vg = v.reshape(B, rows, GRID_W, H, dh)
    cols = np.arange(GRID_W)
    col_start = np.clip(cols - NA_KW // 2, 0, GRID_W - NA_KW)
    col_valid = (cols[None, :] >= col_start[:, None]) & (cols[None, :] < col_start[:, None] + NA_KW)
    col_idx = np.clip(cols[None, :] - cols[:, None] + NA_KW - 1, 0, 2 * NA_KW - 2)
    scale = NA_HEAD_DIM ** -0.5

    def row_block(args):
        r, qr = args
        start = jnp.clip(r - kh // 2, 0, rows - kh)
        kb = lax.dynamic_slice_in_dim(kg, start, kh, axis=1)
        vb = lax.dynamic_slice_in_dim(vg, start, kh, axis=1)
        dr = start + jnp.arange(kh) - r + NA_KH_MAX - 1
        bias = rpb[:, dr[None, :, None], col_idx[:, None, :]]
        s = jnp.einsum('bqhd,bijhd->bhqij', qr, kb).astype(jnp.float32) * scale + bias.astype(jnp.float32)
        s = jnp.where(col_valid[:, None, :], s, NEG_INF)
        p = jax.nn.softmax(s.reshape(B, H, GRID_W, kh * GRID_W), axis=-1)
        p = p.reshape(B, H, GRID_W, kh, GRID_W).astype(v.dtype)
        return jnp.einsum('bhqij,bijhd->bqhd', p, vb)

    out = lax.map(row_block, (jnp.arange(rows), jnp.moveaxis(qg, 1, 0)))
    return jnp.moveaxis(out, 0, 1).reshape(B, S, H * dh)


def latent_attention(c_q, c_kv, k_rope, w_uq, q_norm, w_ukv, kv_norm, cos, sin):
    B, S, _ = c_q.shape
    q = (rmsnorm(c_q, q_norm) @ w_uq).reshape(B, S, MLA_HEADS, MLA_NOPE + MLA_ROPE)
    q_nope, q_pe = q[..., :MLA_NOPE], q[..., MLA_NOPE:]
    q_pe = apply_rope(q_pe, cos[:, None, :], sin[:, None, :])
    kv = (rmsnorm(c_kv, kv_norm) @ w_ukv).reshape(B, S, MLA_HEADS, MLA_NOPE + MLA_V)
    k_nope, v = kv[..., :MLA_NOPE], kv[..., MLA_NOPE:]
    k_pe = apply_rope(k_rope, cos, sin)
    nb = S // Q_BLOCK
    scale = (MLA_NOPE + MLA_ROPE) ** -0.5

    def to_blocks(t):
        return jnp.moveaxis(t.reshape(B, nb, Q_BLOCK, *t.shape[2:]), 1, 0)

    def q_block(args):
        qn, qp = args
        s = (jnp.einsum('bqhd,bkhd->bhqk', qn, k_nope)
             + jnp.einsum('bqhr,bkr->bhqk', qp, k_pe)).astype(jnp.float32) * scale
        p = jax.nn.softmax(s, axis=-1).astype(v.dtype)
        return jnp.einsum('bhqk,bkhd->bqhd', p, v)

    out = lax.map(q_block, (to_blocks(q_nope), to_blocks(q_pe)))
    return jnp.moveaxis(out, 0, 1).reshape(B, S, MLA_HEADS * MLA_V)


def fourier_mix(u):
    B, S, _ = u.shape
    g = u.reshape(B, S, FNET_GROUPS, FNET_GROUP_DIM).astype(jnp.float32)
    f = jnp.fft.fft2(g, axes=(1, 3), norm='ortho').real
    return f.astype(u.dtype).reshape(B, S, FNET_WIDTH)


def hybrid_mixer(xn, w_in, b_gate, w_uq, q_norm, w_ukv, kv_norm, rpb, w_branch, w_o, cos, sin):
    B, S, _ = xn.shape
    proj = xn @ w_in
    offsets = [int(o) for o in np.cumsum(IN_SPLITS)[:-1]]
    qkv_na, c_q, c_kv, k_rope, u_f, gate_logits = jnp.split(proj, offsets, axis=-1)
    q_na, k_na, v_na = [t.reshape(B, S, NA_HEADS, NA_HEAD_DIM) for t in jnp.split(qkv_na, 3, axis=-1)]
    y_na = neighbourhood_attention(q_na, k_na, v_na, rpb)
    y_mla = latent_attention(c_q, c_kv, k_rope, w_uq, q_norm, w_ukv, kv_norm, cos, sin)
    y_f = fourier_mix(u_f)
    gates = jax.nn.sigmoid((gate_logits + b_gate).astype(jnp.float32)).astype(xn.dtype)
    gates = gates.reshape(B, S, N_BRANCHES, D_MODEL)
    merged = (gates[:, :, 0] * (y_na @ w_branch[0])
              + gates[:, :, 1] * (y_mla @ w_branch[1])
              + gates[:, :, 2] * (y_f @ w_branch[2]))
    return merged @ w_o


def expert_choice_ffn(xn, w_router, w_g, w_u, w_d):
    B, S, D = xn.shape
    cap = CAPACITY_FACTOR * S // N_EXPERTS
    aff = jax.nn.softmax((xn @ w_router).astype(jnp.float32), axis=-1)
    gate, idx = lax.top_k(jnp.swapaxes(aff, 1, 2), cap)
    xe = jax.vmap(lambda xb, ib: xb[ib])(xn, idx)
    h = jax.nn.silu(jnp.einsum('becd,edf->becf', xe, w_g)) * jnp.einsum('becd,edf->becf', xe, w_u)
    ye = jnp.einsum('becf,efd->becd', h, w_d) * gate[..., None].astype(xn.dtype)
    return jax.vmap(lambda yb, ib: jnp.zeros((S, D), yb.dtype).at[ib.reshape(-1)].add(yb.reshape(-1, D)))(ye, idx)


def setup_inputs(seed: int = 0) -> dict:
    key = jax.random.key(seed)
    ks = jax.random.split(key, 20)

    def w(k, shape, fan_in):
        return jax.random.normal(k, shape, jnp.float32) * fan_in ** -0.5

    def gain(k, shape):
        return 1.0 + 0.05 * jax.random.normal(k, shape, jnp.float32)

    return {
        'x': jax.random.normal(ks[0], (BATCH, SEQ, D_MODEL), jnp.float32),
        'w_in': w(ks[1], (DEPTH, D_MODEL, IN_COLS), D_MODEL),
        'b_gate': 0.01 * jax.random.normal(ks[2], (DEPTH, N_BRANCHES * D_MODEL), jnp.float32),
        'w_uq': w(ks[3], (DEPTH, MLA_Q_LORA, MLA_HEADS * (MLA_NOPE + MLA_ROPE)), MLA_Q_LORA),
        'q_norm': gain(ks[4], (DEPTH, MLA_Q_LORA)),
        'w_ukv': w(ks[5], (DEPTH, MLA_KV_LORA, MLA_HEADS * (MLA_NOPE + MLA_V)), MLA_KV_LORA),
        'kv_norm': gain(ks[6], (DEPTH, MLA_KV_LORA)),
        'na_rpb': 0.1 * jax.random.normal(ks[7], (DEPTH, NA_HEADS, 2 * NA_KH_MAX - 1, 2 * NA_KW - 1), jnp.float32),
        'w_branch': w(ks[8], (DEPTH, N_BRANCHES, BRANCH_WIDTH, D_MODEL), BRANCH_WIDTH),
        'w_o': w(ks[9], (DEPTH, D_MODEL, D_MODEL), D_MODEL),
        'norm_mix': gain(ks[10], (DEPTH, D_MODEL)),
        'norm_moe': gain(ks[11], (DEPTH, D_MODEL)),
        'w_router': w(ks[12], (DEPTH, D_MODEL, N_EXPERTS), D_MODEL),
        'w_exp_gate': w(ks[13], (DEPTH, N_EXPERTS, D_MODEL, EXPERT_FF), D_MODEL),
        'w_exp_up': w(ks[14], (DEPTH, N_EXPERTS, D_MODEL, EXPERT_FF), D_MODEL),
        'w_exp_down': w(ks[15], (DEPTH, N_EXPERTS, EXPERT_FF, D_MODEL), EXPERT_FF),
        'norm_final': gain(ks[16], (D_MODEL,)),
    }


def reference(x, w_in, b_gate, w_uq, q_norm, w_ukv, kv_norm, na_rpb, w_branch, w_o,
              norm_mix, norm_moe, w_router, w_exp_gate, w_exp_up, w_exp_down, norm_final):
    cos, sin = rope_tables(x.shape[1])
    h = x
    for l in range(DEPTH):
        h = h + hybrid_mixer(rmsnorm(h, norm_mix[l]), w_in[l], b_gate[l], w_uq[l], q_norm[l],
                             w_ukv[l], kv_norm[l], na_rpb[l], w_branch[l], w_o[l], cos, sin)
        h = h + expert_choice_ffn(rmsnorm(h, norm_moe[l]), w_router[l], w_exp_gate[l],
                                  w_exp_up[l], w_exp_down[l])
    return rmsnorm(h, norm_final)
```

```python
import functools

import numpy as np
import jax
import jax.numpy as jnp
from jax import lax
from jax.experimental import pallas as pl
from jax.experimental.pallas import tpu as pltpu

D_MODEL = 2048
GRID_W = 64
NA_HEADS = 16
NA_HEAD_DIM = 64
NA_WIDTH = NA_HEADS * NA_HEAD_DIM
NA_KH = 8
NA_KW = 16
MLA_HEADS = 8
MLA_NOPE = 128
MLA_ROPE = 64
MLA_V = 128
MLA_Q_LORA = 448
MLA_KV_LORA = 160
ROPE_THETA = 10000.0
FNET_GROUPS = 4
FNET_GROUP_DIM = 256
N_BRANCHES = 3
BRANCH_WIDTH = 1024
N_EXPERTS = 16
EXPERT_FF = 2048
CAPACITY_FACTOR = 2
RMS_EPS = 1e-6
NEG_INF = -1e30

F32 = jnp.float32
BF16 = jnp.bfloat16

COL_QKV = 0
COL_UF = 3072
COL_LAT = 4096
COL_GATE = 5120
PROJ_COLS = COL_GATE + N_BRANCHES * D_MODEL
LAT_W = 1024
CQ_PAD = 512
CKV_PAD = 256

VMEM_LIMIT = 56 * 1024 * 1024


def _cparams(*sem):
    return pltpu.CompilerParams(dimension_semantics=sem, vmem_limit_bytes=VMEM_LIMIT)


def _inproj_kernel(x_ref, g_ref, w_ref, b_ref, o_ref, xn_ref, *, gate_tile0):
    j = pl.program_id(1)

    @pl.when(j == 0)
    def _():
        x = x_ref[...]
        ms = jnp.mean(x * x, axis=-1, keepdims=True)
        xn_ref[...] = (x * lax.rsqrt(ms + RMS_EPS) * g_ref[...]).astype(BF16)

    acc = jnp.dot(xn_ref[...], w_ref[...], preferred_element_type=F32)

    @pl.when(j < gate_tile0)
    def _():
        o_ref[...] = acc.astype(o_ref.dtype)

    @pl.when(j >= gate_tile0)
    def _():
        z = acc + b_ref[...]
        o_ref[...] = (1.0 / (1.0 + jnp.exp(-z))).astype(o_ref.dtype)


def _inproj(h2, gain, w_all, b_all, tm=1024, tn=512):
    T = h2.shape[0]
    return pl.pallas_call(
        functools.partial(_inproj_kernel, gate_tile0=COL_GATE // tn),
        out_shape=jax.ShapeDtypeStruct((T, PROJ_COLS), BF16),
        grid=(T // tm, PROJ_COLS // tn),
        in_specs=[
            pl.BlockSpec((tm, D_MODEL), lambda i, j: (i, 0)),
            pl.BlockSpec((1, D_MODEL), lambda i, j: (0, 0)),
            pl.BlockSpec((D_MODEL, tn), lambda i, j: (0, j)),
            pl.BlockSpec((1, tn), lambda i, j: (0, j)),
        ],
        out_specs=pl.BlockSpec((tm, tn), lambda i, j: (i, j)),
        scratch_shapes=[pltpu.VMEM((tm, D_MODEL), BF16)],
        compiler_params=_cparams("parallel", "arbitrary"),
        name="inproj",
    )(h2, gain, w_all, b_all)


def _na_kernel(q_ref, k_ref, v_ref, bias_ref, o_ref, *, rows):
    r = pl.program_id(1)
    start = jnp.clip(r - NA_KH // 2, 0, rows - NA_KH)
    row0 = pl.multiple_of(start * GRID_W, GRID_W)
    nk = NA_KH * GRID_W
    lo = lax.broadcasted_iota(jnp.int32, (GRID_W, 128), 1) < NA_HEAD_DIM
    for hp in range(NA_HEADS // 2):
        sl = slice(hp * 128, (hp + 1) * 128)
        q2 = q_ref[:, sl]
        zero = jnp.zeros_like(q2)
        qq = jnp.concatenate([jnp.where(lo, q2, zero), jnp.where(lo, zero, q2)], axis=0)
        k2 = k_ref[pl.ds(row0, nk), sl]
        v2 = v_ref[pl.ds(row0, nk), sl]
        s = lax.dot_general(qq, k2, (((1,), (1,)), ((), ())), preferred_element_type=F32)
        s = s * (NA_HEAD_DIM ** -0.5) + bias_ref[0, hp]
        m = jnp.max(s, axis=-1, keepdims=True)
        p = jnp.exp(s - m)
        l = jnp.sum(p, axis=-1, keepdims=True)
        o = jnp.dot(p.astype(BF16), v2, preferred_element_type=F32) / l
        o_ref[:, sl] = jnp.where(lo, o[:GRID_W], o[GRID_W:]).astype(o_ref.dtype)


def _na_bias_tables(rpb):
    cols = np.arange(GRID_W)
    col_start = np.clip(cols - NA_KW // 2, 0, GRID_W - NA_KW)
    col_valid = (cols[None, :] >= col_start[:, None]) & (cols[None, :] < col_start[:, None] + NA_KW)
    col_idx = np.clip(cols[None, :] - cols[:, None] + NA_KW - 1, 0, 2 * NA_KW - 2)
    t = np.arange(NA_KH)
    dr = np.arange(NA_KH)[None, :] - t[:, None] + NA_KH - 1
    b = rpb[:, dr[:, None, :, None], col_idx[None, :, None, :]]
    b = jnp.where(col_valid[None, None, :, None, :], b, NEG_INF)
    b = jnp.transpose(b, (1, 0, 2, 3, 4)).reshape(NA_KH, NA_HEADS // 2, 2 * GRID_W, NA_KH * GRID_W)
    return b.astype(F32)


def _na_attention(proj, bias, B, S):
    rows = S // GRID_W
    assert rows >= NA_KH
    T = B * S

    def bias_map(b, r):
        return (r - jnp.clip(r - NA_KH // 2, 0, rows - NA_KH), 0, 0, 0)

    return pl.pallas_call(
        functools.partial(_na_kernel, rows=rows),
        out_shape=jax.ShapeDtypeStruct((T, NA_WIDTH), BF16),
        grid=(B, rows),
        in_specs=[
            pl.BlockSpec((GRID_W, NA_WIDTH), lambda b, r: (b * rows + r, 0)),
            pl.BlockSpec((S, NA_WIDTH), lambda b, r: (b, 1)),
            pl.BlockSpec((S, NA_WIDTH), lambda b, r: (b, 2)),
            pl.BlockSpec((1, NA_HEADS // 2, 2 * GRID_W, NA_KH * GRID_W), bias_map),
        ],
        out_specs=pl.BlockSpec((GRID_W, NA_WIDTH), lambda b, r: (b * rows + r, 0)),
        compiler_params=_cparams("parallel", "arbitrary"),
        name="na_attn",
    )(proj, proj, proj, bias)


def _mla_up_kernel(lat_ref, cos_ref, sin_ref, qn_ref, kvn_ref, wq_ref, wrot_ref, wkv_ref, q_ref, k_ref, v_ref):
    cq = lat_ref[:, 0:CQ_PAD].astype(F32)
    ms = jnp.sum(cq * cq, axis=-1, keepdims=True) * (1.0 / MLA_Q_LORA)
    xq = (cq * lax.rsqrt(ms + RMS_EPS) * qn_ref[...]).astype(BF16)
    ckv = lat_ref[:, CQ_PAD:CQ_PAD + CKV_PAD].astype(F32)
    ms2 = jnp.sum(ckv * ckv, axis=-1, keepdims=True) * (1.0 / MLA_KV_LORA)
    xkv = (ckv * lax.rsqrt(ms2 + RMS_EPS) * kvn_ref[...]).astype(BF16)
    cos = cos_ref[...]
    sin = sin_ref[...]
    kpe = (lat_ref[:, 768:896].astype(F32) * cos + lat_ref[:, 896:1024].astype(F32) * sin).astype(BF16)
    scale = (MLA_NOPE + MLA_ROPE) ** -0.5
    for h in range(MLA_HEADS):
        a = jnp.dot(xq, wq_ref[:, h * 256:(h + 1) * 256], preferred_element_type=F32)
        rt = jnp.dot(xq, wrot_ref[:, h * 128:(h + 1) * 128], preferred_element_type=F32)
        q_ref[0, h, :, 0:128] = (a[:, 0:128] * scale).astype(BF16)
        q_ref[0, h, :, 128:256] = ((a[:, 128:256] * cos + rt * sin) * scale).astype(BF16)
        kv = jnp.dot(xkv, wkv_ref[:, h * 256:(h + 1) * 256], preferred_element_type=F32)
        k_ref[0, h, :, 0:128] = kv[:, 0:128].astype(BF16)
        k_ref[0, h, :, 128:256] = kpe
        v_ref[0, h] = kv[:, 128:256].astype(BF16)


def _mla_up(proj, cos128, sin128, qn, kvn, wq, wrot, wkv, B, S, tm=512):
    nt = S // tm
    lat_blk = COL_LAT // LAT_W
    const = lambda b, i: (0, 0)
    return pl.pallas_call(
        _mla_up_kernel,
        out_shape=(
            jax.ShapeDtypeStruct((B, MLA_HEADS, S, 256), BF16),
            jax.ShapeDtypeStruct((B, MLA_HEADS, S, 256), BF16),
            jax.ShapeDtypeStruct((B, MLA_HEADS, S, MLA_V), BF16),
        ),
        grid=(B, nt),
        in_specs=[
            pl.BlockSpec((tm, LAT_W), lambda b, i: (b * nt + i, lat_blk)),
            pl.BlockSpec((tm, 128), lambda b, i: (i, 0)),
            pl.BlockSpec((tm, 128), lambda b, i: (i, 0)),
            pl.BlockSpec((1, CQ_PAD), const),
            pl.BlockSpec((1, CKV_PAD), const),
            pl.BlockSpec((CQ_PAD, MLA_HEADS * 256), const),
            pl.BlockSpec((CQ_PAD, MLA_HEADS * 128), const),
            pl.BlockSpec((CKV_PAD, MLA_HEADS * 256), const),
        ],
        out_specs=(
            pl.BlockSpec((1, MLA_HEADS, tm, 256), lambda b, i: (b, 0, i, 0)),
            pl.BlockSpec((1, MLA_HEADS, tm, 256), lambda b, i: (b, 0, i, 0)),
            pl.BlockSpec((1, MLA_HEADS, tm, MLA_V), lambda b, i: (b, 0, i, 0)),
        ),
        compiler_params=_cparams("parallel", "parallel"),
        name="mla_up",
    )(proj, cos128, sin128, qn, kvn, wq, wrot, wkv)


def _mla_attn_kernel(q_ref, k_ref, v_ref, o_ref):
    s = lax.dot_general(q_ref[0, 0], k_ref[0, 0], (((1,), (1,)), ((), ())), preferred_element_type=F32)
    m = jnp.max(s, axis=-1, keepdims=True)
    p = jnp.exp(s - m)
    l = jnp.sum(p, axis=-1, keepdims=True)
    o = jnp.dot(p.astype(BF16), v_ref[0, 0], preferred_element_type=F32) / l
    o_ref[...] = o.astype(o_ref.dtype)


def _mla_attn(q, k, v, B, S, tq=512):
    nq = S // tq
    return pl.pallas_call(
        _mla_attn_kernel,
        out_shape=jax.ShapeDtypeStruct((B * S, MLA_HEADS * MLA_V), BF16),
        grid=(B, MLA_HEADS, nq),
        in_specs=[
            pl.BlockSpec((1, 1, tq, 256), lambda b, h, i: (b, h, i, 0)),
            pl.BlockSpec((1, 1, S, 256), lambda b, h, i: (b, h, 0, 0)),
            pl.BlockSpec((1, 1, S, MLA_V), lambda b, h, i: (b, h, 0, 0)),
        ],
        out_specs=pl.BlockSpec((tq, MLA_V), lambda b, h, i: (b * nq + i, h)),
        compiler_params=_cparams("parallel", "parallel", "arbitrary"),
        name="mla_attn",
    )(q, k, v)


def _fnet_ch_kernel(u_ref, w_ref, a_ref, b_ref):
    gd = FNET_GROUP_DIM
    for g in range(FNET_GROUPS):
        ab = jnp.dot(u_ref[:, g * gd:(g + 1) * gd], w_ref[...], preferred_element_type=F32)
        a_ref[:, g * gd:(g + 1) * gd] = ab[:, :gd].astype(a_ref.dtype)
        b_ref[:, g * gd:(g + 1) * gd] = ab[:, gd:].astype(b_ref.dtype)


def _fnet_channel(proj, w_cs, tm=1024):
    T = proj.shape[0]
    W = FNET_GROUPS * FNET_GROUP_DIM
    return pl.pallas_call(
        _fnet_ch_kernel,
        out_shape=(jax.ShapeDtypeStruct((T, W), BF16), jax.ShapeDtypeStruct((T, W), BF16)),
        grid=(T // tm,),
        in_specs=[
            pl.BlockSpec((tm, W), lambda i: (i, COL_UF // W)),
            pl.BlockSpec((FNET_GROUP_DIM, 2 * FNET_GROUP_DIM), lambda i: (0, 0)),
        ],
        out_specs=(pl.BlockSpec((tm, W), lambda i: (i, 0)), pl.BlockSpec((tm, W), lambda i: (i, 0))),
        compiler_params=_cparams("parallel"),
        name="fnet_channel",
    )(proj, w_cs)


def _fnet_pos_kernel(c_ref, s_ref, a_ref, b_ref, o_ref, *, scale):
    y = (jnp.dot(c_ref[...], a_ref[...], preferred_element_type=F32)
         + jnp.dot(s_ref[...], b_ref[...], preferred_element_type=F32))
    o_ref[...] = (y * scale).astype(o_ref.dtype)


def _fnet_position(c_tab, sn_tab, a, b, B, S, tm=512, tn=512):
    W = FNET_GROUPS * FNET_GROUP_DIM
    nm = S // tm
    scale = float((S * FNET_GROUP_DIM) ** -0.5)
    return pl.pallas_call(
        functools.partial(_fnet_pos_kernel, scale=scale),
        out_shape=jax.ShapeDtypeStruct((B * S, W), BF16),
        grid=(nm, B, W // tn),
        in_specs=[
            pl.BlockSpec((tm, S), lambda m, bb, n: (m, 0)),
            pl.BlockSpec((tm, S), lambda m, bb, n: (m, 0)),
            pl.BlockSpec((S, tn), lambda m, bb, n: (bb, n)),
            pl.BlockSpec((S, tn), lambda m, bb, n: (bb, n)),
        ],
        out_specs=pl.BlockSpec((tm, tn), lambda m, bb, n: (bb * nm + m, n)),
        compiler_params=_cparams("parallel", "parallel", "parallel"),
        name="fnet_position",
    )(c_tab, sn_tab, a, b)


def _dft_tables(S):
    gd = FNET_GROUP_DIM
    ck = (np.arange(gd)[:, None] * np.arange(gd)[None, :]) % gd
    ang = 2.0 * np.pi * ck / gd
    w_cs = jnp.asarray(np.concatenate([np.cos(ang), np.sin(ang)], axis=1), F32).astype(BF16)
    kn = (lax.broadcasted_iota(jnp.int32, (S, S), 0) * lax.broadcasted_iota(jnp.int32, (S, S), 1)) % S
    angs = kn.astype(F32) * (2.0 * np.pi / S)
    return w_cs, jnp.cos(angs).astype(BF16), (-jnp.sin(angs)).astype(BF16)


def _merge_kernel(yn_ref, ym_ref, yf_ref, w_ref, g0_ref, g1_ref, g2_ref, o_ref):
    acc = g0_ref[...].astype(F32) * jnp.dot(yn_ref[...], w_ref[0], preferred_element_type=F32)
    acc += g1_ref[...].astype(F32) * jnp.dot(ym_ref[...], w_ref[1], preferred_element_type=F32)
    acc += g2_ref[...].astype(F32) * jnp.dot(yf_ref[...], w_ref[2], preferred_element_type=F32)
    o_ref[...] = acc.astype(o_ref.dtype)


def _merge(y_na, y_mla, y_f, w_branch, proj, tm=1024, tn=512):
    T = y_na.shape[0]
    ybs = pl.BlockSpec((tm, BRANCH_WIDTH), lambda i, j: (i, 0))

    def gate_spec(br):
        off = (COL_GATE + br * D_MODEL) // tn
        return pl.BlockSpec((tm, tn), lambda i, j: (i, off + j))

    return pl.pallas_call(
        _merge_kernel,
        out_shape=jax.ShapeDtypeStruct((T, D_MODEL), BF16),
        grid=(T // tm, D_MODEL // tn),
        in_specs=[ybs, ybs, ybs,
                  pl.BlockSpec((N_BRANCHES, BRANCH_WIDTH, tn), lambda i, j: (0, 0, j)),
                  gate_spec(0), gate_spec(1), gate_spec(2)],
        out_specs=pl.BlockSpec((tm, tn), lambda i, j: (i, j)),
        compiler_params=_cparams("parallel", "arbitrary"),
        name="merge",
    )(y_na, y_mla, y_f, w_branch, proj, proj, proj)


def _outproj_kernel(m_ref, w_ref, h_ref, o_ref):
    o_ref[...] = h_ref[...] + jnp.dot(m_ref[...], w_ref[...], preferred_element_type=F32)


def _outproj(merged, w_o, h2, tm=1024, tn=512):
    T = merged.shape[0]
    return pl.pallas_call(
        _outproj_kernel,
        out_shape=jax.ShapeDtypeStruct((T, D_MODEL), F32),
        grid=(T // tm, D_MODEL // tn),
        in_specs=[
            pl.BlockSpec((tm, D_MODEL), lambda i, j: (i, 0)),
            pl.BlockSpec((D_MODEL, tn), lambda i, j: (0, j)),
            pl.BlockSpec((tm, tn), lambda i, j: (i, j)),
        ],
        out_specs=pl.BlockSpec((tm, tn), lambda i, j: (i, j)),
        compiler_params=_cparams("parallel", "arbitrary"),
        name="outproj",
    )(merged, w_o, h2)


def _router_kernel(x_ref, g_ref, wr_ref, xn_ref, aff_ref):
    x = x_ref[...]
    ms = jnp.mean(x * x, axis=-1, keepdims=True)
    xn = x * lax.rsqrt(ms + RMS_EPS) * g_ref[...]
    xn_ref[...] = xn.astype(xn_ref.dtype)
    logits = jnp.dot(xn, wr_ref[...], preferred_element_type=F32, precision=lax.Precision.HIGHEST)
    m = jnp.max(logits, axis=-1, keepdims=True)
    e = jnp.exp(logits - m)
    aff_ref[...] = e / jnp.sum(e, axis=-1, keepdims=True)


def _router(h2, gain, w_router, tm=1024):
    T = h2.shape[0]
    return pl.pallas_call(
        _router_kernel,
        out_shape=(jax.ShapeDtypeStruct((T, D_MODEL), BF16), jax.ShapeDtypeStruct((T, N_EXPERTS), F32)),
        grid=(T // tm,),
        in_specs=[
            pl.BlockSpec((tm, D_MODEL), lambda i: (i, 0)),
            pl.BlockSpec((1, D_MODEL), lambda i: (0, 0)),
            pl.BlockSpec((D_MODEL, N_EXPERTS), lambda i: (0, 0)),
        ],
        out_specs=(pl.BlockSpec((tm, D_MODEL), lambda i: (i, 0)), pl.BlockSpec((tm, N_EXPERTS), lambda i: (i, 0))),
        compiler_params=_cparams("parallel"),
        name="router",
    )(h2, gain, w_router)


def _expert_kernel(x_ref, wg_ref, wu_ref, wd_ref, gate_ref, o_ref, acc_ref):
    f = pl.program_id(2)
    x = x_ref[...]
    a = jnp.dot(x, wg_ref[0], preferred_element_type=F32)
    u = jnp.dot(x, wu_ref[0], preferred_element_type=F32)
    hid = (a * (1.0 / (1.0 + jnp.exp(-a))) * u).astype(BF16)
    y = jnp.dot(hid, wd_ref[0], preferred_element_type=F32)

    @pl.when(f == 0)
    def _():
        acc_ref[...] = y

    @pl.when(f > 0)
    def _():
        acc_ref[...] += y

    @pl.when(f == pl.num_programs(2) - 1)
    def _():
        o_ref[...] = acc_ref[...] * gate_ref[...]


def _experts(xe, wg, wu, wd, gate, tm=1024, tf=512):
    n_tok = xe.shape[0]
    per_e = n_tok // N_EXPERTS
    tm = min(tm, per_e)
    mt = per_e // tm
    return pl.pallas_call(
        _expert_kernel,
        out_shape=jax.ShapeDtypeStruct((n_tok, D_MODEL), F32),
        grid=(N_EXPERTS, mt, EXPERT_FF // tf),
        in_specs=[
            pl.BlockSpec((tm, D_MODEL), lambda e, m, f: (e * mt + m, 0)),
            pl.BlockSpec((1, D_MODEL, tf), lambda e, m, f: (e, 0, f)),
            pl.BlockSpec((1, D_MODEL, tf), lambda e, m, f: (e, 0, f)),
            pl.BlockSpec((1, tf, D_MODEL), lambda e, m, f: (e, f, 0)),
            pl.BlockSpec((tm, 1), lambda e, m, f: (e * mt + m, 0)),
        ],
        out_specs=pl.BlockSpec((tm, D_MODEL), lambda e, m, f: (e * mt + m, 0)),
        scratch_shapes=[pltpu.VMEM((tm, D_MODEL), F32)],
        compiler_params=_cparams("parallel", "parallel", "arbitrary"),
        name="experts",
    )(xe, wg, wu, wd, gate)


def _final_norm_kernel(x_ref, g_ref, o_ref):
    x = x_ref[...]
    ms = jnp.mean(x * x, axis=-1, keepdims=True)
    o_ref[...] = x * lax.rsqrt(ms + RMS_EPS) * g_ref[...]


def _final_norm(h2, gain, tm=1024):
    T = h2.shape[0]
    return pl.pallas_call(
        _final_norm_kernel,
        out_shape=jax.ShapeDtypeStruct((T, D_MODEL), F32),
        grid=(T // tm,),
        in_specs=[pl.BlockSpec((tm, D_MODEL), lambda i: (i, 0)), pl.BlockSpec((1, D_MODEL), lambda i: (0, 0))],
        out_specs=pl.BlockSpec((tm, D_MODEL), lambda i: (i, 0)),
        compiler_params=_cparams("parallel"),
        name="final_norm",
    )(h2, gain)


def _pad_cols(w, n):
    return jnp.pad(w, ((0, 0), (0, n - w.shape[1])))


def _rot_cols(w):
    half = w.shape[1] // 2
    return jnp.concatenate([-w[:, half:], w[:, :half]], axis=1)


def _prep_in_weights(w_in, b_gate):
    o = np.cumsum([0, 3 * NA_WIDTH, MLA_Q_LORA, MLA_KV_LORA, MLA_ROPE, FNET_GROUPS * FNET_GROUP_DIM])
    qkv, cq, ckv, kr, uf, gl = (w_in[:, o[0]:o[1]], w_in[:, o[1]:o[2]], w_in[:, o[2]:o[3]], w_in[:, o[3]:o[4]],
                                w_in[:, o[4]:o[5]], w_in[:, o[5]:])
    w_all = jnp.concatenate([qkv, uf, _pad_cols(cq, CQ_PAD), _pad_cols(ckv, CKV_PAD), _pad_cols(kr, 128),
                             _pad_cols(_rot_cols(kr), 128), gl], axis=1).astype(BF16)
    b_all = jnp.concatenate([jnp.zeros((COL_GATE,), F32), b_gate]).reshape(1, PROJ_COLS)
    return w_all, b_all


def _prep_mla_weights(w_uq, q_norm, w_ukv, kv_norm):
    qd = MLA_NOPE + MLA_ROPE
    wq3 = w_uq.reshape(MLA_Q_LORA, MLA_HEADS, qd)
    nope, pe = wq3[:, :, :MLA_NOPE], wq3[:, :, MLA_NOPE:]
    z64 = jnp.zeros((MLA_Q_LORA, MLA_HEADS, 64), F32)
    wq = jnp.concatenate([nope, pe, z64], axis=2).reshape(MLA_Q_LORA, MLA_HEADS * 256)
    rot = jnp.concatenate([-pe[:, :, 32:], pe[:, :, :32], z64], axis=2).reshape(MLA_Q_LORA, MLA_HEADS * 128)
    rpad = ((0, CQ_PAD - MLA_Q_LORA), (0, 0))
    wq = jnp.pad(wq, rpad).astype(BF16)
    rot = jnp.pad(rot, rpad).astype(BF16)
    wkv = jnp.pad(w_ukv, ((0, CKV_PAD - MLA_KV_LORA), (0, 0))).astype(BF16)
    qn = jnp.pad(q_norm, (0, CQ_PAD - MLA_Q_LORA)).reshape(1, CQ_PAD)
    kvn = jnp.pad(kv_norm, (0, CKV_PAD - MLA_KV_LORA)).reshape(1, CKV_PAD)
    return wq, rot, wkv, qn, kvn


def _rope_tables128(S):
    pos = jnp.arange(S, dtype=F32)
    inv = 1.0 / (ROPE_THETA ** (jnp.arange(0, MLA_ROPE, 2, dtype=F32) / MLA_ROPE))
    ang = pos[:, None] * inv[None, :]
    z = jnp.zeros((S, 64), F32)
    cos, sin = jnp.cos(ang), jnp.sin(ang)
    return jnp.concatenate([cos, cos, z], axis=1), jnp.concatenate([sin, sin, z], axis=1)


def kernel(x, w_in, b_gate, w_uq, q_norm, w_ukv, kv_norm, na_rpb, w_branch, w_o, norm_mix, norm_moe,
           w_router, w_exp_gate, w_exp_up, w_exp_down, norm_final):
    B, S, D = x.shape
    T = B * S
    depth = w_in.shape[0]
    cap = CAPACITY_FACTOR * S // N_EXPERTS
    cos128, sin128 = _rope_tables128(S)
    w_cs, c_tab, sn_tab = _dft_tables(S)
    h = x.reshape(T, D)
    for l in range(depth):
        w_all, b_all = _prep_in_weights(w_in[l], b_gate[l])
        proj = _inproj(h, norm_mix[l].reshape(1, D), w_all, b_all)
        y_na = _na_attention(proj, _na_bias_tables(na_rpb[l]), B, S)
        wq, wrot, wkv, qn, kvn = _prep_mla_weights(w_uq[l], q_norm[l], w_ukv[l], kv_norm[l])
        q, k, v = _mla_up(proj, cos128, sin128, qn, kvn, wq, wrot, wkv, B, S)
        y_mla = _mla_attn(q, k, v, B, S)
        fa, fb = _fnet_channel(proj, w_cs)
        y_f = _fnet_position(c_tab, sn_tab, fa, fb, B, S)
        merged = _merge(y_na, y_mla, y_f, w_branch[l].astype(BF16), proj)
        h = _outproj(merged, w_o[l].astype(BF16), h)
        xn, aff = _router(h, norm_moe[l].reshape(1, D), w_router[l])
        gate, idx = lax.top_k(jnp.swapaxes(aff.reshape(B, S, N_EXPERTS), 1, 2), cap)
        flat = (idx + (jnp.arange(B, dtype=idx.dtype) * S)[:, None, None])
        flat = jnp.swapaxes(flat, 0, 1).reshape(-1)
        gate_e = jnp.swapaxes(gate, 0, 1).reshape(-1, 1)
        xe = xn[flat]
        ye = _experts(xe, w_exp_gate[l].astype(BF16), w_exp_up[l].astype(BF16), w_exp_down[l].astype(BF16), gate_e)
        h = h.at[flat].add(ye)
    return _final_norm(h, norm_final.reshape(1, D)).reshape(B, S, D)
```

```python
import functools

import numpy as np
import jax
import jax.numpy as jnp
from jax import lax
from jax.experimental import pallas as pl
from jax.experimental.pallas import tpu as pltpu

D_MODEL = 2048
GRID_W = 64
NA_HEADS = 16
NA_HEAD_DIM = 64
NA_WIDTH = NA_HEADS * NA_HEAD_DIM
NA_KH = 8
NA_KW = 16
MLA_HEADS = 8
MLA_NOPE = 128
MLA_ROPE = 64
MLA_V = 128
MLA_Q_LORA = 448
MLA_KV_LORA = 160
ROPE_THETA = 10000.0
FNET_GROUPS = 4
FNET_GROUP_DIM = 256
N_BRANCHES = 3
BRANCH_WIDTH = 1024
N_EXPERTS = 16
EXPERT_FF = 2048
CAPACITY_FACTOR = 2
RMS_EPS = 1e-6
NEG_INF = -1e30

F32 = jnp.float32
BF16 = jnp.bfloat16

COL_QKV = 0
COL_UF = 3072
COL_LAT = 4096
COL_GATE = 5120
PROJ_COLS = COL_GATE + N_BRANCHES * D_MODEL
LAT_W = 1024
CQ_PAD = 512
CKV_PAD = 256
VT_ROWS = MLA_V + 16
LOG2E = 1.4426950408889634

VMEM_LIMIT = 56 * 1024 * 1024


def _cparams(*sem):
    return pltpu.CompilerParams(dimension_semantics=sem, vmem_limit_bytes=VMEM_LIMIT)


def _inproj_kernel(x_ref, g_ref, w_ref, b_ref, o_ref, xn_ref, *, gate_tile0):
    j = pl.program_id(1)

    @pl.when(j == 0)
    def _():
        x = x_ref[...]
        ms = jnp.mean(x * x, axis=-1, keepdims=True)
        xn_ref[...] = (x * lax.rsqrt(ms + RMS_EPS) * g_ref[...]).astype(BF16)

    acc = jnp.dot(xn_ref[...], w_ref[...], preferred_element_type=F32)

    @pl.when(j < gate_tile0)
    def _():
        o_ref[...] = acc.astype(o_ref.dtype)

    @pl.when(j >= gate_tile0)
    def _():
        z = acc + b_ref[...]
        o_ref[...] = (1.0 / (1.0 + jnp.exp(-z))).astype(o_ref.dtype)


def _inproj(h2, gain, w_all, b_all, tm=1024, tn=512):
    T = h2.shape[0]
    return pl.pallas_call(
        functools.partial(_inproj_kernel, gate_tile0=COL_GATE // tn),
        out_shape=jax.ShapeDtypeStruct((T, PROJ_COLS), BF16),
        grid=(T // tm, PROJ_COLS // tn),
        in_specs=[
            pl.BlockSpec((tm, D_MODEL), lambda i, j: (i, 0)),
            pl.BlockSpec((1, D_MODEL), lambda i, j: (0, 0)),
            pl.BlockSpec((D_MODEL, tn), lambda i, j: (0, j)),
            pl.BlockSpec((1, tn), lambda i, j: (0, j)),
        ],
        out_specs=pl.BlockSpec((tm, tn), lambda i, j: (i, j)),
        scratch_shapes=[pltpu.VMEM((tm, D_MODEL), BF16)],
        compiler_params=_cparams("parallel", "arbitrary"),
        name="inproj",
    )(h2, gain, w_all, b_all)


def _na_kernel(q_ref, k_ref, v_ref, bias_ref, o_ref, *, rows):
    r = pl.program_id(1)
    start = jnp.clip(r - NA_KH // 2, 0, rows - NA_KH)
    row0 = pl.multiple_of(start * GRID_W, GRID_W)
    nk = NA_KH * GRID_W
    lo = lax.broadcasted_iota(jnp.int32, (GRID_W, 128), 1) < NA_HEAD_DIM
    for hp in range(NA_HEADS // 2):
        sl = slice(hp * 128, (hp + 1) * 128)
        q2 = q_ref[:, sl]
        zero = jnp.zeros_like(q2)
        qq = jnp.concatenate([jnp.where(lo, q2, zero), jnp.where(lo, zero, q2)], axis=0)
        k2 = k_ref[pl.ds(row0, nk), sl]
        v2 = v_ref[pl.ds(row0, nk), sl]
        s = lax.dot_general(qq, k2, (((1,), (1,)), ((), ())), preferred_element_type=F32)
        s = s * (NA_HEAD_DIM ** -0.5) + bias_ref[0, hp]
        m = jnp.max(s, axis=-1, keepdims=True)
        p = jnp.exp(s - m)
        l = jnp.sum(p, axis=-1, keepdims=True)
        o = jnp.dot(p.astype(BF16), v2, preferred_element_type=F32) / l
        o_ref[:, sl] = jnp.where(lo, o[:GRID_W], o[GRID_W:]).astype(o_ref.dtype)


def _na_bias_tables(rpb):
    cols = np.arange(GRID_W)
    col_start = np.clip(cols - NA_KW // 2, 0, GRID_W - NA_KW)
    col_valid = (cols[None, :] >= col_start[:, None]) & (cols[None, :] < col_start[:, None] + NA_KW)
    col_idx = np.clip(cols[None, :] - cols[:, None] + NA_KW - 1, 0, 2 * NA_KW - 2)
    onehot = (col_idx[None] == np.arange(2 * NA_KW - 1)[:, None, None]).astype(np.float32)
    toep = jnp.einsum("hdc,cqj->hdqj", rpb.astype(F32), jnp.asarray(onehot), precision=lax.Precision.HIGHEST)
    variants = []
    for t in range(NA_KH):
        b = toep[:, NA_KH - 1 - t:2 * NA_KH - 1 - t]
        b = jnp.where(col_valid[None, None], b, NEG_INF)
        variants.append(jnp.transpose(b, (0, 2, 1, 3)).reshape(NA_HEADS // 2, 2 * GRID_W, NA_KH * GRID_W))
    return jnp.stack(variants).astype(F32)


def _na_attention(proj, bias, B, S):
    rows = S // GRID_W
    assert rows >= NA_KH
    T = B * S

    def bias_map(b, r):
        return (r - jnp.clip(r - NA_KH // 2, 0, rows - NA_KH), 0, 0, 0)

    return pl.pallas_call(
        functools.partial(_na_kernel, rows=rows),
        out_shape=jax.ShapeDtypeStruct((T, NA_WIDTH), BF16),
        grid=(B, rows),
        in_specs=[
            pl.BlockSpec((GRID_W, NA_WIDTH), lambda b, r: (b * rows + r, 0)),
            pl.BlockSpec((S, NA_WIDTH), lambda b, r: (b, 1)),
            pl.BlockSpec((S, NA_WIDTH), lambda b, r: (b, 2)),
            pl.BlockSpec((1, NA_HEADS // 2, 2 * GRID_W, NA_KH * GRID_W), bias_map),
        ],
        out_specs=pl.BlockSpec((GRID_W, NA_WIDTH), lambda b, r: (b * rows + r, 0)),
        compiler_params=_cparams("parallel", "arbitrary"),
        name="na_attn",
    )(proj, proj, proj, bias)


def _mla_up_kernel(lat_ref, cos_ref, sin_ref, qn_ref, kvn_ref, wq_ref, wrot_ref, wk_ref, wvt_ref,
                   q_ref, k_ref, vt_ref):
    cq = lat_ref[:, 0:CQ_PAD].astype(F32)
    ms = jnp.sum(cq * cq, axis=-1, keepdims=True) * (1.0 / MLA_Q_LORA)
    xq = (cq * lax.rsqrt(ms + RMS_EPS) * qn_ref[...]).astype(BF16)
    ckv = lat_ref[:, CQ_PAD:CQ_PAD + CKV_PAD].astype(F32)
    ms2 = jnp.sum(ckv * ckv, axis=-1, keepdims=True) * (1.0 / MLA_KV_LORA)
    xkv = (ckv * lax.rsqrt(ms2 + RMS_EPS) * kvn_ref[...]).astype(BF16)
    cos = cos_ref[...]
    sin = sin_ref[...]
    kpe = (lat_ref[:, 768:896].astype(F32) * cos + lat_ref[:, 896:1024].astype(F32) * sin).astype(BF16)
    scale = (MLA_NOPE + MLA_ROPE) ** -0.5 * LOG2E
    ones = jnp.ones((VT_ROWS - MLA_V, lat_ref.shape[0]), BF16)
    for h in range(MLA_HEADS):
        a = jnp.dot(xq, wq_ref[:, h * 256:(h + 1) * 256], preferred_element_type=F32)
        rt = jnp.dot(xq, wrot_ref[:, h * 128:(h + 1) * 128], preferred_element_type=F32)
        q_ref[0, h, :, 0:128] = (a[:, 0:128] * scale).astype(BF16)
        q_ref[0, h, :, 128:256] = ((a[:, 128:256] * cos + rt * sin) * scale).astype(BF16)
        kn = jnp.dot(xkv, wk_ref[:, h * 128:(h + 1) * 128], preferred_element_type=F32)
        k_ref[0, h, :, 0:128] = kn.astype(BF16)
        k_ref[0, h, :, 128:256] = kpe
        vt = lax.dot_general(wvt_ref[h * 128:(h + 1) * 128, :], xkv, (((1,), (1,)), ((), ())),
                             preferred_element_type=F32)
        vt_ref[0, h, 0:MLA_V, :] = vt.astype(BF16)
        vt_ref[0, h, MLA_V:VT_ROWS, :] = ones


def _mla_up(proj, cos128, sin128, qn, kvn, wq, wrot, wk, wvt, B, S, tm=512):
    nt = S // tm
    lat_blk = COL_LAT // LAT_W
    const = lambda b, i: (0, 0)
    return pl.pallas_call(
        _mla_up_kernel,
        out_shape=(
            jax.ShapeDtypeStruct((B, MLA_HEADS, S, 256), BF16),
            jax.ShapeDtypeStruct((B, MLA_HEADS, S, 256), BF16),
            jax.ShapeDtypeStruct((B, MLA_HEADS, VT_ROWS, S), BF16),
        ),
        grid=(B, nt),
        in_specs=[
            pl.BlockSpec((tm, LAT_W), lambda b, i: (b * nt + i, lat_blk)),
            pl.BlockSpec((tm, 128), lambda b, i: (i, 0)),
            pl.BlockSpec((tm, 128), lambda b, i: (i, 0)),
            pl.BlockSpec((1, CQ_PAD), const),
            pl.BlockSpec((1, CKV_PAD), const),
            pl.BlockSpec((CQ_PAD, MLA_HEADS * 256), const),
            pl.BlockSpec((CQ_PAD, MLA_HEADS * 128), const),
            pl.BlockSpec((CKV_PAD, MLA_HEADS * MLA_NOPE), const),
            pl.BlockSpec((MLA_HEADS * MLA_V, CKV_PAD), const),
        ],
        out_specs=(
            pl.BlockSpec((1, MLA_HEADS, tm, 256), lambda b, i: (b, 0, i, 0)),
            pl.BlockSpec((1, MLA_HEADS, tm, 256), lambda b, i: (b, 0, i, 0)),
            pl.BlockSpec((1, MLA_HEADS, VT_ROWS, tm), lambda b, i: (b, 0, 0, i)),
        ),
        compiler_params=_cparams("parallel", "parallel"),
        name="mla_up",
    )(proj, cos128, sin128, qn, kvn, wq, wrot, wk, wvt)


def _mla_attn_kernel(q_ref, k_ref, vt_ref, o_ref, *, ck):
    q = q_ref[0, 0]
    n_chunks = k_ref.shape[2] // ck
    nt = (((1,), (1,)), ((), ()))
    m = acc = None

    def scores(c):
        return lax.dot_general(k_ref[0, 0, c * ck:(c + 1) * ck, :], q, nt, preferred_element_type=F32)

    s_next = scores(0)
    for c in range(n_chunks):
        s = s_next
        if c + 1 < n_chunks:
            s_next = scores(c + 1)
        mc = jnp.max(s, axis=0, keepdims=True)
        m_new = mc if c == 0 else jnp.maximum(m, mc)
        p = jnp.exp2(s - m_new).astype(BF16)
        pv = jnp.dot(vt_ref[0, 0, :, c * ck:(c + 1) * ck], p, preferred_element_type=F32)
        acc = pv if c == 0 else acc * jnp.exp2(m - m_new) + pv
        m = m_new
    o = acc[0:MLA_V] / acc[MLA_V:MLA_V + 1]
    o_ref[...] = o.T.astype(o_ref.dtype)


def _mla_attn(q, k, vt, B, S, tq=1024, ck=512):
    nq = S // tq
    return pl.pallas_call(
        functools.partial(_mla_attn_kernel, ck=ck),
        out_shape=jax.ShapeDtypeStruct((B * S, MLA_HEADS * MLA_V), BF16),
        grid=(B, MLA_HEADS, nq),
        in_specs=[
            pl.BlockSpec((1, 1, tq, 256), lambda b, h, i: (b, h, i, 0)),
            pl.BlockSpec((1, 1, S, 256), lambda b, h, i: (b, h, 0, 0)),
            pl.BlockSpec((1, 1, VT_ROWS, S), lambda b, h, i: (b, h, 0, 0)),
        ],
        out_specs=pl.BlockSpec((tq, MLA_V), lambda b, h, i: (b * nq + i, h)),
        compiler_params=_cparams("parallel", "parallel", "arbitrary"),
        name="mla_attn",
    )(q, k, vt)


def _fnet_ch_kernel(u_ref, w_ref, a_ref, b_ref):
    gd = FNET_GROUP_DIM
    for g in range(FNET_GROUPS):
        ab = jnp.dot(u_ref[:, g * gd:(g + 1) * gd], w_ref[...], preferred_element_type=F32)
        a_ref[:, g * gd:(g + 1) * gd] = ab[:, :gd].astype(a_ref.dtype)
        b_ref[:, g * gd:(g + 1) * gd] = ab[:, gd:].astype(b_ref.dtype)


def _fnet_channel(proj, w_cs, tm=1024):
    T = proj.shape[0]
    W = FNET_GROUPS * FNET_GROUP_DIM
    return pl.pallas_call(
        _fnet_ch_kernel,
        out_shape=(jax.ShapeDtypeStruct((T, W), BF16), jax.ShapeDtypeStruct((T, W), BF16)),
        grid=(T // tm,),
        in_specs=[
            pl.BlockSpec((tm, W), lambda i: (i, COL_UF // W)),
            pl.BlockSpec((FNET_GROUP_DIM, 2 * FNET_GROUP_DIM), lambda i: (0, 0)),
        ],
        out_specs=(pl.BlockSpec((tm, W), lambda i: (i, 0)), pl.BlockSpec((tm, W), lambda i: (i, 0))),
        compiler_params=_cparams("parallel"),
        name="fnet_channel",
    )(proj, w_cs)


def _fnet_pos_kernel(c_ref, s_ref, a_ref, b_ref, o_ref, *, scale):
    y = (jnp.dot(c_ref[...], a_ref[...], preferred_element_type=F32)
         + jnp.dot(s_ref[...], b_ref[...], preferred_element_type=F32))
    o_ref[...] = (y * scale).astype(o_ref.dtype)


def _fnet_position(c_tab, sn_tab, a, b, B, S, tm=512, tn=512):
    W = FNET_GROUPS * FNET_GROUP_DIM
    nm = S // tm
    scale = float((S * FNET_GROUP_DIM) ** -0.5)
    return pl.pallas_call(
        functools.partial(_fnet_pos_kernel, scale=scale),
        out_shape=jax.ShapeDtypeStruct((B * S, W), BF16),
        grid=(nm, B, W // tn),
        in_specs=[
            pl.BlockSpec((tm, S), lambda m, bb, n: (m, 0)),
            pl.BlockSpec((tm, S), lambda m, bb, n: (m, 0)),
            pl.BlockSpec((S, tn), lambda m, bb, n: (bb, n)),
            pl.BlockSpec((S, tn), lambda m, bb, n: (bb, n)),
        ],
        out_specs=pl.BlockSpec((tm, tn), lambda m, bb, n: (bb * nm + m, n)),
        compiler_params=_cparams("parallel", "parallel", "parallel"),
        name="fnet_position",
    )(c_tab, sn_tab, a, b)


def _dft_tables(S):
    gd = FNET_GROUP_DIM
    ck = (np.arange(gd)[:, None] * np.arange(gd)[None, :]) % gd
    ang = 2.0 * np.pi * ck / gd
    w_cs = jnp.asarray(np.concatenate([np.cos(ang), np.sin(ang)], axis=1), F32).astype(BF16)
    kn = (lax.broadcasted_iota(jnp.int32, (S, S), 0) * lax.broadcasted_iota(jnp.int32, (S, S), 1)) % S
    angs = kn.astype(F32) * (2.0 * np.pi / S)
    return w_cs, jnp.cos(angs).astype(BF16), (-jnp.sin(angs)).astype(BF16)


def _merge_kernel(yn_ref, ym_ref, yf_ref, w_ref, g0_ref, g1_ref, g2_ref, o_ref):
    acc = g0_ref[...].astype(F32) * jnp.dot(yn_ref[...], w_ref[0].astype(BF16), preferred_element_type=F32)
    acc += g1_ref[...].astype(F32) * jnp.dot(ym_ref[...], w_ref[1].astype(BF16), preferred_element_type=F32)
    acc += g2_ref[...].astype(F32) * jnp.dot(yf_ref[...], w_ref[2].astype(BF16), preferred_element_type=F32)
    o_ref[...] = acc.astype(o_ref.dtype)


def _merge(y_na, y_mla, y_f, w_branch, proj, tm=1024, tn=512):
    T = y_na.shape[0]
    ybs = pl.BlockSpec((tm, BRANCH_WIDTH), lambda i, j: (i, 0))

    def gate_spec(br):
        off = (COL_GATE + br * D_MODEL) // tn
        return pl.BlockSpec((tm, tn), lambda i, j: (i, off + j))

    return pl.pallas_call(
        _merge_kernel,
        out_shape=jax.ShapeDtypeStruct((T, D_MODEL), BF16),
        grid=(T // tm, D_MODEL // tn),
        in_specs=[ybs, ybs, ybs,
                  pl.BlockSpec((N_BRANCHES, BRANCH_WIDTH, tn), lambda i, j: (0, 0, j)),
                  gate_spec(0), gate_spec(1), gate_spec(2)],
        out_specs=pl.BlockSpec((tm, tn), lambda i, j: (i, j)),
        compiler_params=_cparams("parallel", "arbitrary"),
        name="merge",
    )(y_na, y_mla, y_f, w_branch, proj, proj, proj)


def _outproj_kernel(m_ref, w_ref, h_ref, o_ref):
    o_ref[...] = h_ref[...] + jnp.dot(m_ref[...], w_ref[...].astype(BF16), preferred_element_type=F32)


def _outproj(merged, w_o, h2, tm=1024, tn=512):
    T = merged.shape[0]
    return pl.pallas_call(
        _outproj_kernel,
        out_shape=jax.ShapeDtypeStruct((T, D_MODEL), F32),
        grid=(T // tm, D_MODEL // tn),
        in_specs=[
            pl.BlockSpec((tm, D_MODEL), lambda i, j: (i, 0)),
            pl.BlockSpec((D_MODEL, tn), lambda i, j: (0, j)),
            pl.BlockSpec((tm, tn), lambda i, j: (i, j)),
        ],
        out_specs=pl.BlockSpec((tm, tn), lambda i, j: (i, j)),
        compiler_params=_cparams("parallel", "arbitrary"),
        name="outproj",
    )(merged, w_o, h2)


def _router_kernel(x_ref, g_ref, wr_ref, xn_ref, aff_ref):
    x = x_ref[...]
    ms = jnp.mean(x * x, axis=-1, keepdims=True)
    xn = x * lax.rsqrt(ms + RMS_EPS) * g_ref[...]
    xn_ref[...] = xn.astype(xn_ref.dtype)
    logits = jnp.dot(xn, wr_ref[...], preferred_element_type=F32, precision=lax.Precision.HIGHEST)
    m = jnp.max(logits, axis=-1, keepdims=True)
    e = jnp.exp(logits - m)
    aff_ref[...] = e / jnp.sum(e, axis=-1, keepdims=True)


def _router(h2, gain, w_router, tm=1024):
    T = h2.shape[0]
    return pl.pallas_call(
        _router_kernel,
        out_shape=(jax.ShapeDtypeStruct((T, D_MODEL), BF16), jax.ShapeDtypeStruct((T, N_EXPERTS), F32)),
        grid=(T // tm,),
        in_specs=[
            pl.BlockSpec((tm, D_MODEL), lambda i: (i, 0)),
            pl.BlockSpec((1, D_MODEL), lambda i: (0, 0)),
            pl.BlockSpec((D_MODEL, N_EXPERTS), lambda i: (0, 0)),
        ],
        out_specs=(pl.BlockSpec((tm, D_MODEL), lambda i: (i, 0)), pl.BlockSpec((tm, N_EXPERTS), lambda i: (i, 0))),
        compiler_params=_cparams("parallel"),
        name="router",
    )(h2, gain, w_router)


def _expert_kernel(x_ref, wg_ref, wu_ref, wd_ref, gate_ref, o_ref, acc_ref):
    f = pl.program_id(2)
    x = x_ref[...]
    a = jnp.dot(x, wg_ref[0].astype(BF16), preferred_element_type=F32)
    u = jnp.dot(x, wu_ref[0].astype(BF16), preferred_element_type=F32)
    hid = (a * (1.0 / (1.0 + jnp.exp(-a))) * u).astype(BF16)
    y = jnp.dot(hid, wd_ref[0].astype(BF16), preferred_element_type=F32)

    @pl.when(f == 0)
    def _():
        acc_ref[...] = y

    @pl.when(f > 0)
    def _():
        acc_ref[...] += y

    @pl.when(f == pl.num_programs(2) - 1)
    def _():
        o_ref[...] = (acc_ref[...] * gate_ref[...]).astype(o_ref.dtype)


def _experts(xe, wg, wu, wd, gate, tm=1024, tf=256):
    n_tok = xe.shape[0]
    per_e = n_tok // N_EXPERTS
    tm = min(tm, per_e)
    mt = per_e // tm
    return pl.pallas_call(
        _expert_kernel,
        out_shape=jax.ShapeDtypeStruct((n_tok, D_MODEL), BF16),
        grid=(N_EXPERTS, mt, EXPERT_FF // tf),
        in_specs=[
            pl.BlockSpec((tm, D_MODEL), lambda e, m, f: (e * mt + m, 0)),
            pl.BlockSpec((1, D_MODEL, tf), lambda e, m, f: (e, 0, f)),
            pl.BlockSpec((1, D_MODEL, tf), lambda e, m, f: (e, 0, f)),
            pl.BlockSpec((1, tf, D_MODEL), lambda e, m, f: (e, f, 0)),
            pl.BlockSpec((tm, 1), lambda e, m, f: (e * mt + m, 0)),
        ],
        out_specs=pl.BlockSpec((tm, D_MODEL), lambda e, m, f: (e * mt + m, 0)),
        scratch_shapes=[pltpu.VMEM((tm, D_MODEL), F32)],
        compiler_params=_cparams("parallel", "parallel", "arbitrary"),
        name="experts",
    )(xe, wg, wu, wd, gate)


def _final_norm_kernel(x_ref, g_ref, o_ref):
    x = x_ref[...]
    ms = jnp.mean(x * x, axis=-1, keepdims=True)
    o_ref[...] = x * lax.rsqrt(ms + RMS_EPS) * g_ref[...]


def _final_norm(h2, gain, tm=1024):
    T = h2.shape[0]
    return pl.pallas_call(
        _final_norm_kernel,
        out_shape=jax.ShapeDtypeStruct((T, D_MODEL), F32),
        grid=(T // tm,),
        in_specs=[pl.BlockSpec((tm, D_MODEL), lambda i: (i, 0)), pl.BlockSpec((1, D_MODEL), lambda i: (0, 0))],
        out_specs=pl.BlockSpec((tm, D_MODEL), lambda i: (i, 0)),
        compiler_params=_cparams("parallel"),
        name="final_norm",
    )(h2, gain)


def _pad_cols(w, n):
    return jnp.pad(w, ((0, 0), (0, n - w.shape[1])))


def _rot_cols(w):
    half = w.shape[1] // 2
    return jnp.concatenate([-w[:, half:], w[:, :half]], axis=1)


def _prep_in_weights(w_in, b_gate):
    o = np.cumsum([0, 3 * NA_WIDTH, MLA_Q_LORA, MLA_KV_LORA, MLA_ROPE, FNET_GROUPS * FNET_GROUP_DIM])
    qkv, cq, ckv, kr, uf, gl = (w_in[:, o[0]:o[1]], w_in[:, o[1]:o[2]], w_in[:, o[2]:o[3]], w_in[:, o[3]:o[4]],
                                w_in[:, o[4]:o[5]], w_in[:, o[5]:])
    w_all = jnp.concatenate([qkv, uf, _pad_cols(cq, CQ_PAD), _pad_cols(ckv, CKV_PAD), _pad_cols(kr, 128),
                             _pad_cols(_rot_cols(kr), 128), gl], axis=1).astype(BF16)
    b_all = jnp.concatenate([jnp.zeros((COL_GATE,), F32), b_gate]).reshape(1, PROJ_COLS)
    return w_all, b_all


def _prep_mla_weights(w_uq, q_norm, w_ukv, kv_norm):
    qd = MLA_NOPE + MLA_ROPE
    wq3 = w_uq.reshape(MLA_Q_LORA, MLA_HEADS, qd)
    nope, pe = wq3[:, :, :MLA_NOPE], wq3[:, :, MLA_NOPE:]
    z64 = jnp.zeros((MLA_Q_LORA, MLA_HEADS, 64), F32)
    wq = jnp.concatenate([nope, pe, z64], axis=2).reshape(MLA_Q_LORA, MLA_HEADS * 256)
    rot = jnp.concatenate([-pe[:, :, 32:], pe[:, :, :32], z64], axis=2).reshape(MLA_Q_LORA, MLA_HEADS * 128)
    rpad = ((0, CQ_PAD - MLA_Q_LORA), (0, 0))
    wq = jnp.pad(wq, rpad).astype(BF16)
    rot = jnp.pad(rot, rpad).astype(BF16)
    wkv3 = jnp.pad(w_ukv, ((0, CKV_PAD - MLA_KV_LORA), (0, 0))).reshape(CKV_PAD, MLA_HEADS, MLA_NOPE + MLA_V)
    wk = wkv3[:, :, :MLA_NOPE].reshape(CKV_PAD, MLA_HEADS * MLA_NOPE).astype(BF16)
    wvt = jnp.transpose(wkv3[:, :, MLA_NOPE:], (1, 2, 0)).reshape(MLA_HEADS * MLA_V, CKV_PAD).astype(BF16)
    qn = jnp.pad(q_norm, (0, CQ_PAD - MLA_Q_LORA)).reshape(1, CQ_PAD)
    kvn = jnp.pad(kv_norm, (0, CKV_PAD - MLA_KV_LORA)).reshape(1, CKV_PAD)
    return wq, rot, wk, wvt, qn, kvn


def _rope_tables128(S):
    pos = jnp.arange(S, dtype=F32)
    inv = 1.0 / (ROPE_THETA ** (jnp.arange(0, MLA_ROPE, 2, dtype=F32) / MLA_ROPE))
    ang = pos[:, None] * inv[None, :]
    z = jnp.zeros((S, 64), F32)
    cos, sin = jnp.cos(ang), jnp.sin(ang)
    return jnp.concatenate([cos, cos, z], axis=1), jnp.concatenate([sin, sin, z], axis=1)


def kernel(x, w_in, b_gate, w_uq, q_norm, w_ukv, kv_norm, na_rpb, w_branch, w_o, norm_mix, norm_moe,
           w_router, w_exp_gate, w_exp_up, w_exp_down, norm_final):
    B, S, D = x.shape
    T = B * S
    depth = w_in.shape[0]
    cap = CAPACITY_FACTOR * S // N_EXPERTS
    cos128, sin128 = _rope_tables128(S)
    w_cs, c_tab, sn_tab = _dft_tables(S)
    h = x.reshape(T, D)
    for l in range(depth):
        w_all, b_all = _prep_in_weights(w_in[l], b_gate[l])
        proj = _inproj(h, norm_mix[l].reshape(1, D), w_all, b_all)
        y_na = _na_attention(proj, _na_bias_tables(na_rpb[l]), B, S)
        wq, wrot, wk, wvt, qn, kvn = _prep_mla_weights(w_uq[l], q_norm[l], w_ukv[l], kv_norm[l])
        q, k, vt = _mla_up(proj, cos128, sin128, qn, kvn, wq, wrot, wk, wvt, B, S)
        y_mla = _mla_attn(q, k, vt, B, S)
        fa, fb = _fnet_channel(proj, w_cs)
        y_f = _fnet_position(c_tab, sn_tab, fa, fb, B, S)
        merged = _merge(y_na, y_mla, y_f, w_branch[l], proj)
        h = _outproj(merged, w_o[l], h)
        xn, aff = _router(h, norm_moe[l].reshape(1, D), w_router[l])
        gate, idx = lax.top_k(jnp.swapaxes(aff.reshape(B, S, N_EXPERTS), 1, 2), cap)
        flat = (idx + (jnp.arange(B, dtype=idx.dtype) * S)[:, None, None])
        flat = jnp.swapaxes(flat, 0, 1).reshape(-1)
        gate_e = jnp.swapaxes(gate, 0, 1).reshape(-1, 1)
        xe = xn[flat]
        ye = _experts(xe, w_exp_gate[l], w_exp_up[l], w_exp_down[l], gate_e)
        h = h.at[flat].add(ye.astype(F32))
    return _final_norm(h, norm_final.reshape(1, D)).reshape(B, S, D)
```

```python
import functools

import numpy as np
import jax
import jax.numpy as jnp
from jax import lax
from jax.experimental import pallas as pl
from jax.experimental.pallas import tpu as pltpu

D_MODEL = 2048
GRID_W = 64
NA_HEADS = 16
NA_HEAD_DIM = 64
NA_WIDTH = NA_HEADS * NA_HEAD_DIM
NA_KH = 8
NA_KW = 16
MLA_HEADS = 8
MLA_NOPE = 128
MLA_ROPE = 64
MLA_V = 128
MLA_Q_LORA = 448
MLA_KV_LORA = 160
ROPE_THETA = 10000.0
FNET_GROUPS = 4
FNET_GROUP_DIM = 256
N_BRANCHES = 3
BRANCH_WIDTH = 1024
N_EXPERTS = 16
EXPERT_FF = 2048
CAPACITY_FACTOR = 2
RMS_EPS = 1e-6
NEG_INF = -1e30

F32 = jnp.float32
BF16 = jnp.bfloat16

COL_QKV = 0
COL_UF = 3072
COL_LAT = 4096
COL_GATE = 5120
PROJ_COLS = COL_GATE + N_BRANCHES * D_MODEL
LAT_W = 1024
CQ_PAD = 512
CKV_PAD = 256
VT_ROWS = MLA_V + 16
LOG2E = 1.4426950408889634

VMEM_LIMIT = 56 * 1024 * 1024


def _cparams(*sem):
    return pltpu.CompilerParams(dimension_semantics=sem, vmem_limit_bytes=VMEM_LIMIT)


def _inproj_kernel(x_ref, g_ref, w_ref, b_ref, o_ref, xn_ref, *, gate_tile0):
    j = pl.program_id(1)

    @pl.when(j == 0)
    def _():
        x = x_ref[...]
        ms = jnp.mean(x * x, axis=-1, keepdims=True)
        xn_ref[...] = (x * lax.rsqrt(ms + RMS_EPS) * g_ref[...]).astype(BF16)

    acc = jnp.dot(xn_ref[...], w_ref[...], preferred_element_type=F32)

    @pl.when(j < gate_tile0)
    def _():
        o_ref[...] = acc.astype(o_ref.dtype)

    @pl.when(j >= gate_tile0)
    def _():
        z = acc + b_ref[...]
        o_ref[...] = (0.5 * jnp.tanh(0.5 * z) + 0.5).astype(o_ref.dtype)


def _inproj(h2, gain, w_all, b_all, tm=1024, tn=1024):
    T = h2.shape[0]
    return pl.pallas_call(
        functools.partial(_inproj_kernel, gate_tile0=COL_GATE // tn),
        out_shape=jax.ShapeDtypeStruct((T, PROJ_COLS), BF16),
        grid=(T // tm, PROJ_COLS // tn),
        in_specs=[
            pl.BlockSpec((tm, D_MODEL), lambda i, j: (i, 0)),
            pl.BlockSpec((1, D_MODEL), lambda i, j: (0, 0)),
            pl.BlockSpec((D_MODEL, tn), lambda i, j: (0, j)),
            pl.BlockSpec((1, tn), lambda i, j: (0, j)),
        ],
        out_specs=pl.BlockSpec((tm, tn), lambda i, j: (i, j)),
        scratch_shapes=[pltpu.VMEM((tm, D_MODEL), BF16)],
        compiler_params=_cparams("parallel", "arbitrary"),
        name="inproj",
    )(h2, gain, w_all, b_all)


def _na_kernel(q_ref, k_ref, v_ref, bias_a_ref, bias_b_ref, o_ref, *, rows):
    i = pl.program_id(1)
    nk = NA_KH * GRID_W
    lo = lax.broadcasted_iota(jnp.int32, (GRID_W, 128), 1) < NA_HEAD_DIM
    nt = (((1,), (1,)), ((), ()))
    row0 = []
    for rr in range(2):
        start = jnp.clip(2 * i + rr - NA_KH // 2, 0, rows - NA_KH)
        row0.append(pl.multiple_of(start * GRID_W, GRID_W))

    def scores(rr, hp):
        sl = slice(hp * 128, (hp + 1) * 128)
        q2 = q_ref[rr * GRID_W:(rr + 1) * GRID_W, sl]
        zero = jnp.zeros_like(q2)
        qq = jnp.concatenate([jnp.where(lo, q2, zero), jnp.where(lo, zero, q2)], axis=0)
        return lax.dot_general(qq, k_ref[pl.ds(row0[rr], nk), sl], nt, preferred_element_type=F32)

    chains = [(rr, hp) for hp in range(NA_HEADS // 2) for rr in range(2)]
    s_next = scores(*chains[0])
    for n, (rr, hp) in enumerate(chains):
        s = s_next
        if n + 1 < len(chains):
            s_next = scores(*chains[n + 1])
        sl = slice(hp * 128, (hp + 1) * 128)
        bias_ref = bias_a_ref if rr == 0 else bias_b_ref
        s = s * (NA_HEAD_DIM ** -0.5 * LOG2E) + bias_ref[0, hp]
        m = jnp.max(s, axis=-1, keepdims=True)
        p = jnp.exp2(s - m)
        l = jnp.sum(p, axis=-1, keepdims=True)
        o = jnp.dot(p.astype(BF16), v_ref[pl.ds(row0[rr], nk), sl], preferred_element_type=F32) / l
        o_ref[rr * GRID_W:(rr + 1) * GRID_W, sl] = jnp.where(lo, o[:GRID_W], o[GRID_W:]).astype(o_ref.dtype)


def _na_bias_tables(rpb):
    cols = np.arange(GRID_W)
    col_start = np.clip(cols - NA_KW // 2, 0, GRID_W - NA_KW)
    col_valid = (cols[None, :] >= col_start[:, None]) & (cols[None, :] < col_start[:, None] + NA_KW)
    col_idx = np.clip(cols[None, :] - cols[:, None] + NA_KW - 1, 0, 2 * NA_KW - 2)
    onehot = (col_idx[None] == np.arange(2 * NA_KW - 1)[:, None, None]).astype(np.float32)
    toep = jnp.einsum("hdc,cqj->hdqj", rpb.astype(F32), jnp.asarray(onehot), precision=lax.Precision.HIGHEST)
    variants = []
    for t in range(NA_KH):
        b = toep[:, NA_KH - 1 - t:2 * NA_KH - 1 - t]
        b = jnp.where(col_valid[None, None], b, NEG_INF)
        variants.append(jnp.transpose(b, (0, 2, 1, 3)).reshape(NA_HEADS // 2, 2 * GRID_W, NA_KH * GRID_W))
    return (jnp.stack(variants) * LOG2E).astype(F32)


def _na_attention(proj, bias, B, S):
    rows = S // GRID_W
    assert rows >= NA_KH and rows % 2 == 0
    T = B * S
    half = rows // 2

    def bias_spec(rr):
        def bias_map(b, i):
            r = 2 * i + rr
            return (r - jnp.clip(r - NA_KH // 2, 0, rows - NA_KH), 0, 0, 0)
        return pl.BlockSpec((1, NA_HEADS // 2, 2 * GRID_W, NA_KH * GRID_W), bias_map)

    return pl.pallas_call(
        functools.partial(_na_kernel, rows=rows),
        out_shape=jax.ShapeDtypeStruct((T, NA_WIDTH), BF16),
        grid=(B, half),
        in_specs=[
            pl.BlockSpec((2 * GRID_W, NA_WIDTH), lambda b, i: (b * half + i, 0)),
            pl.BlockSpec((S, NA_WIDTH), lambda b, i: (b, 1)),
            pl.BlockSpec((S, NA_WIDTH), lambda b, i: (b, 2)),
            bias_spec(0), bias_spec(1),
        ],
        out_specs=pl.BlockSpec((2 * GRID_W, NA_WIDTH), lambda b, i: (b * half + i, 0)),
        compiler_params=_cparams("parallel", "arbitrary"),
        name="na_attn",
    )(proj, proj, proj, bias, bias)


def _mla_up_kernel(lat_ref, cos_ref, sin_ref, qn_ref, kvn_ref, wq_ref, wrot_ref, wk_ref, wvt_ref,
                   q_ref, k_ref, vt_ref):
    cq = lat_ref[:, 0:CQ_PAD].astype(F32)
    ms = jnp.sum(cq * cq, axis=-1, keepdims=True) * (1.0 / MLA_Q_LORA)
    xq = (cq * lax.rsqrt(ms + RMS_EPS) * qn_ref[...]).astype(BF16)
    ckv = lat_ref[:, CQ_PAD:CQ_PAD + CKV_PAD].astype(F32)
    ms2 = jnp.sum(ckv * ckv, axis=-1, keepdims=True) * (1.0 / MLA_KV_LORA)
    xkv = (ckv * lax.rsqrt(ms2 + RMS_EPS) * kvn_ref[...]).astype(BF16)
    cos = cos_ref[...]
    sin = sin_ref[...]
    kpe = (lat_ref[:, 768:896].astype(F32) * cos + lat_ref[:, 896:1024].astype(F32) * sin).astype(BF16)
    scale = (MLA_NOPE + MLA_ROPE) ** -0.5 * LOG2E
    ones = jnp.ones((VT_ROWS - MLA_V, lat_ref.shape[0]), BF16)
    for h in range(MLA_HEADS):
        a = jnp.dot(xq, wq_ref[:, h * 256:(h + 1) * 256], preferred_element_type=F32)
        rt = jnp.dot(xq, wrot_ref[:, h * 128:(h + 1) * 128], preferred_element_type=F32)
        q_ref[0, h, :, 0:128] = (a[:, 0:128] * scale).astype(BF16)
        q_ref[0, h, :, 128:256] = ((a[:, 128:256] * cos + rt * sin) * scale).astype(BF16)
        kn = jnp.dot(xkv, wk_ref[:, h * 128:(h + 1) * 128], preferred_element_type=F32)
        k_ref[0, h, :, 0:128] = kn.astype(BF16)
        k_ref[0, h, :, 128:256] = kpe
        vt = lax.dot_general(wvt_ref[h * 128:(h + 1) * 128, :], xkv, (((1,), (1,)), ((), ())),
                             preferred_element_type=F32)
        vt_ref[0, h, 0:MLA_V, :] = vt.astype(BF16)
        vt_ref[0, h, MLA_V:VT_ROWS, :] = ones


def _mla_up(proj, cos128, sin128, qn, kvn, wq, wrot, wk, wvt, B, S, tm=512):
    nt = S // tm
    lat_blk = COL_LAT // LAT_W
    const = lambda b, i: (0, 0)
    return pl.pallas_call(
        _mla_up_kernel,
        out_shape=(
            jax.ShapeDtypeStruct((B, MLA_HEADS, S, 256), BF16),
            jax.ShapeDtypeStruct((B, MLA_HEADS, S, 256), BF16),
            jax.ShapeDtypeStruct((B, MLA_HEADS, VT_ROWS, S), BF16),
        ),
        grid=(B, nt),
        in_specs=[
            pl.BlockSpec((tm, LAT_W), lambda b, i: (b * nt + i, lat_blk)),
            pl.BlockSpec((tm, 128), lambda b, i: (i, 0)),
            pl.BlockSpec((tm, 128), lambda b, i: (i, 0)),
            pl.BlockSpec((1, CQ_PAD), const),
            pl.BlockSpec((1, CKV_PAD), const),
            pl.BlockSpec((CQ_PAD, MLA_HEADS * 256), const),
            pl.BlockSpec((CQ_PAD, MLA_HEADS * 128), const),
            pl.BlockSpec((CKV_PAD, MLA_HEADS * MLA_NOPE), const),
            pl.BlockSpec((MLA_HEADS * MLA_V, CKV_PAD), const),
        ],
        out_specs=(
            pl.BlockSpec((1, MLA_HEADS, tm, 256), lambda b, i: (b, 0, i, 0)),
            pl.BlockSpec((1, MLA_HEADS, tm, 256), lambda b, i: (b, 0, i, 0)),
            pl.BlockSpec((1, MLA_HEADS, VT_ROWS, tm), lambda b, i: (b, 0, 0, i)),
        ),
        compiler_params=_cparams("parallel", "parallel"),
        name="mla_up",
    )(proj, cos128, sin128, qn, kvn, wq, wrot, wk, wvt)


def _mla_attn_kernel(q_ref, k_ref, vt_ref, o_ref, *, ck):
    q = q_ref[0, 0]
    n_chunks = k_ref.shape[2] // ck
    nt = (((1,), (1,)), ((), ()))
    m = acc = None

    def scores(c):
        return lax.dot_general(k_ref[0, 0, c * ck:(c + 1) * ck, :], q, nt, preferred_element_type=F32)

    s_next = scores(0)
    for c in range(n_chunks):
        s = s_next
        if c + 1 < n_chunks:
            s_next = scores(c + 1)
        mc = jnp.max(s, axis=0, keepdims=True)
        m_new = mc if c == 0 else jnp.maximum(m, mc)
        p = jnp.exp2(s - m_new).astype(BF16)
        pv = jnp.dot(vt_ref[0, 0, :, c * ck:(c + 1) * ck], p, preferred_element_type=F32)
        acc = pv if c == 0 else acc * jnp.exp2(m - m_new) + pv
        m = m_new
    o = acc[0:MLA_V] / acc[MLA_V:MLA_V + 1]
    o_ref[...] = o.T.astype(o_ref.dtype)


def _mla_attn(q, k, vt, B, S, tq=1024, ck=512):
    nq = S // tq
    return pl.pallas_call(
        functools.partial(_mla_attn_kernel, ck=ck),
        out_shape=jax.ShapeDtypeStruct((B * S, MLA_HEADS * MLA_V), BF16),
        grid=(B, MLA_HEADS, nq),
        in_specs=[
            pl.BlockSpec((1, 1, tq, 256), lambda b, h, i: (b, h, i, 0)),
            pl.BlockSpec((1, 1, S, 256), lambda b, h, i: (b, h, 0, 0)),
            pl.BlockSpec((1, 1, VT_ROWS, S), lambda b, h, i: (b, h, 0, 0)),
        ],
        out_specs=pl.BlockSpec((tq, MLA_V), lambda b, h, i: (b * nq + i, h)),
        compiler_params=_cparams("parallel", "parallel", "arbitrary"),
        name="mla_attn",
    )(q, k, vt)


def _fnet_ch_kernel(u_ref, w_ref, a_ref, b_ref):
    gd = FNET_GROUP_DIM
    for g in range(FNET_GROUPS):
        ab = jnp.dot(u_ref[:, g * gd:(g + 1) * gd], w_ref[...], preferred_element_type=F32)
        a_ref[:, g * gd:(g + 1) * gd] = ab[:, :gd].astype(a_ref.dtype)
        b_ref[:, g * gd:(g + 1) * gd] = ab[:, gd:].astype(b_ref.dtype)


def _fnet_channel(proj, w_cs, tm=1024):
    T = proj.shape[0]
    W = FNET_GROUPS * FNET_GROUP_DIM
    return pl.pallas_call(
        _fnet_ch_kernel,
        out_shape=(jax.ShapeDtypeStruct((T, W), BF16), jax.ShapeDtypeStruct((T, W), BF16)),
        grid=(T // tm,),
        in_specs=[
            pl.BlockSpec((tm, W), lambda i: (i, COL_UF // W)),
            pl.BlockSpec((FNET_GROUP_DIM, 2 * FNET_GROUP_DIM), lambda i: (0, 0)),
        ],
        out_specs=(pl.BlockSpec((tm, W), lambda i: (i, 0)), pl.BlockSpec((tm, W), lambda i: (i, 0))),
        compiler_params=_cparams("parallel"),
        name="fnet_channel",
    )(proj, w_cs)


def _fnet_pos_kernel(c_ref, s_ref, a_ref, b_ref, o_ref, *, scale):
    y = (jnp.dot(c_ref[...], a_ref[...], preferred_element_type=F32)
         + jnp.dot(s_ref[...], b_ref[...], preferred_element_type=F32))
    o_ref[...] = (y * scale).astype(o_ref.dtype)


def _fnet_position(c_tab, sn_tab, a, b, B, S, tm=512, tn=512):
    W = FNET_GROUPS * FNET_GROUP_DIM
    nm = S // tm
    scale = float((S * FNET_GROUP_DIM) ** -0.5)
    return pl.pallas_call(
        functools.partial(_fnet_pos_kernel, scale=scale),
        out_shape=jax.ShapeDtypeStruct((B * S, W), BF16),
        grid=(nm, B, W // tn),
        in_specs=[
            pl.BlockSpec((tm, S), lambda m, bb, n: (m, 0)),
            pl.BlockSpec((tm, S), lambda m, bb, n: (m, 0)),
            pl.BlockSpec((S, tn), lambda m, bb, n: (bb, n)),
            pl.BlockSpec((S, tn), lambda m, bb, n: (bb, n)),
        ],
        out_specs=pl.BlockSpec((tm, tn), lambda m, bb, n: (bb * nm + m, n)),
        compiler_params=_cparams("parallel", "parallel", "parallel"),
        name="fnet_position",
    )(c_tab, sn_tab, a, b)


def _dft_tables(S):
    gd = FNET_GROUP_DIM
    ck = (np.arange(gd)[:, None] * np.arange(gd)[None, :]) % gd
    ang = 2.0 * np.pi * ck / gd
    w_cs = jnp.asarray(np.concatenate([np.cos(ang), np.sin(ang)], axis=1), F32).astype(BF16)
    kn = (lax.broadcasted_iota(jnp.int32, (S, S), 0) * lax.broadcasted_iota(jnp.int32, (S, S), 1)) % S
    angs = kn.astype(F32) * (2.0 * np.pi / S)
    return w_cs, jnp.cos(angs).astype(BF16), (-jnp.sin(angs)).astype(BF16)


def _merge_kernel(yn_ref, ym_ref, yf_ref, w_ref, g0_ref, g1_ref, g2_ref, o_ref):
    acc = g0_ref[...].astype(F32) * jnp.dot(yn_ref[...], w_ref[0, 0].astype(BF16), preferred_element_type=F32)
    acc += g1_ref[...].astype(F32) * jnp.dot(ym_ref[...], w_ref[0, 1].astype(BF16), preferred_element_type=F32)
    acc += g2_ref[...].astype(F32) * jnp.dot(yf_ref[...], w_ref[0, 2].astype(BF16), preferred_element_type=F32)
    o_ref[...] = acc.astype(o_ref.dtype)


def _merge(y_na, y_mla, y_f, w_branch, layer, proj, tm=1024, tn=512):
    T = y_na.shape[0]
    ybs = pl.BlockSpec((tm, BRANCH_WIDTH), lambda i, j: (i, 0))

    def gate_spec(br):
        off = (COL_GATE + br * D_MODEL) // tn
        return pl.BlockSpec((tm, tn), lambda i, j: (i, off + j))

    return pl.pallas_call(
        _merge_kernel,
        out_shape=jax.ShapeDtypeStruct((T, D_MODEL), BF16),
        grid=(T // tm, D_MODEL // tn),
        in_specs=[ybs, ybs, ybs,
                  pl.BlockSpec((1, N_BRANCHES, BRANCH_WIDTH, tn), lambda i, j: (layer, 0, 0, j)),
                  gate_spec(0), gate_spec(1), gate_spec(2)],
        out_specs=pl.BlockSpec((tm, tn), lambda i, j: (i, j)),
        compiler_params=_cparams("parallel", "arbitrary"),
        name="merge",
    )(y_na, y_mla, y_f, w_branch, proj, proj, proj)


def _outproj_kernel(m_ref, w_ref, h_ref, o_ref):
    o_ref[...] = h_ref[...] + jnp.dot(m_ref[...], w_ref[0].astype(BF16), preferred_element_type=F32)


def _outproj(merged, w_o, layer, h2, tm=1024, tn=512):
    T = merged.shape[0]
    return pl.pallas_call(
        _outproj_kernel,
        out_shape=jax.ShapeDtypeStruct((T, D_MODEL), F32),
        grid=(T // tm, D_MODEL // tn),
        in_specs=[
            pl.BlockSpec((tm, D_MODEL), lambda i, j: (i, 0)),
            pl.BlockSpec((1, D_MODEL, tn), lambda i, j: (layer, 0, j)),
            pl.BlockSpec((tm, tn), lambda i, j: (i, j)),
        ],
        out_specs=pl.BlockSpec((tm, tn), lambda i, j: (i, j)),
        compiler_params=_cparams("parallel", "arbitrary"),
        name="outproj",
    )(merged, w_o, h2)


def _router_kernel(x_ref, g_ref, wr_ref, xn_ref, aff_ref):
    x = x_ref[...]
    ms = jnp.mean(x * x, axis=-1, keepdims=True)
    xn = x * lax.rsqrt(ms + RMS_EPS) * g_ref[...]
    xn_ref[...] = xn.astype(xn_ref.dtype)
    logits = jnp.dot(xn, wr_ref[...], preferred_element_type=F32, precision=lax.Precision.HIGHEST)
    m = jnp.max(logits, axis=-1, keepdims=True)
    e = jnp.exp(logits - m)
    aff_ref[...] = e / jnp.sum(e, axis=-1, keepdims=True)


def _router(h2, gain, w_router, tm=1024):
    T = h2.shape[0]
    return pl.pallas_call(
        _router_kernel,
        out_shape=(jax.ShapeDtypeStruct((T, D_MODEL), BF16), jax.ShapeDtypeStruct((T, N_EXPERTS), F32)),
        grid=(T // tm,),
        in_specs=[
            pl.BlockSpec((tm, D_MODEL), lambda i: (i, 0)),
            pl.BlockSpec((1, D_MODEL), lambda i: (0, 0)),
            pl.BlockSpec((D_MODEL, N_EXPERTS), lambda i: (0, 0)),
        ],
        out_specs=(pl.BlockSpec((tm, D_MODEL), lambda i: (i, 0)), pl.BlockSpec((tm, N_EXPERTS), lambda i: (i, 0))),
        compiler_params=_cparams("parallel"),
        name="router",
    )(h2, gain, w_router)


def _expert_kernel(x_ref, wg_ref, wu_ref, wd_ref, gate_ref, o_ref, acc_ref):
    f = pl.program_id(2)
    x = x_ref[...]
    a = jnp.dot(x, wg_ref[0, 0].astype(BF16), preferred_element_type=F32)
    u = jnp.dot(x, wu_ref[0, 0].astype(BF16), preferred_element_type=F32)
    hid = (a * (1.0 / (1.0 + jnp.exp(-a))) * u).astype(BF16)
    y = jnp.dot(hid, wd_ref[0, 0].astype(BF16), preferred_element_type=F32)

    @pl.when(f == 0)
    def _():
        acc_ref[...] = y

    @pl.when(f > 0)
    def _():
        acc_ref[...] += y

    @pl.when(f == pl.num_programs(2) - 1)
    def _():
        o_ref[...] = (acc_ref[...] * gate_ref[...]).astype(o_ref.dtype)


def _experts(xe, wg, wu, wd, layer, gate, tm=1024, tf=256):
    n_tok = xe.shape[0]
    per_e = n_tok // N_EXPERTS
    tm = min(tm, per_e)
    mt = per_e // tm
    return pl.pallas_call(
        _expert_kernel,
        out_shape=jax.ShapeDtypeStruct((n_tok, D_MODEL), BF16),
        grid=(N_EXPERTS, mt, EXPERT_FF // tf),
        in_specs=[
            pl.BlockSpec((tm, D_MODEL), lambda e, m, f: (e * mt + m, 0)),
            pl.BlockSpec((1, 1, D_MODEL, tf), lambda e, m, f: (layer, e, 0, f)),
            pl.BlockSpec((1, 1, D_MODEL, tf), lambda e, m, f: (layer, e, 0, f)),
            pl.BlockSpec((1, 1, tf, D_MODEL), lambda e, m, f: (layer, e, f, 0)),
            pl.BlockSpec((tm, 1), lambda e, m, f: (e * mt + m, 0)),
        ],
        out_specs=pl.BlockSpec((tm, D_MODEL), lambda e, m, f: (e * mt + m, 0)),
        scratch_shapes=[pltpu.VMEM((tm, D_MODEL), F32)],
        compiler_params=_cparams("parallel", "parallel", "arbitrary"),
        name="experts",
    )(xe, wg, wu, wd, gate)


def _final_norm_kernel(x_ref, g_ref, o_ref):
    x = x_ref[...]
    ms = jnp.mean(x * x, axis=-1, keepdims=True)
    o_ref[...] = x * lax.rsqrt(ms + RMS_EPS) * g_ref[...]


def _final_norm(h2, gain, tm=1024):
    T = h2.shape[0]
    return pl.pallas_call(
        _final_norm_kernel,
        out_shape=jax.ShapeDtypeStruct((T, D_MODEL), F32),
        grid=(T // tm,),
        in_specs=[pl.BlockSpec((tm, D_MODEL), lambda i: (i, 0)), pl.BlockSpec((1, D_MODEL), lambda i: (0, 0))],
        out_specs=pl.BlockSpec((tm, D_MODEL), lambda i: (i, 0)),
        compiler_params=_cparams("parallel"),
        name="final_norm",
    )(h2, gain)


def _pad_cols(w, n):
    return jnp.pad(w, ((0, 0), (0, n - w.shape[1])))


def _rot_cols(w):
    half = w.shape[1] // 2
    return jnp.concatenate([-w[:, half:], w[:, :half]], axis=1)


def _prep_in_weights(w_in, b_gate):
    o = np.cumsum([0, 3 * NA_WIDTH, MLA_Q_LORA, MLA_KV_LORA, MLA_ROPE, FNET_GROUPS * FNET_GROUP_DIM])
    qkv, cq, ckv, kr, uf, gl = (w_in[:, o[0]:o[1]], w_in[:, o[1]:o[2]], w_in[:, o[2]:o[3]], w_in[:, o[3]:o[4]],
                                w_in[:, o[4]:o[5]], w_in[:, o[5]:])
    w_all = jnp.concatenate([qkv, uf, _pad_cols(cq, CQ_PAD), _pad_cols(ckv, CKV_PAD), _pad_cols(kr, 128),
                             _pad_cols(_rot_cols(kr), 128), gl], axis=1).astype(BF16)
    b_all = jnp.concatenate([jnp.zeros((COL_GATE,), F32), b_gate]).reshape(1, PROJ_COLS)
    return w_all, b_all


def _prep_mla_weights(w_uq, q_norm, w_ukv, kv_norm):
    qd = MLA_NOPE + MLA_ROPE
    wq3 = w_uq.reshape(MLA_Q_LORA, MLA_HEADS, qd)
    nope, pe = wq3[:, :, :MLA_NOPE], wq3[:, :, MLA_NOPE:]
    z64 = jnp.zeros((MLA_Q_LORA, MLA_HEADS, 64), F32)
    wq = jnp.concatenate([nope, pe, z64], axis=2).reshape(MLA_Q_LORA, MLA_HEADS * 256)
    rot = jnp.concatenate([-pe[:, :, 32:], pe[:, :, :32], z64], axis=2).reshape(MLA_Q_LORA, MLA_HEADS * 128)
    rpad = ((0, CQ_PAD - MLA_Q_LORA), (0, 0))
    wq = jnp.pad(wq, rpad).astype(BF16)
    rot = jnp.pad(rot, rpad).astype(BF16)
    wkv3 = jnp.pad(w_ukv, ((0, CKV_PAD - MLA_KV_LORA), (0, 0))).reshape(CKV_PAD, MLA_HEADS, MLA_NOPE + MLA_V)
    wk = wkv3[:, :, :MLA_NOPE].reshape(CKV_PAD, MLA_HEADS * MLA_NOPE).astype(BF16)
    wvt = jnp.transpose(wkv3[:, :, MLA_NOPE:], (1, 2, 0)).reshape(MLA_HEADS * MLA_V, CKV_PAD).astype(BF16)
    qn = jnp.pad(q_norm, (0, CQ_PAD - MLA_Q_LORA)).reshape(1, CQ_PAD)
    kvn = jnp.pad(kv_norm, (0, CKV_PAD - MLA_KV_LORA)).reshape(1, CKV_PAD)
    return wq, rot, wk, wvt, qn, kvn


def _rope_tables128(S):
    pos = jnp.arange(S, dtype=F32)
    inv = 1.0 / (ROPE_THETA ** (jnp.arange(0, MLA_ROPE, 2, dtype=F32) / MLA_ROPE))
    ang = pos[:, None] * inv[None, :]
    z = jnp.zeros((S, 64), F32)
    cos, sin = jnp.cos(ang), jnp.sin(ang)
    return jnp.concatenate([cos, cos, z], axis=1), jnp.concatenate([sin, sin, z], axis=1)


def kernel(x, w_in, b_gate, w_uq, q_norm, w_ukv, kv_norm, na_rpb, w_branch, w_o, norm_mix, norm_moe,
           w_router, w_exp_gate, w_exp_up, w_exp_down, norm_final):
    B, S, D = x.shape
    T = B * S
    depth = w_in.shape[0]
    cap = CAPACITY_FACTOR * S // N_EXPERTS
    cos128, sin128 = _rope_tables128(S)
    w_cs, c_tab, sn_tab = _dft_tables(S)
    h = x.reshape(T, D)
    for l in range(depth):
        w_all, b_all = _prep_in_weights(w_in[l], b_gate[l])
        proj = _inproj(h, norm_mix[l].reshape(1, D), w_all, b_all)
        y_na = _na_attention(proj, _na_bias_tables(na_rpb[l]), B, S)
        wq, wrot, wk, wvt, qn, kvn = _prep_mla_weights(w_uq[l], q_norm[l], w_ukv[l], kv_norm[l])
        q, k, vt = _mla_up(proj, cos128, sin128, qn, kvn, wq, wrot, wk, wvt, B, S)
        y_mla = _mla_attn(q, k, vt, B, S)
        fa, fb = _fnet_channel(proj, w_cs)
        y_f = _fnet_position(c_tab, sn_tab, fa, fb, B, S)
        merged = _merge(y_na, y_mla, y_f, w_branch, l, proj)
        h = _outproj(merged, w_o, l, h)
        xn, aff = _router(h, norm_moe[l].reshape(1, D), w_router[l])
        gate, idx = lax.top_k(jnp.swapaxes(aff.reshape(B, S, N_EXPERTS), 1, 2), cap)
        flat = (idx + (jnp.arange(B, dtype=idx.dtype) * S)[:, None, None])
        flat = jnp.swapaxes(flat, 0, 1).reshape(-1)
        gate_e = jnp.swapaxes(gate, 0, 1).reshape(-1, 1)
        xe = xn[flat]
        ye = _experts(xe, w_exp_gate, w_exp_up, w_exp_down, l, gate_e)
        h = h.at[flat].add(ye.astype(F32))
    return _final_norm(h, norm_final.reshape(1, D)).reshape(B, S, D)
```

```python
import functools

import numpy as np
import jax
import jax.numpy as jnp
from jax import lax
from jax.experimental import pallas as pl
from jax.experimental.pallas import tpu as pltpu

D_MODEL = 2048
GRID_W = 64
NA_HEADS = 16
NA_HEAD_DIM = 64
NA_WIDTH = NA_HEADS * NA_HEAD_DIM
NA_KH = 8
NA_KW = 16
MLA_HEADS = 8
MLA_NOPE = 128
MLA_ROPE = 64
MLA_V = 128
MLA_Q_LORA = 448
MLA_KV_LORA = 160
ROPE_THETA = 10000.0
FNET_GROUPS = 4
FNET_GROUP_DIM = 256
N_BRANCHES = 3
BRANCH_WIDTH = 1024
N_EXPERTS = 16
EXPERT_FF = 2048
CAPACITY_FACTOR = 2
RMS_EPS = 1e-6
NEG_INF = -1e30

F32 = jnp.float32
BF16 = jnp.bfloat16
I32 = jnp.int32

COL_QKV = 0
COL_UF = 3072
COL_LAT = 4096
COL_GATE = 5120
PROJ_COLS = COL_GATE + N_BRANCHES * D_MODEL
LAT_W = 1024
CQ_PAD = 512
CKV_PAD = 256
VT_ROWS = MLA_V + 16
LOG2E = 1.4426950408889634
TOK_BITS = 12
PFX_BLK = 512
COMB_TT = 256

VMEM_LIMIT = 56 * 1024 * 1024


def _cparams(*sem, **kw):
    return pltpu.CompilerParams(dimension_semantics=sem, vmem_limit_bytes=VMEM_LIMIT, **kw)


def _inproj_kernel(x_ref, g_ref, w_ref, b_ref, o_ref, xn_ref, *, gate_tile0):
    j = pl.program_id(1)

    @pl.when(j == 0)
    def _():
        x = x_ref[...]
        ms = jnp.mean(x * x, axis=-1, keepdims=True)
        xn_ref[...] = (x * lax.rsqrt(ms + RMS_EPS) * g_ref[...]).astype(BF16)

    acc = jnp.dot(xn_ref[...], w_ref[...], preferred_element_type=F32)

    @pl.when(j < gate_tile0)
    def _():
        o_ref[...] = acc.astype(o_ref.dtype)

    @pl.when(j >= gate_tile0)
    def _():
        z = acc + b_ref[...]
        o_ref[...] = (0.5 * jnp.tanh(0.5 * z) + 0.5).astype(o_ref.dtype)


def _inproj(h2, gain, w_all, b_all, tm=1024, tn=1024):
    T = h2.shape[0]
    return pl.pallas_call(
        functools.partial(_inproj_kernel, gate_tile0=COL_GATE // tn),
        out_shape=jax.ShapeDtypeStruct((T, PROJ_COLS), BF16),
        grid=(T // tm, PROJ_COLS // tn),
        in_specs=[
            pl.BlockSpec((tm, D_MODEL), lambda i, j: (i, 0)),
            pl.BlockSpec((1, D_MODEL), lambda i, j: (0, 0)),
            pl.BlockSpec((D_MODEL, tn), lambda i, j: (0, j)),
            pl.BlockSpec((1, tn), lambda i, j: (0, j)),
        ],
        out_specs=pl.BlockSpec((tm, tn), lambda i, j: (i, j)),
        scratch_shapes=[pltpu.VMEM((tm, D_MODEL), BF16)],
        compiler_params=_cparams("parallel", "arbitrary"),
        name="inproj",
    )(h2, gain, w_all, b_all)


def _na_kernel(q_ref, k_ref, v_ref, bias_a_ref, bias_b_ref, o_ref, *, rows):
    i = pl.program_id(1)
    nk = NA_KH * GRID_W
    lo = lax.broadcasted_iota(jnp.int32, (GRID_W, 128), 1) < NA_HEAD_DIM
    nt = (((1,), (1,)), ((), ()))
    row0 = []
    for rr in range(2):
        start = jnp.clip(2 * i + rr - NA_KH // 2, 0, rows - NA_KH)
        row0.append(pl.multiple_of(start * GRID_W, GRID_W))

    def scores(rr, hp):
        sl = slice(hp * 128, (hp + 1) * 128)
        q2 = q_ref[rr * GRID_W:(rr + 1) * GRID_W, sl]
        zero = jnp.zeros_like(q2)
        qq = jnp.concatenate([jnp.where(lo, q2, zero), jnp.where(lo, zero, q2)], axis=0)
        return lax.dot_general(qq, k_ref[pl.ds(row0[rr], nk), sl], nt, preferred_element_type=F32)

    chains = [(rr, hp) for hp in range(NA_HEADS // 2) for rr in range(2)]
    s_next = scores(*chains[0])
    for n, (rr, hp) in enumerate(chains):
        s = s_next
        if n + 1 < len(chains):
            s_next = scores(*chains[n + 1])
        sl = slice(hp * 128, (hp + 1) * 128)
        bias_ref = bias_a_ref if rr == 0 else bias_b_ref
        s = s * (NA_HEAD_DIM ** -0.5 * LOG2E) + bias_ref[0, hp]
        m = jnp.max(s, axis=-1, keepdims=True)
        p = jnp.exp2(s - m)
        l = jnp.sum(p, axis=-1, keepdims=True)
        o = jnp.dot(p.astype(BF16), v_ref[pl.ds(row0[rr], nk), sl], preferred_element_type=F32) / l
        o_ref[rr * GRID_W:(rr + 1) * GRID_W, sl] = jnp.where(lo, o[:GRID_W], o[GRID_W:]).astype(o_ref.dtype)


def _na_bias_tables(rpb):
    cols = np.arange(GRID_W)
    col_start = np.clip(cols - NA_KW // 2, 0, GRID_W - NA_KW)
    col_valid = (cols[None, :] >= col_start[:, None]) & (cols[None, :] < col_start[:, None] + NA_KW)
    col_idx = np.clip(cols[None, :] - cols[:, None] + NA_KW - 1, 0, 2 * NA_KW - 2)
    onehot = (col_idx[None] == np.arange(2 * NA_KW - 1)[:, None, None]).astype(np.float32)
    toep = jnp.einsum("hdc,cqj->hdqj", rpb.astype(F32), jnp.asarray(onehot), precision=lax.Precision.HIGHEST)
    variants = []
    for t in range(NA_KH):
        b = toep[:, NA_KH - 1 - t:2 * NA_KH - 1 - t]
        b = jnp.where(col_valid[None, None], b, NEG_INF)
        variants.append(jnp.transpose(b, (0, 2, 1, 3)).reshape(NA_HEADS // 2, 2 * GRID_W, NA_KH * GRID_W))
    return (jnp.stack(variants) * LOG2E).astype(F32)


def _na_attention(proj, bias, B, S):
    rows = S // GRID_W
    assert rows >= NA_KH and rows % 2 == 0
    T = B * S
    half = rows // 2

    def bias_spec(rr):
        def bias_map(b, i):
            r = 2 * i + rr
            return (r - jnp.clip(r - NA_KH // 2, 0, rows - NA_KH), 0, 0, 0)
        return pl.BlockSpec((1, NA_HEADS // 2, 2 * GRID_W, NA_KH * GRID_W), bias_map)

    return pl.pallas_call(
        functools.partial(_na_kernel, rows=rows),
        out_shape=jax.ShapeDtypeStruct((T, NA_WIDTH), BF16),
        grid=(B, half),
        in_specs=[
            pl.BlockSpec((2 * GRID_W, NA_WIDTH), lambda b, i: (b * half + i, 0)),
            pl.BlockSpec((S, NA_WIDTH), lambda b, i: (b, 1)),
            pl.BlockSpec((S, NA_WIDTH), lambda b, i: (b, 2)),
            bias_spec(0), bias_spec(1),
        ],
        out_specs=pl.BlockSpec((2 * GRID_W, NA_WIDTH), lambda b, i: (b * half + i, 0)),
        compiler_params=_cparams("parallel", "arbitrary"),
        name="na_attn",
    )(proj, proj, proj, bias, bias)


def _mla_up_kernel(lat_ref, cos_ref, sin_ref, cost_ref, sint_ref, qn_ref, kvn_ref, wq_ref, wrot_ref, wk_ref,
                   wvt_ref, q_ref, k_ref, vt_ref):
    cq = lat_ref[:, 0:CQ_PAD].astype(F32)
    ms = jnp.sum(cq * cq, axis=-1, keepdims=True) * (1.0 / MLA_Q_LORA)
    xq = (cq * lax.rsqrt(ms + RMS_EPS) * qn_ref[...]).astype(BF16)
    ckv = lat_ref[:, CQ_PAD:CQ_PAD + CKV_PAD].astype(F32)
    ms2 = jnp.sum(ckv * ckv, axis=-1, keepdims=True) * (1.0 / MLA_KV_LORA)
    xkv = (ckv * lax.rsqrt(ms2 + RMS_EPS) * kvn_ref[...]).astype(BF16)
    cos = cos_ref[...]
    sin = sin_ref[...]
    kpe = (lat_ref[:, 768:896].astype(F32) * cos + lat_ref[:, 896:1024].astype(F32) * sin).astype(BF16)
    scale = (MLA_NOPE + MLA_ROPE) ** -0.5 * LOG2E
    ones = jnp.ones((VT_ROWS - MLA_V, lat_ref.shape[0]), BF16)
    nt = (((1,), (1,)), ((), ()))
    for h in range(MLA_HEADS):
        a = lax.dot_general(wq_ref[h * 256:(h + 1) * 256, :], xq, nt, preferred_element_type=F32)
        rt = lax.dot_general(wrot_ref[h * 128:(h + 1) * 128, :], xq, nt, preferred_element_type=F32)
        q_ref[0, h, 0:128, :] = (a[0:128] * scale).astype(BF16)
        q_ref[0, h, 128:256, :] = ((a[128:256] * cost_ref[...] + rt * sint_ref[...]) * scale).astype(BF16)
        kn = jnp.dot(xkv, wk_ref[:, h * 128:(h + 1) * 128], preferred_element_type=F32)
        k_ref[0, h, :, 0:128] = kn.astype(BF16)
        k_ref[0, h, :, 128:256] = kpe
        vt = lax.dot_general(wvt_ref[h * 128:(h + 1) * 128, :], xkv, nt, preferred_element_type=F32)
        vt_ref[0, h, 0:MLA_V, :] = vt.astype(BF16)
        vt_ref[0, h, MLA_V:VT_ROWS, :] = ones


def _mla_up(proj, cos128, sin128, cos_t, sin_t, qn, kvn, wq, wrot, wk, wvt, B, S, tm=512):
    nt = S // tm
    lat_blk = COL_LAT // LAT_W
    const = lambda b, i: (0, 0)
    return pl.pallas_call(
        _mla_up_kernel,
        out_shape=(
            jax.ShapeDtypeStruct((B, MLA_HEADS, 256, S), BF16),
            jax.ShapeDtypeStruct((B, MLA_HEADS, S, 256), BF16),
            jax.ShapeDtypeStruct((B, MLA_HEADS, VT_ROWS, S), BF16),
        ),
        grid=(B, nt),
        in_specs=[
            pl.BlockSpec((tm, LAT_W), lambda b, i: (b * nt + i, lat_blk)),
            pl.BlockSpec((tm, 128), lambda b, i: (i, 0)),
            pl.BlockSpec((tm, 128), lambda b, i: (i, 0)),
            pl.BlockSpec((128, tm), lambda b, i: (0, i)),
            pl.BlockSpec((128, tm), lambda b, i: (0, i)),
            pl.BlockSpec((1, CQ_PAD), const),
            pl.BlockSpec((1, CKV_PAD), const),
            pl.BlockSpec((MLA_HEADS * 256, CQ_PAD), const),
            pl.BlockSpec((MLA_HEADS * 128, CQ_PAD), const),
            pl.BlockSpec((CKV_PAD, MLA_HEADS * MLA_NOPE), const),
            pl.BlockSpec((MLA_HEADS * MLA_V, CKV_PAD), const),
        ],
        out_specs=(
            pl.BlockSpec((1, MLA_HEADS, 256, tm), lambda b, i: (b, 0, 0, i)),
            pl.BlockSpec((1, MLA_HEADS, tm, 256), lambda b, i: (b, 0, i, 0)),
            pl.BlockSpec((1, MLA_HEADS, VT_ROWS, tm), lambda b, i: (b, 0, 0, i)),
        ),
        compiler_params=_cparams("parallel", "parallel"),
        name="mla_up",
    )(proj, cos128, sin128, cos_t, sin_t, qn, kvn, wq, wrot, wk, wvt)


def _mla_attn_kernel(q_ref, k_ref, vt_ref, o_ref, *, ck):
    qt = q_ref[0, 0]
    n_chunks = k_ref.shape[2] // ck
    m = acc = None

    def scores(c):
        return jnp.dot(k_ref[0, 0, c * ck:(c + 1) * ck, :], qt, preferred_element_type=F32)

    s_next = scores(0)
    for c in range(n_chunks):
        s = s_next
        if c + 1 < n_chunks:
            s_next = scores(c + 1)
        mc = jnp.max(s, axis=0, keepdims=True)
        m_new = mc if c == 0 else jnp.maximum(m, mc)
        p = jnp.exp2(s - m_new).astype(BF16)
        pv = jnp.dot(vt_ref[0, 0, :, c * ck:(c + 1) * ck], p, preferred_element_type=F32)
        acc = pv if c == 0 else acc * jnp.exp2(m - m_new) + pv
        m = m_new
    o = acc[0:MLA_V] / acc[MLA_V:MLA_V + 1]
    o_ref[...] = o.T.astype(o_ref.dtype)


def _mla_attn(q, k, vt, B, S, tq=1024, ck=512):
    nq = S // tq
    return pl.pallas_call(
        functools.partial(_mla_attn_kernel, ck=ck),
        out_shape=jax.ShapeDtypeStruct((B * S, MLA_HEADS * MLA_V), BF16),
        grid=(B, MLA_HEADS, nq),
        in_specs=[
            pl.BlockSpec((1, 1, 256, tq), lambda b, h, i: (b, h, 0, i)),
            pl.BlockSpec((1, 1, S, 256), lambda b, h, i: (b, h, 0, 0)),
            pl.BlockSpec((1, 1, VT_ROWS, S), lambda b, h, i: (b, h, 0, 0)),
        ],
        out_specs=pl.BlockSpec((tq, MLA_V), lambda b, h, i: (b * nq + i, h)),
        compiler_params=_cparams("parallel", "parallel", "arbitrary"),
        name="mla_attn",
    )(q, k, vt)


def _fnet_ch_kernel(u_ref, w_ref, a_ref, b_ref):
    gd = FNET_GROUP_DIM
    for g in range(FNET_GROUPS):
        ab = jnp.dot(u_ref[:, g * gd:(g + 1) * gd], w_ref[...], preferred_element_type=F32)
        a_ref[:, g * gd:(g + 1) * gd] = ab[:, :gd].astype(a_ref.dtype)
        b_ref[:, g * gd:(g + 1) * gd] = ab[:, gd:].astype(b_ref.dtype)


def _fnet_channel(proj, w_cs, tm=1024):
    T = proj.shape[0]
    W = FNET_GROUPS * FNET_GROUP_DIM
    return pl.pallas_call(
        _fnet_ch_kernel,
        out_shape=(jax.ShapeDtypeStruct((T, W), BF16), jax.ShapeDtypeStruct((T, W), BF16)),
        grid=(T // tm,),
        in_specs=[
            pl.BlockSpec((tm, W), lambda i: (i, COL_UF // W)),
            pl.BlockSpec((FNET_GROUP_DIM, 2 * FNET_GROUP_DIM), lambda i: (0, 0)),
        ],
        out_specs=(pl.BlockSpec((tm, W), lambda i: (i, 0)), pl.BlockSpec((tm, W), lambda i: (i, 0))),
        compiler_params=_cparams("parallel"),
        name="fnet_channel",
    )(proj, w_cs)


def _fnet_pos_kernel(c_ref, s_ref, a_ref, b_ref, o_ref, *, scale):
    y = (jnp.dot(c_ref[...], a_ref[...], preferred_element_type=F32)
         + jnp.dot(s_ref[...], b_ref[...], preferred_element_type=F32))
    o_ref[...] = (y * scale).astype(o_ref.dtype)


def _fnet_position(c_tab, sn_tab, a, b, B, S, tm=512, tn=512):
    W = FNET_GROUPS * FNET_GROUP_DIM
    nm = S // tm
    scale = float((S * FNET_GROUP_DIM) ** -0.5)
    return pl.pallas_call(
        functools.partial(_fnet_pos_kernel, scale=scale),
        out_shape=jax.ShapeDtypeStruct((B * S, W), BF16),
        grid=(nm, B, W // tn),
        in_specs=[
            pl.BlockSpec((tm, S), lambda m, bb, n: (m, 0)),
            pl.BlockSpec((tm, S), lambda m, bb, n: (m, 0)),
            pl.BlockSpec((S, tn), lambda m, bb, n: (bb, n)),
            pl.BlockSpec((S, tn), lambda m, bb, n: (bb, n)),
        ],
        out_specs=pl.BlockSpec((tm, tn), lambda m, bb, n: (bb * nm + m, n)),
        compiler_params=_cparams("parallel", "parallel", "parallel"),
        name="fnet_position",
    )(c_tab, sn_tab, a, b)


def _dft_tables(S):
    gd = FNET_GROUP_DIM
    ck = (np.arange(gd)[:, None] * np.arange(gd)[None, :]) % gd
    ang = 2.0 * np.pi * ck / gd
    w_cs = jnp.asarray(np.concatenate([np.cos(ang), np.sin(ang)], axis=1), F32).astype(BF16)
    kb = 64
    n = lax.broadcasted_iota(jnp.int32, (1, S), 1)

    def thin(rows, period):
        ang_ = ((lax.broadcasted_iota(jnp.int32, (rows, 1), 0) * n) % period).astype(F32) * (2.0 * np.pi / period)
        return jnp.cos(ang_)[:, None, :], jnp.sin(ang_)[:, None, :]

    c_hi, s_hi = thin(S // kb, S // kb)
    c_lo, s_lo = thin(kb, S)
    c_lo, s_lo = c_lo.reshape(1, kb, S), s_lo.reshape(1, kb, S)
    cos_t = (c_hi * c_lo - s_hi * s_lo).reshape(S, S)
    nsin_t = -(s_hi * c_lo + c_hi * s_lo).reshape(S, S)
    return w_cs, cos_t.astype(BF16), nsin_t.astype(BF16)


def _merge_kernel(yn_ref, ym_ref, yf_ref, w_ref, g0_ref, g1_ref, g2_ref, o_ref):
    acc = g0_ref[...].astype(F32) * jnp.dot(yn_ref[...], w_ref[0, 0].astype(BF16), preferred_element_type=F32)
    acc += g1_ref[...].astype(F32) * jnp.dot(ym_ref[...], w_ref[0, 1].astype(BF16), preferred_element_type=F32)
    acc += g2_ref[...].astype(F32) * jnp.dot(yf_ref[...], w_ref[0, 2].astype(BF16), preferred_element_type=F32)
    o_ref[...] = acc.astype(o_ref.dtype)


def _merge(y_na, y_mla, y_f, w_branch, layer, proj, tm=1024, tn=512):
    T = y_na.shape[0]
    ybs = pl.BlockSpec((tm, BRANCH_WIDTH), lambda i, j: (i, 0))

    def gate_spec(br):
        off = (COL_GATE + br * D_MODEL) // tn
        return pl.BlockSpec((tm, tn), lambda i, j: (i, off + j))

    return pl.pallas_call(
        _merge_kernel,
        out_shape=jax.ShapeDtypeStruct((T, D_MODEL), BF16),
        grid=(T // tm, D_MODEL // tn),
        in_specs=[ybs, ybs, ybs,
                  pl.BlockSpec((1, N_BRANCHES, BRANCH_WIDTH, tn), lambda i, j: (layer, 0, 0, j)),
                  gate_spec(0), gate_spec(1), gate_spec(2)],
        out_specs=pl.BlockSpec((tm, tn), lambda i, j: (i, j)),
        compiler_params=_cparams("parallel", "arbitrary"),
        name="merge",
    )(y_na, y_mla, y_f, w_branch, proj, proj, proj)


def _outproj_kernel(m_ref, w_ref, h_ref, o_ref):
    o_ref[...] = h_ref[...] + jnp.dot(m_ref[...], w_ref[0].astype(BF16), preferred_element_type=F32)


def _outproj(merged, w_o, layer, h2, tm=1024, tn=512):
    T = merged.shape[0]
    return pl.pallas_call(
        _outproj_kernel,
        out_shape=jax.ShapeDtypeStruct((T, D_MODEL), F32),
        grid=(T // tm, D_MODEL // tn),
        in_specs=[
            pl.BlockSpec((tm, D_MODEL), lambda i, j: (i, 0)),
            pl.BlockSpec((1, D_MODEL, tn), lambda i, j: (layer, 0, j)),
            pl.BlockSpec((tm, tn), lambda i, j: (i, j)),
        ],
        out_specs=pl.BlockSpec((tm, tn), lambda i, j: (i, j)),
        compiler_params=_cparams("parallel", "arbitrary"),
        name="outproj",
    )(merged, w_o, h2)


def _router_kernel(x_ref, g_ref, wrt_ref, aff_ref):
    x = x_ref[...]
    ms = jnp.mean(x * x, axis=-1, keepdims=True)
    xn = x * lax.rsqrt(ms + RMS_EPS) * g_ref[...]
    logits = lax.dot_general(wrt_ref[...], xn, (((1,), (1,)), ((), ())), preferred_element_type=F32,
                             precision=lax.Precision.HIGHEST)
    m = jnp.max(logits, axis=0, keepdims=True)
    e = jnp.exp(logits - m)
    aff_ref[0] = e / jnp.sum(e, axis=0, keepdims=True)


def _router(h2, gain, w_router_t, B, S, tm=1024):
    nt = S // tm
    return pl.pallas_call(
        _router_kernel,
        out_shape=jax.ShapeDtypeStruct((B, N_EXPERTS, S), F32),
        grid=(B, nt),
        in_specs=[
            pl.BlockSpec((tm, D_MODEL), lambda b, i: (b * nt + i, 0)),
            pl.BlockSpec((1, D_MODEL), lambda b, i: (0, 0)),
            pl.BlockSpec((N_EXPERTS, D_MODEL), lambda b, i: (0, 0)),
        ],
        out_specs=pl.BlockSpec((1, N_EXPERTS, tm), lambda b, i: (b, 0, i)),
        compiler_params=_cparams("parallel", "parallel"),
        name="router",
    )(h2, gain, w_router_t)


def _route_kernel(a_ref, val_ref, gate_ref, starts_ref, kmax_ref, pos_s, pv_s, a_s, *, cap, tt):
    E, S = a_ref.shape[1], a_ref.shape[2]
    a = a_ref[0]

    def as_float(b):
        return lax.bitcast_convert_type(b, F32)

    def bisect(_, carry):
        lo, hi = carry
        mid = lo + ((hi - lo + 1) >> 1)
        cnt = jnp.sum(jnp.where(a >= as_float(mid), 1.0, 0.0), axis=1, keepdims=True)
        ge = cnt >= cap
        return jnp.where(ge, mid, lo), jnp.where(ge, hi, mid - 1)

    lo0 = jnp.zeros((E, 1), I32)
    hi0 = jnp.full((E, 1), 0x7F7FFFFF, I32)
    thr_bits, _ = lax.fori_loop(0, 31, bisect, (lo0, hi0))
    thr, thr_up = as_float(thr_bits), as_float(thr_bits + 1)

    r_i = lax.broadcasted_iota(I32, (PFX_BLK, PFX_BLK), 0)
    c_i = lax.broadcasted_iota(I32, (PFX_BLK, PFX_BLK), 1)
    tri = jnp.where(r_i < c_i, 1.0, 0.0).astype(BF16)

    def excl_prefix(mask):
        x = jnp.where(mask, 1.0, 0.0).astype(BF16)
        carry = jnp.zeros((E, 1), F32)
        outs = []
        for j in range(S // PFX_BLK):
            blk = x[:, j * PFX_BLK:(j + 1) * PFX_BLK]
            outs.append(jnp.dot(blk, tri, preferred_element_type=F32) + carry)
            carry = carry + jnp.sum(blk.astype(F32), axis=1, keepdims=True)
        return jnp.concatenate(outs, axis=1)

    gt = a >= thr_up
    eq = (a >= thr) & (a < thr_up)
    need = cap - jnp.sum(jnp.where(gt, 1.0, 0.0), axis=1, keepdims=True)
    sel = gt | (eq & (excl_prefix(eq) < need))
    pos = excl_prefix(sel)

    self_ = jnp.where(sel, 1.0, 0.0)
    e_r = lax.broadcasted_iota(I32, (E, E), 0)
    e_c = lax.broadcasted_iota(I32, (E, E), 1)
    low = jnp.where(e_c < e_r, 1.0, 0.0).astype(BF16)
    rank = jnp.dot(low, self_.astype(BF16), preferred_element_type=F32)
    tok = lax.broadcasted_iota(I32, (E, S), 1)
    pos_s[...] = jnp.where(sel, pos.astype(I32), -1)
    pv_s[...] = (tok + (rank.astype(I32) << TOK_BITS)).astype(F32)
    a_s[...] = a

    lane = lax.broadcasted_iota(I32, (E, 128), 1)
    starts = jnp.full((E, 128), cap, I32)
    kcount = jnp.sum(self_, axis=0, keepdims=True)
    kmax = jnp.zeros((8, 128), I32)
    lane8 = lax.broadcasted_iota(I32, (8, 128), 1)
    posi = pos.astype(I32)
    for j in range(S // tt):
        starts = jnp.where(lane == j, posi[:, j * tt:j * tt + 1], starts)
        kj = jnp.max(kcount[:, j * tt:(j + 1) * tt], axis=1, keepdims=True).astype(I32)
        kmax = jnp.where(lane8 == j, kj, kmax)
    starts_ref[0] = starts
    kmax_ref[0] = kmax

    c_iota = lax.broadcasted_iota(I32, (cap, PFX_BLK), 0)
    lane_c = lax.broadcasted_iota(I32, (cap, 128), 1)

    def per_expert(e, carry):
        val_acc, gate_acc = carry
        v_part = jnp.zeros((cap, PFX_BLK), F32)
        g_part = jnp.zeros((cap, PFX_BLK), F32)
        for j in range(S // PFX_BLK):
            sl = pl.ds(j * PFX_BLK, PFX_BLK)
            hit = pos_s[pl.ds(e, 1), sl] == c_iota
            v_part = v_part + jnp.where(hit, pv_s[pl.ds(e, 1), sl], 0.0)
            g_part = g_part + jnp.where(hit, a_s[pl.ds(e, 1), sl], 0.0)
        v = jnp.sum(v_part, axis=1, keepdims=True).astype(I32)
        g = jnp.sum(g_part, axis=1, keepdims=True)
        return jnp.where(lane_c == e, v, val_acc), jnp.where(lane_c == e, g, gate_acc)

    val, gate = lax.fori_loop(0, E, per_expert, (jnp.zeros((cap, 128), I32), jnp.zeros((cap, 128), F32)))
    val_ref[0] = val
    gate_ref[0] = gate


def _route(aff_t, cap, tt=COMB_TT):
    B, E, S = aff_t.shape
    assert S <= (1 << TOK_BITS) and S // tt < 128 and S % PFX_BLK == 0
    return pl.pallas_call(
        functools.partial(_route_kernel, cap=cap, tt=tt),
        out_shape=(
            jax.ShapeDtypeStruct((B, cap, 128), I32),
            jax.ShapeDtypeStruct((B, cap, 128), F32),
            jax.ShapeDtypeStruct((B, E, 128), I32),
            jax.ShapeDtypeStruct((B, 8, 128), I32),
        ),
        grid=(B,),
        in_specs=[pl.BlockSpec((1, E, S), lambda b: (b, 0, 0))],
        out_specs=(
            pl.BlockSpec((1, cap, 128), lambda b: (b, 0, 0)),
            pl.BlockSpec((1, cap, 128), lambda b: (b, 0, 0)),
            pl.BlockSpec((1, E, 128), lambda b: (b, 0, 0)),
            pl.BlockSpec((1, 8, 128), lambda b: (b, 0, 0)),
        ),
        scratch_shapes=[pltpu.VMEM((E, S), I32), pltpu.VMEM((E, S), F32), pltpu.VMEM((E, S), F32)],
        compiler_params=_cparams("parallel"),
        name="route",
    )(aff_t)


def _gather_kernel(rows_ref, h_hbm, g_ref, o_ref, buf, sem, *, gt):
    base = pl.program_id(0) * gt

    def issue(r, _):
        pltpu.make_async_copy(h_hbm.at[pl.ds(rows_ref[base + r], 1)], buf.at[pl.ds(r, 1)], sem).start()
        return 0

    lax.fori_loop(0, gt, issue, 0, unroll=8)
    pltpu.make_async_copy(h_hbm.at[pl.ds(0, gt)], buf, sem).wait()
    x = buf[...]
    ms = jnp.mean(x * x, axis=-1, keepdims=True)
    o_ref[...] = (x * lax.rsqrt(ms + RMS_EPS) * g_ref[...]).astype(o_ref.dtype)


def _gather_norm(rows, h2, gain, gt=1024):
    n = rows.shape[0]
    gt = min(gt, n)
    return pl.pallas_call(
        functools.partial(_gather_kernel, gt=gt),
        out_shape=jax.ShapeDtypeStruct((n, D_MODEL), BF16),
        grid_spec=pltpu.PrefetchScalarGridSpec(
            num_scalar_prefetch=1,
            grid=(n // gt,),
            in_specs=[pl.BlockSpec(memory_space=pl.ANY), pl.BlockSpec((1, D_MODEL), lambda i, rows: (0, 0))],
            out_specs=pl.BlockSpec((gt, D_MODEL), lambda i, rows: (i, 0)),
            scratch_shapes=[pltpu.VMEM((gt, D_MODEL), F32), pltpu.SemaphoreType.DMA(())],
        ),
        compiler_params=_cparams("arbitrary", disable_bounds_checks=True),
        name="gather_norm",
    )(rows, h2, gain)


def _combine_kernel(dst_ref, starts_ref, kmax_ref, h_ref, ye_hbm, o_ref, planes, sem, *, B, E, cap, tt, nt_pad):
    b = pl.program_id(0)
    j = pl.program_id(1)
    k = kmax_ref[b * nt_pad + j]

    def zero(p, _):
        planes[pl.ds(pl.multiple_of(p * tt, tt), tt), :] = jnp.zeros((tt, D_MODEL), F32)
        return 0

    lax.fori_loop(0, k, zero, 0)

    def row_copy(slot, dst_row):
        return pltpu.make_async_copy(ye_hbm.at[pl.ds(slot, 1)], planes.at[pl.ds(dst_row, 1)], sem)

    def per_expert(e, n):
        sbase = (b * E + e) * nt_pad + j
        c0 = starts_ref[sbase]
        c1 = starts_ref[sbase + 1]
        lbase = (e * B + b) * cap

        def issue(c):
            row_copy(lbase + c, dst_ref[lbase + c]).start()

        def issue2(i, _):
            issue(c0 + 2 * i)
            issue(c0 + 2 * i + 1)
            return 0

        cnt = c1 - c0
        lax.fori_loop(0, cnt >> 1, issue2, 0)

        @pl.when((cnt & 1) == 1)
        def _():
            issue(c1 - 1)

        return n + cnt

    n = lax.fori_loop(0, E, per_expert, 0)

    for bit in range((E * tt).bit_length()):
        @pl.when(((n >> bit) & 1) == 1)
        def _():
            rows = 1 << bit
            pltpu.make_async_copy(ye_hbm.at[pl.ds(0, rows)], planes.at[pl.ds(0, rows)], sem).wait()

    for cs in range(D_MODEL // 128):
        cols = slice(cs * 128, (cs + 1) * 128)

        def add(p, acc):
            return acc + planes[pl.ds(pl.multiple_of(p * tt, tt), tt), cols]

        o_ref[:, cols] = lax.fori_loop(0, k, add, h_ref[:, cols])


def _combine(h2, ye, dst_rows, starts, kmax, B, S, cap, tt=COMB_TT):
    E = N_EXPERTS
    nt = S // tt
    nt_pad = starts.shape[0] // (B * E)
    return pl.pallas_call(
        functools.partial(_combine_kernel, B=B, E=E, cap=cap, tt=tt, nt_pad=nt_pad),
        out_shape=jax.ShapeDtypeStruct(h2.shape, F32),
        grid_spec=pltpu.PrefetchScalarGridSpec(
            num_scalar_prefetch=3,
            grid=(B, nt),
            in_specs=[pl.BlockSpec((tt, D_MODEL), lambda b, j, *_: (b * nt + j, 0)),
                      pl.BlockSpec(memory_space=pl.ANY)],
            out_specs=pl.BlockSpec((tt, D_MODEL), lambda b, j, *_: (b * nt + j, 0)),
            scratch_shapes=[pltpu.VMEM((E * tt, D_MODEL), F32), pltpu.SemaphoreType.DMA(())],
        ),
        compiler_params=_cparams("arbitrary", "arbitrary", disable_bounds_checks=True),
        name="combine",
    )(dst_rows, starts, kmax, h2, ye)


def _ffn_up_kernel(x_ref, wg_ref, wu_ref, o_ref):
    x = x_ref[...]
    a = jnp.dot(x, wg_ref[0, 0].astype(BF16), preferred_element_type=F32)
    u = jnp.dot(x, wu_ref[0, 0].astype(BF16), preferred_element_type=F32)
    o_ref[...] = (a * (1.0 / (1.0 + jnp.exp(-a))) * u).astype(o_ref.dtype)


def _ffn_down_kernel(hid_ref, wd_ref, gate_ref, o_ref):
    y = jnp.dot(hid_ref[...], wd_ref[0, 0].astype(BF16), preferred_element_type=F32)
    o_ref[...] = y * gate_ref[...]


def _experts(xe, wg, wu, wd, layer, gate, tf=256, tn=512):
    n_tok = xe.shape[0]
    per_e = n_tok // N_EXPERTS
    ff = wg.shape[-1]
    hid = pl.pallas_call(
        _ffn_up_kernel,
        out_shape=jax.ShapeDtypeStruct((n_tok, ff), BF16),
        grid=(N_EXPERTS, ff // tf),
        in_specs=[
            pl.BlockSpec((per_e, D_MODEL), lambda e, f: (e, 0)),
            pl.BlockSpec((1, 1, D_MODEL, tf), lambda e, f: (layer, e, 0, f)),
            pl.BlockSpec((1, 1, D_MODEL, tf), lambda e, f: (layer, e, 0, f)),
        ],
        out_specs=pl.BlockSpec((per_e, tf), lambda e, f: (e, f)),
        compiler_params=_cparams("parallel", "arbitrary"),
        name="ffn_up",
    )(xe, wg, wu)
    return pl.pallas_call(
        _ffn_down_kernel,
        out_shape=jax.ShapeDtypeStruct((n_tok, D_MODEL), F32),
        grid=(N_EXPERTS, D_MODEL // tn),
        in_specs=[
            pl.BlockSpec((per_e, ff), lambda e, n: (e, 0)),
            pl.BlockSpec((1, 1, ff, tn), lambda e, n: (layer, e, 0, n)),
            pl.BlockSpec((per_e, 1), lambda e, n: (e, 0)),
        ],
        out_specs=pl.BlockSpec((per_e, tn), lambda e, n: (e, n)),
        compiler_params=_cparams("parallel", "arbitrary"),
        name="ffn_down",
    )(hid, wd, gate)


def _moe_layer(h2, gain, w_router, wg, wu, wd, layer, B, S):
    E = N_EXPERTS
    cap = CAPACITY_FACTOR * S // E
    nt = S // COMB_TT
    aff_t = _router(h2, gain, jnp.transpose(w_router), B, S)
    val, gate, starts, kmax = _route(aff_t, cap)
    val_e = jnp.transpose(val[:, :, :E], (2, 0, 1))
    gate_e = jnp.transpose(gate[:, :, :E], (2, 0, 1)).reshape(-1, 1)
    tok_e = val_e & ((1 << TOK_BITS) - 1)
    rows = (tok_e + (jnp.arange(B, dtype=I32) * S)[None, :, None]).reshape(-1)
    xe = _gather_norm(rows, h2, gain)
    ye = _experts(xe, wg, wu, wd, layer, gate_e)
    dst = ((val_e >> TOK_BITS) * COMB_TT + tok_e % COMB_TT).reshape(-1)
    starts_flat = starts[:, :, :nt + 1].reshape(-1)
    kmax_flat = kmax[:, 0, :nt + 1].reshape(-1)
    return _combine(h2, ye, dst, starts_flat, kmax_flat, B, S, cap)


def _final_norm_kernel(x_ref, g_ref, o_ref):
    x = x_ref[...]
    ms = jnp.mean(x * x, axis=-1, keepdims=True)
    o_ref[...] = x * lax.rsqrt(ms + RMS_EPS) * g_ref[...]


def _final_norm(h2, gain, tm=1024):
    T = h2.shape[0]
    return pl.pallas_call(
        _final_norm_kernel,
        out_shape=jax.ShapeDtypeStruct((T, D_MODEL), F32),
        grid=(T // tm,),
        in_specs=[pl.BlockSpec((tm, D_MODEL), lambda i: (i, 0)), pl.BlockSpec((1, D_MODEL), lambda i: (0, 0))],
        out_specs=pl.BlockSpec((tm, D_MODEL), lambda i: (i, 0)),
        compiler_params=_cparams("parallel"),
        name="final_norm",
    )(h2, gain)


def _pad_cols(w, n):
    return jnp.pad(w, ((0, 0), (0, n - w.shape[1])))


def _rot_cols(w):
    half = w.shape[1] // 2
    return jnp.concatenate([-w[:, half:], w[:, :half]], axis=1)


def _prep_in_weights(w_in, b_gate):
    o = np.cumsum([0, 3 * NA_WIDTH, MLA_Q_LORA, MLA_KV_LORA, MLA_ROPE, FNET_GROUPS * FNET_GROUP_DIM])
    qkv, cq, ckv, kr, uf, gl = (w_in[:, o[0]:o[1]], w_in[:, o[1]:o[2]], w_in[:, o[2]:o[3]], w_in[:, o[3]:o[4]],
                                w_in[:, o[4]:o[5]], w_in[:, o[5]:])
    w_all = jnp.concatenate([qkv, uf, _pad_cols(cq, CQ_PAD), _pad_cols(ckv, CKV_PAD), _pad_cols(kr, 128),
                             _pad_cols(_rot_cols(kr), 128), gl], axis=1).astype(BF16)
    b_all = jnp.concatenate([jnp.zeros((COL_GATE,), F32), b_gate]).reshape(1, PROJ_COLS)
    return w_all, b_all


def _prep_mla_weights(w_uq, q_norm, w_ukv, kv_norm):
    qd = MLA_NOPE + MLA_ROPE
    wq3 = w_uq.reshape(MLA_Q_LORA, MLA_HEADS, qd)
    nope, pe = wq3[:, :, :MLA_NOPE], wq3[:, :, MLA_NOPE:]
    z64 = jnp.zeros((MLA_Q_LORA, MLA_HEADS, 64), F32)
    wq = jnp.concatenate([nope, pe, z64], axis=2).reshape(MLA_Q_LORA, MLA_HEADS * 256)
    rot = jnp.concatenate([-pe[:, :, 32:], pe[:, :, :32], z64], axis=2).reshape(MLA_Q_LORA, MLA_HEADS * 128)
    rpad = ((0, CQ_PAD - MLA_Q_LORA), (0, 0))
    wq = jnp.transpose(jnp.pad(wq, rpad)).astype(BF16)
    rot = jnp.transpose(jnp.pad(rot, rpad)).astype(BF16)
    wkv3 = jnp.pad(w_ukv, ((0, CKV_PAD - MLA_KV_LORA), (0, 0))).reshape(CKV_PAD, MLA_HEADS, MLA_NOPE + MLA_V)
    wk = wkv3[:, :, :MLA_NOPE].reshape(CKV_PAD, MLA_HEADS * MLA_NOPE).astype(BF16)
    wvt = jnp.transpose(wkv3[:, :, MLA_NOPE:], (1, 2, 0)).reshape(MLA_HEADS * MLA_V, CKV_PAD).astype(BF16)
    qn = jnp.pad(q_norm, (0, CQ_PAD - MLA_Q_LORA)).reshape(1, CQ_PAD)
    kvn = jnp.pad(kv_norm, (0, CKV_PAD - MLA_KV_LORA)).reshape(1, CKV_PAD)
    return wq, rot, wk, wvt, qn, kvn


def _rope_tables128(S):
    pos = jnp.arange(S, dtype=F32)
    inv = 1.0 / (ROPE_THETA ** (jnp.arange(0, MLA_ROPE, 2, dtype=F32) / MLA_ROPE))
    ang = pos[:, None] * inv[None, :]
    z = jnp.zeros((S, 64), F32)
    cos, sin = jnp.cos(ang), jnp.sin(ang)
    return jnp.concatenate([cos, cos, z], axis=1), jnp.concatenate([sin, sin, z], axis=1)


def kernel(x, w_in, b_gate, w_uq, q_norm, w_ukv, kv_norm, na_rpb, w_branch, w_o, norm_mix, norm_moe,
           w_router, w_exp_gate, w_exp_up, w_exp_down, norm_final):
    B, S, D = x.shape
    T = B * S
    depth = w_in.shape[0]
    cos128, sin128 = _rope_tables128(S)
    cos_t, sin_t = jnp.transpose(cos128), jnp.transpose(sin128)
    w_cs, c_tab, sn_tab = _dft_tables(S)
    h = x.reshape(T, D)
    for l in range(depth):
        w_all, b_all = _prep_in_weights(w_in[l], b_gate[l])
        proj = _inproj(h, norm_mix[l].reshape(1, D), w_all, b_all)
        y_na = _na_attention(proj, _na_bias_tables(na_rpb[l]), B, S)
        wq, wrot, wk, wvt, qn, kvn = _prep_mla_weights(w_uq[l], q_norm[l], w_ukv[l], kv_norm[l])
        q, k, vt = _mla_up(proj, cos128, sin128, cos_t, sin_t, qn, kvn, wq, wrot, wk, wvt, B, S)
        y_mla = _mla_attn(q, k, vt, B, S)
        fa, fb = _fnet_channel(proj, w_cs)
        y_f = _fnet_position(c_tab, sn_tab, fa, fb, B, S)
        merged = _merge(y_na, y_mla, y_f, w_branch, l, proj)
        h = _outproj(merged, w_o, l, h)
        h = _moe_layer(h, norm_moe[l].reshape(1, D), w_router[l], w_exp_gate, w_exp_up, w_exp_down, l, B, S)
    return _final_norm(h, norm_final.reshape(1, D)).reshape(B, S, D)
```

```python
import functools

import numpy as np
import jax
import jax.numpy as jnp
from jax import lax
from jax.experimental import pallas as pl
from jax.experimental.pallas import tpu as pltpu

D_MODEL = 2048
GRID_W = 64
NA_HEADS = 16
NA_HEAD_DIM = 64
NA_WIDTH = NA_HEADS * NA_HEAD_DIM
NA_KH = 8
NA_KW = 16
MLA_HEADS = 8
MLA_NOPE = 128
MLA_ROPE = 64
MLA_V = 128
MLA_Q_LORA = 448
MLA_KV_LORA = 160
ROPE_THETA = 10000.0
FNET_GROUPS = 4
FNET_GROUP_DIM = 256
N_BRANCHES = 3
BRANCH_WIDTH = 1024
N_EXPERTS = 16
EXPERT_FF = 2048
CAPACITY_FACTOR = 2
RMS_EPS = 1e-6
NEG_INF = -1e30

F32 = jnp.float32
BF16 = jnp.bfloat16
I32 = jnp.int32

COL_QKV = 0
COL_UF = 3072
COL_LAT = 4096
COL_GATE = 5120
PROJ_COLS = COL_GATE + N_BRANCHES * D_MODEL
LAT_W = 1024
CQ_PAD = 512
CKV_PAD = 256
VT_ROWS = MLA_V + 16
LOG2E = 1.4426950408889634
TOK_BITS = 12
PFX_BLK = 512
COMB_TT = 256

VMEM_LIMIT = 56 * 1024 * 1024


def _cparams(*sem, **kw):
    return pltpu.CompilerParams(dimension_semantics=sem, vmem_limit_bytes=VMEM_LIMIT, **kw)


def _inproj_kernel(x_ref, g_ref, w_ref, b_ref, o_ref, xn_ref, *, gate_tile0):
    j = pl.program_id(1)

    @pl.when(j == 0)
    def _():
        x = x_ref[...]
        ms = jnp.mean(x * x, axis=-1, keepdims=True)
        xn_ref[...] = (x * lax.rsqrt(ms + RMS_EPS) * g_ref[...]).astype(BF16)

    acc = jnp.dot(xn_ref[...], w_ref[...], preferred_element_type=F32)

    @pl.when(j < gate_tile0)
    def _():
        o_ref[...] = acc.astype(o_ref.dtype)

    @pl.when(j >= gate_tile0)
    def _():
        z = acc + b_ref[...]
        o_ref[...] = (0.5 * jnp.tanh(0.5 * z) + 0.5).astype(o_ref.dtype)


def _inproj(h2, gain, w_all, b_all, tm=1024, tn=1024):
    T = h2.shape[0]
    return pl.pallas_call(
        functools.partial(_inproj_kernel, gate_tile0=COL_GATE // tn),
        out_shape=jax.ShapeDtypeStruct((T, PROJ_COLS), BF16),
        grid=(T // tm, PROJ_COLS // tn),
        in_specs=[
            pl.BlockSpec((tm, D_MODEL), lambda i, j: (i, 0)),
            pl.BlockSpec((1, D_MODEL), lambda i, j: (0, 0)),
            pl.BlockSpec((D_MODEL, tn), lambda i, j: (0, j)),
            pl.BlockSpec((1, tn), lambda i, j: (0, j)),
        ],
        out_specs=pl.BlockSpec((tm, tn), lambda i, j: (i, j)),
        scratch_shapes=[pltpu.VMEM((tm, D_MODEL), BF16)],
        compiler_params=_cparams("parallel", "arbitrary"),
        name="inproj",
    )(h2, gain, w_all, b_all)


def _na_kernel(q_ref, k_ref, v_ref, bias_a_ref, bias_b_ref, o_ref, *, rows):
    i = pl.program_id(1)
    nk = NA_KH * GRID_W
    lo = lax.broadcasted_iota(jnp.int32, (GRID_W, 128), 1) < NA_HEAD_DIM
    nt = (((1,), (1,)), ((), ()))
    row0 = []
    for rr in range(2):
        start = jnp.clip(2 * i + rr - NA_KH // 2, 0, rows - NA_KH)
        row0.append(pl.multiple_of(start * GRID_W, GRID_W))

    def scores(rr, hp):
        sl = slice(hp * 128, (hp + 1) * 128)
        q2 = q_ref[rr * GRID_W:(rr + 1) * GRID_W, sl]
        zero = jnp.zeros_like(q2)
        qq = jnp.concatenate([jnp.where(lo, q2, zero), jnp.where(lo, zero, q2)], axis=0)
        return lax.dot_general(qq, k_ref[pl.ds(row0[rr], nk), sl], nt, preferred_element_type=F32)

    chains = [(rr, hp) for hp in range(NA_HEADS // 2) for rr in range(2)]
    s_next = scores(*chains[0])
    for n, (rr, hp) in enumerate(chains):
        s = s_next
        if n + 1 < len(chains):
            s_next = scores(*chains[n + 1])
        sl = slice(hp * 128, (hp + 1) * 128)
        bias_ref = bias_a_ref if rr == 0 else bias_b_ref
        s = s * (NA_HEAD_DIM ** -0.5 * LOG2E) + bias_ref[0, hp]
        m = jnp.max(s, axis=-1, keepdims=True)
        p = jnp.exp2(s - m)
        l = jnp.sum(p, axis=-1, keepdims=True)
        o = jnp.dot(p.astype(BF16), v_ref[pl.ds(row0[rr], nk), sl], preferred_element_type=F32) / l
        o_ref[rr * GRID_W:(rr + 1) * GRID_W, sl] = jnp.where(lo, o[:GRID_W], o[GRID_W:]).astype(o_ref.dtype)


def _na_bias_tables(rpb):
    cols = np.arange(GRID_W)
    col_start = np.clip(cols - NA_KW // 2, 0, GRID_W - NA_KW)
    col_valid = (cols[None, :] >= col_start[:, None]) & (cols[None, :] < col_start[:, None] + NA_KW)
    col_idx = np.clip(cols[None, :] - cols[:, None] + NA_KW - 1, 0, 2 * NA_KW - 2)
    onehot = (col_idx[None] == np.arange(2 * NA_KW - 1)[:, None, None]).astype(np.float32)
    toep = jnp.einsum("hdc,cqj->hdqj", rpb.astype(F32), jnp.asarray(onehot), precision=lax.Precision.HIGHEST)
    variants = []
    for t in range(NA_KH):
        b = toep[:, NA_KH - 1 - t:2 * NA_KH - 1 - t]
        b = jnp.where(col_valid[None, None], b, NEG_INF)
        variants.append(jnp.transpose(b, (0, 2, 1, 3)).reshape(NA_HEADS // 2, 2 * GRID_W, NA_KH * GRID_W))
    return (jnp.stack(variants) * LOG2E).astype(F32)


def _na_attention(proj, bias, B, S):
    rows = S // GRID_W
    assert rows >= NA_KH and rows % 2 == 0
    T = B * S
    half = rows // 2

    def bias_spec(rr):
        def bias_map(b, i):
            r = 2 * i + rr
            return (r - jnp.clip(r - NA_KH // 2, 0, rows - NA_KH), 0, 0, 0)
        return pl.BlockSpec((1, NA_HEADS // 2, 2 * GRID_W, NA_KH * GRID_W), bias_map)

    return pl.pallas_call(
        functools.partial(_na_kernel, rows=rows),
        out_shape=jax.ShapeDtypeStruct((T, NA_WIDTH), BF16),
        grid=(B, half),
        in_specs=[
            pl.BlockSpec((2 * GRID_W, NA_WIDTH), lambda b, i: (b * half + i, 0)),
            pl.BlockSpec((S, NA_WIDTH), lambda b, i: (b, 1)),
            pl.BlockSpec((S, NA_WIDTH), lambda b, i: (b, 2)),
            bias_spec(0), bias_spec(1),
        ],
        out_specs=pl.BlockSpec((2 * GRID_W, NA_WIDTH), lambda b, i: (b * half + i, 0)),
        compiler_params=_cparams("parallel", "arbitrary"),
        name="na_attn",
    )(proj, proj, proj, bias, bias)


def _mla_up_kernel(lat_ref, cos_ref, sin_ref, cost_ref, sint_ref, qn_ref, kvn_ref, wq_ref, wrot_ref, wk_ref,
                   wvt_ref, q_ref, k_ref, vt_ref):
    cq = lat_ref[:, 0:CQ_PAD].astype(F32)
    ms = jnp.sum(cq * cq, axis=-1, keepdims=True) * (1.0 / MLA_Q_LORA)
    xq = (cq * lax.rsqrt(ms + RMS_EPS) * qn_ref[...]).astype(BF16)
    ckv = lat_ref[:, CQ_PAD:CQ_PAD + CKV_PAD].astype(F32)
    ms2 = jnp.sum(ckv * ckv, axis=-1, keepdims=True) * (1.0 / MLA_KV_LORA)
    xkv = (ckv * lax.rsqrt(ms2 + RMS_EPS) * kvn_ref[...]).astype(BF16)
    cos = cos_ref[...]
    sin = sin_ref[...]
    kpe = (lat_ref[:, 768:896].astype(F32) * cos + lat_ref[:, 896:1024].astype(F32) * sin).astype(BF16)
    scale = (MLA_NOPE + MLA_ROPE) ** -0.5 * LOG2E
    ones = jnp.ones((VT_ROWS - MLA_V, lat_ref.shape[0]), BF16)
    nt = (((1,), (1,)), ((), ()))
    for h in range(MLA_HEADS):
        a = lax.dot_general(wq_ref[h * 256:(h + 1) * 256, :], xq, nt, preferred_element_type=F32)
        rt = lax.dot_general(wrot_ref[h * 128:(h + 1) * 128, :], xq, nt, preferred_element_type=F32)
        q_ref[0, h, 0:128, :] = (a[0:128] * scale).astype(BF16)
        q_ref[0, h, 128:256, :] = ((a[128:256] * cost_ref[...] + rt * sint_ref[...]) * scale).astype(BF16)
        kn = jnp.dot(xkv, wk_ref[:, h * 128:(h + 1) * 128], preferred_element_type=F32)
        k_ref[0, h, :, 0:128] = kn.astype(BF16)
        k_ref[0, h, :, 128:256] = kpe
        vt = lax.dot_general(wvt_ref[h * 128:(h + 1) * 128, :], xkv, nt, preferred_element_type=F32)
        vt_ref[0, h, 0:MLA_V, :] = vt.astype(BF16)
        vt_ref[0, h, MLA_V:VT_ROWS, :] = ones


def _mla_up(proj, cos128, sin128, cos_t, sin_t, qn, kvn, wq, wrot, wk, wvt, B, S, tm=512):
    nt = S // tm
    lat_blk = COL_LAT // LAT_W
    const = lambda b, i: (0, 0)
    return pl.pallas_call(
        _mla_up_kernel,
        out_shape=(
            jax.ShapeDtypeStruct((B, MLA_HEADS, 256, S), BF16),
            jax.ShapeDtypeStruct((B, MLA_HEADS, S, 256), BF16),
            jax.ShapeDtypeStruct((B, MLA_HEADS, VT_ROWS, S), BF16),
        ),
        grid=(B, nt),
        in_specs=[
            pl.BlockSpec((tm, LAT_W), lambda b, i: (b * nt + i, lat_blk)),
            pl.BlockSpec((tm, 128), lambda b, i: (i, 0)),
            pl.BlockSpec((tm, 128), lambda b, i: (i, 0)),
            pl.BlockSpec((128, tm), lambda b, i: (0, i)),
            pl.BlockSpec((128, tm), lambda b, i: (0, i)),
            pl.BlockSpec((1, CQ_PAD), const),
            pl.BlockSpec((1, CKV_PAD), const),
            pl.BlockSpec((MLA_HEADS * 256, CQ_PAD), const),
            pl.BlockSpec((MLA_HEADS * 128, CQ_PAD), const),
            pl.BlockSpec((CKV_PAD, MLA_HEADS * MLA_NOPE), const),
            pl.BlockSpec((MLA_HEADS * MLA_V, CKV_PAD), const),
        ],
        out_specs=(
            pl.BlockSpec((1, MLA_HEADS, 256, tm), lambda b, i: (b, 0, 0, i)),
            pl.BlockSpec((1, MLA_HEADS, tm, 256), lambda b, i: (b, 0, i, 0)),
            pl.BlockSpec((1, MLA_HEADS, VT_ROWS, tm), lambda b, i: (b, 0, 0, i)),
        ),
        compiler_params=_cparams("parallel", "parallel"),
        name="mla_up",
    )(proj, cos128, sin128, cos_t, sin_t, qn, kvn, wq, wrot, wk, wvt)


def _mla_attn_kernel(q_ref, k_ref, vt_ref, o_ref, *, ck):
    qt = q_ref[0, 0]
    n_chunks = k_ref.shape[2] // ck
    m = acc = None

    def scores(c):
        return jnp.dot(k_ref[0, 0, c * ck:(c + 1) * ck, :], qt, preferred_element_type=F32)

    s_next = scores(0)
    for c in range(n_chunks):
        s = s_next
        if c + 1 < n_chunks:
            s_next = scores(c + 1)
        mc = jnp.max(s, axis=0, keepdims=True)
        m_new = mc if c == 0 else jnp.maximum(m, mc)
        p = jnp.exp2(s - m_new).astype(BF16)
        pv = jnp.dot(vt_ref[0, 0, :, c * ck:(c + 1) * ck], p, preferred_element_type=F32)
        acc = pv if c == 0 else acc * jnp.exp2(m - m_new) + pv
        m = m_new
    o = acc[0:MLA_V] / acc[MLA_V:MLA_V + 1]
    o_ref[...] = o.T.astype(o_ref.dtype)


def _mla_attn(q, k, vt, B, S, tq=1024, ck=512):
    nq = S // tq
    return pl.pallas_call(
        functools.partial(_mla_attn_kernel, ck=ck),
        out_shape=jax.ShapeDtypeStruct((B * S, MLA_HEADS * MLA_V), BF16),
        grid=(B, MLA_HEADS, nq),
        in_specs=[
            pl.BlockSpec((1, 1, 256, tq), lambda b, h, i: (b, h, 0, i)),
            pl.BlockSpec((1, 1, S, 256), lambda b, h, i: (b, h, 0, 0)),
            pl.BlockSpec((1, 1, VT_ROWS, S), lambda b, h, i: (b, h, 0, 0)),
        ],
        out_specs=pl.BlockSpec((tq, MLA_V), lambda b, h, i: (b * nq + i, h)),
        compiler_params=_cparams("parallel", "parallel", "arbitrary"),
        name="mla_attn",
    )(q, k, vt)


def _fnet_ch_kernel(u_ref, w_ref, a_ref, b_ref):
    gd = FNET_GROUP_DIM
    for g in range(FNET_GROUPS):
        ab = jnp.dot(u_ref[:, g * gd:(g + 1) * gd], w_ref[...], preferred_element_type=F32)
        a_ref[:, g * gd:(g + 1) * gd] = ab[:, :gd].astype(a_ref.dtype)
        b_ref[:, g * gd:(g + 1) * gd] = ab[:, gd:].astype(b_ref.dtype)


def _fnet_channel(proj, w_cs, tm=1024):
    T = proj.shape[0]
    W = FNET_GROUPS * FNET_GROUP_DIM
    return pl.pallas_call(
        _fnet_ch_kernel,
        out_shape=(jax.ShapeDtypeStruct((T, W), BF16), jax.ShapeDtypeStruct((T, W), BF16)),
        grid=(T // tm,),
        in_specs=[
            pl.BlockSpec((tm, W), lambda i: (i, COL_UF // W)),
            pl.BlockSpec((FNET_GROUP_DIM, 2 * FNET_GROUP_DIM), lambda i: (0, 0)),
        ],
        out_specs=(pl.BlockSpec((tm, W), lambda i: (i, 0)), pl.BlockSpec((tm, W), lambda i: (i, 0))),
        compiler_params=_cparams("parallel"),
        name="fnet_channel",
    )(proj, w_cs)


def _fnet_pos_kernel(c_ref, s_ref, a_ref, b_ref, o_ref, *, scale):
    y = (jnp.dot(c_ref[...], a_ref[...], preferred_element_type=F32)
         + jnp.dot(s_ref[...], b_ref[...], preferred_element_type=F32))
    o_ref[...] = (y * scale).astype(o_ref.dtype)


def _fnet_position(c_tab, sn_tab, a, b, B, S, tm=512, tn=512):
    W = FNET_GROUPS * FNET_GROUP_DIM
    nm = S // tm
    scale = float((S * FNET_GROUP_DIM) ** -0.5)
    return pl.pallas_call(
        functools.partial(_fnet_pos_kernel, scale=scale),
        out_shape=jax.ShapeDtypeStruct((B * S, W), BF16),
        grid=(nm, B, W // tn),
        in_specs=[
            pl.BlockSpec((tm, S), lambda m, bb, n: (m, 0)),
            pl.BlockSpec((tm, S), lambda m, bb, n: (m, 0)),
            pl.BlockSpec((S, tn), lambda m, bb, n: (bb, n)),
            pl.BlockSpec((S, tn), lambda m, bb, n: (bb, n)),
        ],
        out_specs=pl.BlockSpec((tm, tn), lambda m, bb, n: (bb * nm + m, n)),
        compiler_params=_cparams("parallel", "parallel", "parallel"),
        name="fnet_position",
    )(c_tab, sn_tab, a, b)


def _dft_tables(S):
    gd = FNET_GROUP_DIM
    ck = (np.arange(gd)[:, None] * np.arange(gd)[None, :]) % gd
    ang = 2.0 * np.pi * ck / gd
    w_cs = jnp.asarray(np.concatenate([np.cos(ang), np.sin(ang)], axis=1), F32).astype(BF16)
    kb = 64
    n = lax.broadcasted_iota(jnp.int32, (1, S), 1)

    def thin(rows, period):
        ang_ = ((lax.broadcasted_iota(jnp.int32, (rows, 1), 0) * n) % period).astype(F32) * (2.0 * np.pi / period)
        return jnp.cos(ang_)[:, None, :], jnp.sin(ang_)[:, None, :]

    c_hi, s_hi = thin(S // kb, S // kb)
    c_lo, s_lo = thin(kb, S)
    c_lo, s_lo = c_lo.reshape(1, kb, S), s_lo.reshape(1, kb, S)
    cos_t = (c_hi * c_lo - s_hi * s_lo).reshape(S, S)
    nsin_t = -(s_hi * c_lo + c_hi * s_lo).reshape(S, S)
    return w_cs, cos_t.astype(BF16), nsin_t.astype(BF16)


def _merge_kernel(yn_ref, ym_ref, yf_ref, w_ref, g0_ref, g1_ref, g2_ref, o_ref, wbf_ref):
    @pl.when(pl.program_id(1) == 0)
    def _():
        wbf_ref[...] = w_ref[0].astype(BF16)

    acc = g0_ref[...].astype(F32) * jnp.dot(yn_ref[...], wbf_ref[0], preferred_element_type=F32)
    acc += g1_ref[...].astype(F32) * jnp.dot(ym_ref[...], wbf_ref[1], preferred_element_type=F32)
    acc += g2_ref[...].astype(F32) * jnp.dot(yf_ref[...], wbf_ref[2], preferred_element_type=F32)
    o_ref[...] = acc.astype(o_ref.dtype)


def _merge(y_na, y_mla, y_f, w_branch, layer, proj, tm=1024, tn=512):
    T = y_na.shape[0]
    ybs = pl.BlockSpec((tm, BRANCH_WIDTH), lambda j, i: (i, 0))

    def gate_spec(br):
        off = (COL_GATE + br * D_MODEL) // tn
        return pl.BlockSpec((tm, tn), lambda j, i: (i, off + j))

    return pl.pallas_call(
        _merge_kernel,
        out_shape=jax.ShapeDtypeStruct((T, D_MODEL), BF16),
        grid=(D_MODEL // tn, T // tm),
        in_specs=[ybs, ybs, ybs,
                  pl.BlockSpec((1, N_BRANCHES, BRANCH_WIDTH, tn), lambda j, i: (layer, 0, 0, j)),
                  gate_spec(0), gate_spec(1), gate_spec(2)],
        out_specs=pl.BlockSpec((tm, tn), lambda j, i: (i, j)),
        scratch_shapes=[pltpu.VMEM((N_BRANCHES, BRANCH_WIDTH, tn), BF16)],
        compiler_params=_cparams("parallel", "arbitrary"),
        name="merge",
    )(y_na, y_mla, y_f, w_branch, proj, proj, proj)


def _outproj_kernel(m_ref, w_ref, h_ref, o_ref, wbf_ref):
    @pl.when(pl.program_id(1) == 0)
    def _():
        wbf_ref[...] = w_ref[0].astype(BF16)

    o_ref[...] = h_ref[...] + jnp.dot(m_ref[...], wbf_ref[...], preferred_element_type=F32)


def _outproj(merged, w_o, layer, h2, tm=1024, tn=512):
    T = merged.shape[0]
    return pl.pallas_call(
        _outproj_kernel,
        out_shape=jax.ShapeDtypeStruct((T, D_MODEL), F32),
        grid=(D_MODEL // tn, T // tm),
        in_specs=[
            pl.BlockSpec((tm, D_MODEL), lambda j, i: (i, 0)),
            pl.BlockSpec((1, D_MODEL, tn), lambda j, i: (layer, 0, j)),
            pl.BlockSpec((tm, tn), lambda j, i: (i, j)),
        ],
        out_specs=pl.BlockSpec((tm, tn), lambda j, i: (i, j)),
        scratch_shapes=[pltpu.VMEM((D_MODEL, tn), BF16)],
        compiler_params=_cparams("parallel", "arbitrary"),
        name="outproj",
    )(merged, w_o, h2)


def _router_kernel(x_ref, g_ref, wrt_ref, aff_ref):
    x = x_ref[...]
    ms = jnp.mean(x * x, axis=-1, keepdims=True)
    xn = x * lax.rsqrt(ms + RMS_EPS) * g_ref[...]
    logits = lax.dot_general(wrt_ref[...], xn, (((1,), (1,)), ((), ())), preferred_element_type=F32,
                             precision=lax.Precision.HIGHEST)
    m = jnp.max(logits, axis=0, keepdims=True)
    e = jnp.exp(logits - m)
    aff_ref[0] = e / jnp.sum(e, axis=0, keepdims=True)


def _router(h2, gain, w_router_t, B, S, tm=1024):
    nt = S // tm
    return pl.pallas_call(
        _router_kernel,
        out_shape=jax.ShapeDtypeStruct((B, N_EXPERTS, S), F32),
        grid=(B, nt),
        in_specs=[
            pl.BlockSpec((tm, D_MODEL), lambda b, i: (b * nt + i, 0)),
            pl.BlockSpec((1, D_MODEL), lambda b, i: (0, 0)),
            pl.BlockSpec((N_EXPERTS, D_MODEL), lambda b, i: (0, 0)),
        ],
        out_specs=pl.BlockSpec((1, N_EXPERTS, tm), lambda b, i: (b, 0, i)),
        compiler_params=_cparams("parallel", "parallel"),
        name="router",
    )(h2, gain, w_router_t)


def _route_kernel(a_ref, val_ref, gate_ref, starts_ref, kmax_ref, pos_s, pv_s, a_s, *, cap, tt):
    E, S = a_ref.shape[1], a_ref.shape[2]
    a = a_ref[0]

    def as_float(b):
        return lax.bitcast_convert_type(b, F32)

    def bisect(_, carry):
        lo, hi = carry
        mid = lo + ((hi - lo + 1) >> 1)
        cnt = jnp.sum(jnp.where(a >= as_float(mid), 1.0, 0.0), axis=1, keepdims=True)
        ge = cnt >= cap
        return jnp.where(ge, mid, lo), jnp.where(ge, hi, mid - 1)

    lo0 = jnp.zeros((E, 1), I32)
    hi0 = jnp.full((E, 1), 0x7F7FFFFF, I32)
    thr_bits, _ = lax.fori_loop(0, 31, bisect, (lo0, hi0))
    thr, thr_up = as_float(thr_bits), as_float(thr_bits + 1)

    r_i = lax.broadcasted_iota(I32, (PFX_BLK, PFX_BLK), 0)
    c_i = lax.broadcasted_iota(I32, (PFX_BLK, PFX_BLK), 1)
    tri = jnp.where(r_i < c_i, 1.0, 0.0).astype(BF16)

    def excl_prefix(mask):
        x = jnp.where(mask, 1.0, 0.0).astype(BF16)
        carry = jnp.zeros((E, 1), F32)
        outs = []
        for j in range(S // PFX_BLK):
            blk = x[:, j * PFX_BLK:(j + 1) * PFX_BLK]
            outs.append(jnp.dot(blk, tri, preferred_element_type=F32) + carry)
            carry = carry + jnp.sum(blk.astype(F32), axis=1, keepdims=True)
        return jnp.concatenate(outs, axis=1)

    gt = a >= thr_up
    eq = (a >= thr) & (a < thr_up)
    need = cap - jnp.sum(jnp.where(gt, 1.0, 0.0), axis=1, keepdims=True)
    sel = gt | (eq & (excl_prefix(eq) < need))
    pos = excl_prefix(sel)

    self_ = jnp.where(sel, 1.0, 0.0)
    e_r = lax.broadcasted_iota(I32, (E, E), 0)
    e_c = lax.broadcasted_iota(I32, (E, E), 1)
    low = jnp.where(e_c < e_r, 1.0, 0.0).astype(BF16)
    rank = jnp.dot(low, self_.astype(BF16), preferred_element_type=F32)
    tok = lax.broadcasted_iota(I32, (E, S), 1)
    pos_s[...] = jnp.where(sel, pos.astype(I32), -1)
    pv_s[...] = (tok + (rank.astype(I32) << TOK_BITS)).astype(F32)
    a_s[...] = a

    lane = lax.broadcasted_iota(I32, (E, 128), 1)
    starts = jnp.full((E, 128), cap, I32)
    kcount = jnp.sum(self_, axis=0, keepdims=True)
    kmax = jnp.zeros((8, 128), I32)
    lane8 = lax.broadcasted_iota(I32, (8, 128), 1)
    posi = pos.astype(I32)
    for j in range(S // tt):
        starts = jnp.where(lane == j, posi[:, j * tt:j * tt + 1], starts)
        kj = jnp.max(kcount[:, j * tt:(j + 1) * tt], axis=1, keepdims=True).astype(I32)
        kmax = jnp.where(lane8 == j, kj, kmax)
    starts_ref[0] = starts
    kmax_ref[0] = kmax

    c_iota = lax.broadcasted_iota(I32, (cap, PFX_BLK), 0)
    lane_c = lax.broadcasted_iota(I32, (cap, 128), 1)

    def per_expert(e, carry):
        val_acc, gate_acc = carry
        v_part = jnp.zeros((cap, PFX_BLK), F32)
        g_part = jnp.zeros((cap, PFX_BLK), F32)
        for j in range(S // PFX_BLK):
            sl = pl.ds(j * PFX_BLK, PFX_BLK)
            hit = pos_s[pl.ds(e, 1), sl] == c_iota
            v_part = v_part + jnp.where(hit, pv_s[pl.ds(e, 1), sl], 0.0)
            g_part = g_part + jnp.where(hit, a_s[pl.ds(e, 1), sl], 0.0)
        v = jnp.sum(v_part, axis=1, keepdims=True).astype(I32)
        g = jnp.sum(g_part, axis=1, keepdims=True)
        return jnp.where(lane_c == e, v, val_acc), jnp.where(lane_c == e, g, gate_acc)

    val, gate = lax.fori_loop(0, E, per_expert, (jnp.zeros((cap, 128), I32), jnp.zeros((cap, 128), F32)))
    val_ref[0] = val
    gate_ref[0] = gate


def _route(aff_t, cap, tt=COMB_TT):
    B, E, S = aff_t.shape
    assert S <= (1 << TOK_BITS) and S // tt < 128 and S % PFX_BLK == 0
    return pl.pallas_call(
        functools.partial(_route_kernel, cap=cap, tt=tt),
        out_shape=(
            jax.ShapeDtypeStruct((B, cap, 128), I32),
            jax.ShapeDtypeStruct((B, cap, 128), F32),
            jax.ShapeDtypeStruct((B, E, 128), I32),
            jax.ShapeDtypeStruct((B, 8, 128), I32),
        ),
        grid=(B,),
        in_specs=[pl.BlockSpec((1, E, S), lambda b: (b, 0, 0))],
        out_specs=(
            pl.BlockSpec((1, cap, 128), lambda b: (b, 0, 0)),
            pl.BlockSpec((1, cap, 128), lambda b: (b, 0, 0)),
            pl.BlockSpec((1, E, 128), lambda b: (b, 0, 0)),
            pl.BlockSpec((1, 8, 128), lambda b: (b, 0, 0)),
        ),
        scratch_shapes=[pltpu.VMEM((E, S), I32), pltpu.VMEM((E, S), F32), pltpu.VMEM((E, S), F32)],
        compiler_params=_cparams("parallel"),
        name="route",
    )(aff_t)


def _gather_kernel(rows_ref, h_hbm, g_ref, o_ref, buf, sem, *, gt):
    i = pl.program_id(0)
    slot = i % 2

    def issue_tile(t, s):
        def issue(r, _):
            pltpu.make_async_copy(h_hbm.at[pl.ds(rows_ref[t * gt + r], 1)], buf.at[s, pl.ds(r, 1)], sem.at[s]).start()
            return 0

        lax.fori_loop(0, gt, issue, 0, unroll=8)

    @pl.when(i == 0)
    def _():
        issue_tile(0, 0)

    @pl.when(i + 1 < pl.num_programs(0))
    def _():
        issue_tile(i + 1, 1 - slot)

    pltpu.make_async_copy(h_hbm.at[pl.ds(0, gt)], buf.at[slot], sem.at[slot]).wait()
    x = buf[slot]
    ms = jnp.mean(x * x, axis=-1, keepdims=True)
    o_ref[...] = (x * lax.rsqrt(ms + RMS_EPS) * g_ref[...]).astype(o_ref.dtype)


def _gather_norm(rows, h2, gain, gt=1024):
    n = rows.shape[0]
    gt = min(gt, n)
    return pl.pallas_call(
        functools.partial(_gather_kernel, gt=gt),
        out_shape=jax.ShapeDtypeStruct((n, D_MODEL), BF16),
        grid_spec=pltpu.PrefetchScalarGridSpec(
            num_scalar_prefetch=1,
            grid=(n // gt,),
            in_specs=[pl.BlockSpec(memory_space=pl.ANY), pl.BlockSpec((1, D_MODEL), lambda i, rows: (0, 0))],
            out_specs=pl.BlockSpec((gt, D_MODEL), lambda i, rows: (i, 0)),
            scratch_shapes=[pltpu.VMEM((2, gt, D_MODEL), F32), pltpu.SemaphoreType.DMA((2,))],
        ),
        compiler_params=_cparams("arbitrary", disable_bounds_checks=True),
        name="gather_norm",
    )(rows, h2, gain)


def _combine_kernel(dst_ref, starts_ref, kmax_ref, h_ref, ye_hbm, fg_ref, o_ref, planes, sem, *,
                    B, E, cap, tt, nt_pad, final_norm):
    b = pl.program_id(0)
    j = pl.program_id(1)
    k = kmax_ref[b * nt_pad + j]

    def zero(p, _):
        planes[pl.ds(pl.multiple_of(p * tt, tt), tt), :] = jnp.zeros((tt, D_MODEL), F32)
        return 0

    lax.fori_loop(0, k, zero, 0)

    def row_copy(slot, dst_row):
        return pltpu.make_async_copy(ye_hbm.at[pl.ds(slot, 1)], planes.at[pl.ds(dst_row, 1)], sem)

    def per_expert(e, n):
        sbase = (b * E + e) * nt_pad + j
        c0 = starts_ref[sbase]
        c1 = starts_ref[sbase + 1]
        lbase = (e * B + b) * cap

        def issue(c):
            row_copy(lbase + c, dst_ref[lbase + c]).start()

        def issue4(i, _):
            for u in range(4):
                issue(c0 + 4 * i + u)
            return 0

        def issue1(i, _):
            issue(c1 - 1 - i)
            return 0

        cnt = c1 - c0
        lax.fori_loop(0, cnt >> 2, issue4, 0)
        lax.fori_loop(0, cnt & 3, issue1, 0)
        return n + cnt

    n = lax.fori_loop(0, E, per_expert, 0)

    for bit in range((E * tt).bit_length()):
        @pl.when(((n >> bit) & 1) == 1)
        def _():
            rows = 1 << bit
            pltpu.make_async_copy(ye_hbm.at[pl.ds(0, rows)], planes.at[pl.ds(0, rows)], sem).wait()

    for cs in range(D_MODEL // 128):
        cols = slice(cs * 128, (cs + 1) * 128)

        def add(p, acc):
            return acc + planes[pl.ds(pl.multiple_of(p * tt, tt), tt), cols]

        o_ref[:, cols] = lax.fori_loop(0, k, add, h_ref[:, cols])

    if final_norm:
        x = o_ref[...]
        ms = jnp.mean(x * x, axis=-1, keepdims=True)
        o_ref[...] = x * lax.rsqrt(ms + RMS_EPS) * fg_ref[...]


def _combine(h2, ye, dst_rows, starts, kmax, final_gain, B, S, cap, tt=COMB_TT):
    E = N_EXPERTS
    nt = S // tt
    nt_pad = starts.shape[0] // (B * E)
    final_norm = final_gain is not None
    fg = final_gain if final_norm else jnp.ones((1, D_MODEL), F32)
    return pl.pallas_call(
        functools.partial(_combine_kernel, B=B, E=E, cap=cap, tt=tt, nt_pad=nt_pad, final_norm=final_norm),
        out_shape=jax.ShapeDtypeStruct(h2.shape, F32),
        grid_spec=pltpu.PrefetchScalarGridSpec(
            num_scalar_prefetch=3,
            grid=(B, nt),
            in_specs=[pl.BlockSpec((tt, D_MODEL), lambda b, j, *_: (b * nt + j, 0)),
                      pl.BlockSpec(memory_space=pl.ANY),
                      pl.BlockSpec((1, D_MODEL), lambda b, j, *_: (0, 0))],
            out_specs=pl.BlockSpec((tt, D_MODEL), lambda b, j, *_: (b * nt + j, 0)),
            scratch_shapes=[pltpu.VMEM((E * tt, D_MODEL), F32), pltpu.SemaphoreType.DMA(())],
        ),
        compiler_params=_cparams("arbitrary", "arbitrary", disable_bounds_checks=True),
        name="combine",
    )(dst_rows, starts, kmax, h2, ye, fg)


def _ffn_up_kernel(x_ref, wg_ref, wu_ref, o_ref):
    x = x_ref[...]
    a = jnp.dot(x, wg_ref[0, 0].astype(BF16), preferred_element_type=F32)
    u = jnp.dot(x, wu_ref[0, 0].astype(BF16), preferred_element_type=F32)
    o_ref[...] = (a * (1.0 / (1.0 + jnp.exp(-a))) * u).astype(o_ref.dtype)


def _ffn_down_kernel(hid_ref, wd_ref, gate_ref, o_ref):
    y = jnp.dot(hid_ref[...], wd_ref[0, 0].astype(BF16), preferred_element_type=F32)
    o_ref[...] = y * gate_ref[...]


def _experts(xe, wg, wu, wd, layer, gate, tf=256, tn=512):
    n_tok = xe.shape[0]
    per_e = n_tok // N_EXPERTS
    ff = wg.shape[-1]
    hid = pl.pallas_call(
        _ffn_up_kernel,
        out_shape=jax.ShapeDtypeStruct((n_tok, ff), BF16),
        grid=(N_EXPERTS, ff // tf),
        in_specs=[
            pl.BlockSpec((per_e, D_MODEL), lambda e, f: (e, 0)),
            pl.BlockSpec((1, 1, D_MODEL, tf), lambda e, f: (layer, e, 0, f)),
            pl.BlockSpec((1, 1, D_MODEL, tf), lambda e, f: (layer, e, 0, f)),
        ],
        out_specs=pl.BlockSpec((per_e, tf), lambda e, f: (e, f)),
        compiler_params=_cparams("parallel", "arbitrary"),
        name="ffn_up",
    )(xe, wg, wu)
    return pl.pallas_call(
        _ffn_down_kernel,
        out_shape=jax.ShapeDtypeStruct((n_tok, D_MODEL), F32),
        grid=(N_EXPERTS, D_MODEL // tn),
        in_specs=[
            pl.BlockSpec((per_e, ff), lambda e, n: (e, 0)),
            pl.BlockSpec((1, 1, ff, tn), lambda e, n: (layer, e, 0, n)),
            pl.BlockSpec((per_e, 1), lambda e, n: (e, 0)),
        ],
        out_specs=pl.BlockSpec((per_e, tn), lambda e, n: (e, n)),
        compiler_params=_cparams("parallel", "arbitrary"),
        name="ffn_down",
    )(hid, wd, gate)


def _moe_layer(h2, gain, w_router, wg, wu, wd, layer, final_gain, B, S):
    E = N_EXPERTS
    cap = CAPACITY_FACTOR * S // E
    nt = S // COMB_TT
    aff_t = _router(h2, gain, jnp.transpose(w_router), B, S)
    val, gate, starts, kmax = _route(aff_t, cap)
    val_e = jnp.transpose(val[:, :, :E], (2, 0, 1))
    gate_e = jnp.transpose(gate[:, :, :E], (2, 0, 1)).reshape(-1, 1)
    tok_e = val_e & ((1 << TOK_BITS) - 1)
    rows = (tok_e + (jnp.arange(B, dtype=I32) * S)[None, :, None]).reshape(-1)
    xe = _gather_norm(rows, h2, gain)
    ye = _experts(xe, wg, wu, wd, layer, gate_e)
    dst = ((val_e >> TOK_BITS) * COMB_TT + tok_e % COMB_TT).reshape(-1)
    starts_flat = starts[:, :, :nt + 1].reshape(-1)
    kmax_flat = kmax[:, 0, :nt + 1].reshape(-1)
    return _combine(h2, ye, dst, starts_flat, kmax_flat, final_gain, B, S, cap)


def _pad_cols(w, n):
    return jnp.pad(w, ((0, 0), (0, n - w.shape[1])))


def _rot_cols(w):
    half = w.shape[1] // 2
    return jnp.concatenate([-w[:, half:], w[:, :half]], axis=1)


def _prep_in_weights(w_in, b_gate):
    o = np.cumsum([0, 3 * NA_WIDTH, MLA_Q_LORA, MLA_KV_LORA, MLA_ROPE, FNET_GROUPS * FNET_GROUP_DIM])
    qkv, cq, ckv, kr, uf, gl = (w_in[:, o[0]:o[1]], w_in[:, o[1]:o[2]], w_in[:, o[2]:o[3]], w_in[:, o[3]:o[4]],
                                w_in[:, o[4]:o[5]], w_in[:, o[5]:])
    w_all = jnp.concatenate([qkv, uf, _pad_cols(cq, CQ_PAD), _pad_cols(ckv, CKV_PAD), _pad_cols(kr, 128),
                             _pad_cols(_rot_cols(kr), 128), gl], axis=1).astype(BF16)
    b_all = jnp.concatenate([jnp.zeros((COL_GATE,), F32), b_gate]).reshape(1, PROJ_COLS)
    return w_all, b_all


def _prep_mla_weights(w_uq, q_norm, w_ukv, kv_norm):
    qd = MLA_NOPE + MLA_ROPE
    wq3 = w_uq.reshape(MLA_Q_LORA, MLA_HEADS, qd)
    nope, pe = wq3[:, :, :MLA_NOPE], wq3[:, :, MLA_NOPE:]
    z64 = jnp.zeros((MLA_Q_LORA, MLA_HEADS, 64), F32)
    wq = jnp.concatenate([nope, pe, z64], axis=2).reshape(MLA_Q_LORA, MLA_HEADS * 256)
    rot = jnp.concatenate([-pe[:, :, 32:], pe[:, :, :32], z64], axis=2).reshape(MLA_Q_LORA, MLA_HEADS * 128)
    rpad = ((0, CQ_PAD - MLA_Q_LORA), (0, 0))
    wq = jnp.transpose(jnp.pad(wq, rpad)).astype(BF16)
    rot = jnp.transpose(jnp.pad(rot, rpad)).astype(BF16)
    wkv3 = jnp.pad(w_ukv, ((0, CKV_PAD - MLA_KV_LORA), (0, 0))).reshape(CKV_PAD, MLA_HEADS, MLA_NOPE + MLA_V)
    wk = wkv3[:, :, :MLA_NOPE].reshape(CKV_PAD, MLA_HEADS * MLA_NOPE).astype(BF16)
    wvt = jnp.transpose(wkv3[:, :, MLA_NOPE:], (1, 2, 0)).reshape(MLA_HEADS * MLA_V, CKV_PAD).astype(BF16)
    qn = jnp.pad(q_norm, (0, CQ_PAD - MLA_Q_LORA)).reshape(1, CQ_PAD)
    kvn = jnp.pad(kv_norm, (0, CKV_PAD - MLA_KV_LORA)).reshape(1, CKV_PAD)
    return wq, rot, wk, wvt, qn, kvn


def _rope_tables128(S):
    pos = jnp.arange(S, dtype=F32)
    inv = 1.0 / (ROPE_THETA ** (jnp.arange(0, MLA_ROPE, 2, dtype=F32) / MLA_ROPE))
    ang = pos[:, None] * inv[None, :]
    z = jnp.zeros((S, 64), F32)
    cos, sin = jnp.cos(ang), jnp.sin(ang)
    return jnp.concatenate([cos, cos, z], axis=1), jnp.concatenate([sin, sin, z], axis=1)


def kernel(x, w_in, b_gate, w_uq, q_norm, w_ukv, kv_norm, na_rpb, w_branch, w_o, norm_mix, norm_moe,
           w_router, w_exp_gate, w_exp_up, w_exp_down, norm_final):
    B, S, D = x.shape
    T = B * S
    depth = w_in.shape[0]
    cos128, sin128 = _rope_tables128(S)
    cos_t, sin_t = jnp.transpose(cos128), jnp.transpose(sin128)
    w_cs, c_tab, sn_tab = _dft_tables(S)
    h = x.reshape(T, D)
    for l in range(depth):
        w_all, b_all = _prep_in_weights(w_in[l], b_gate[l])
        proj = _inproj(h, norm_mix[l].reshape(1, D), w_all, b_all)
        y_na = _na_attention(proj, _na_bias_tables(na_rpb[l]), B, S)
        wq, wrot, wk, wvt, qn, kvn = _prep_mla_weights(w_uq[l], q_norm[l], w_ukv[l], kv_norm[l])
        q, k, vt = _mla_up(proj, cos128, sin128, cos_t, sin_t, qn, kvn, wq, wrot, wk, wvt, B, S)
        y_mla = _mla_attn(q, k, vt, B, S)
        fa, fb = _fnet_channel(proj, w_cs)
        y_f = _fnet_position(c_tab, sn_tab, fa, fb, B, S)
        merged = _merge(y_na, y_mla, y_f, w_branch, l, proj)
        h = _outproj(merged, w_o, l, h)
        final_gain = norm_final.reshape(1, D) if l == depth - 1 else None
        h = _moe_layer(h, norm_moe[l].reshape(1, D), w_router[l], w_exp_gate, w_exp_up, w_exp_down, l, final_gain,
                       B, S)
    return h.reshape(B, S, D)
```

```python
import functools

import numpy as np
import jax
import jax.numpy as jnp
from jax import lax
from jax.experimental import pallas as pl
from jax.experimental.pallas import tpu as pltpu

D_MODEL = 2048
GRID_W = 64
NA_HEADS = 16
NA_HEAD_DIM = 64
NA_WIDTH = NA_HEADS * NA_HEAD_DIM
NA_KH = 8
NA_KW = 16
MLA_HEADS = 8
MLA_NOPE = 128
MLA_ROPE = 64
MLA_V = 128
MLA_Q_LORA = 448
MLA_KV_LORA = 160
ROPE_THETA = 10000.0
FNET_GROUPS = 4
FNET_GROUP_DIM = 256
N_BRANCHES = 3
BRANCH_WIDTH = 1024
N_EXPERTS = 16
EXPERT_FF = 2048
CAPACITY_FACTOR = 2
RMS_EPS = 1e-6
NEG_INF = -1e30

F32 = jnp.float32
BF16 = jnp.bfloat16
I32 = jnp.int32

COL_QKV = 0
COL_UF = 3072
COL_LAT = 4096
COL_GATE = 5120
PROJ_COLS = COL_GATE + N_BRANCHES * D_MODEL
LAT_W = 1024
CQ_PAD = 512
CKV_PAD = 256
VT_ROWS = MLA_V + 16
LOG2E = 1.4426950408889634
TOK_BITS = 12
PFX_BLK = 512
COMB_TT = 256

VMEM_LIMIT = 56 * 1024 * 1024


def _cparams(*sem, **kw):
    return pltpu.CompilerParams(dimension_semantics=sem, vmem_limit_bytes=VMEM_LIMIT, **kw)


def _inproj_kernel(x_ref, g_ref, w_ref, b_ref, o_ref, xn_ref, *, gate_tile0):
    j = pl.program_id(1)

    @pl.when(j == 0)
    def _():
        x = x_ref[...]
        ms = jnp.mean(x * x, axis=-1, keepdims=True)
        xn_ref[...] = (x * lax.rsqrt(ms + RMS_EPS) * g_ref[...]).astype(BF16)

    acc = jnp.dot(xn_ref[...], w_ref[...], preferred_element_type=F32)

    @pl.when(j < gate_tile0)
    def _():
        o_ref[...] = acc.astype(o_ref.dtype)

    @pl.when(j >= gate_tile0)
    def _():
        z = acc + b_ref[...]
        o_ref[...] = (0.5 * jnp.tanh(0.5 * z) + 0.5).astype(o_ref.dtype)


def _inproj(h2, gain, w_all, b_all, tm=1024, tn=1024):
    T = h2.shape[0]
    return pl.pallas_call(
        functools.partial(_inproj_kernel, gate_tile0=COL_GATE // tn),
        out_shape=jax.ShapeDtypeStruct((T, PROJ_COLS), BF16),
        grid=(T // tm, PROJ_COLS // tn),
        in_specs=[
            pl.BlockSpec((tm, D_MODEL), lambda i, j: (i, 0)),
            pl.BlockSpec((1, D_MODEL), lambda i, j: (0, 0)),
            pl.BlockSpec((D_MODEL, tn), lambda i, j: (0, j)),
            pl.BlockSpec((1, tn), lambda i, j: (0, j)),
        ],
        out_specs=pl.BlockSpec((tm, tn), lambda i, j: (i, j)),
        scratch_shapes=[pltpu.VMEM((tm, D_MODEL), BF16)],
        compiler_params=_cparams("parallel", "arbitrary"),
        name="inproj",
    )(h2, gain, w_all, b_all)


def _na_kernel(q_ref, k_ref, v_ref, bias_a_ref, bias_b_ref, o_ref, *, rows):
    i = pl.program_id(1)
    nk = NA_KH * GRID_W
    lo = lax.broadcasted_iota(jnp.int32, (GRID_W, 128), 1) < NA_HEAD_DIM
    nt = (((1,), (1,)), ((), ()))
    row0 = []
    for rr in range(2):
        start = jnp.clip(2 * i + rr - NA_KH // 2, 0, rows - NA_KH)
        row0.append(pl.multiple_of(start * GRID_W, GRID_W))

    def scores(rr, hp):
        sl = slice(hp * 128, (hp + 1) * 128)
        q2 = q_ref[rr * GRID_W:(rr + 1) * GRID_W, sl]
        zero = jnp.zeros_like(q2)
        qq = jnp.concatenate([jnp.where(lo, q2, zero), jnp.where(lo, zero, q2)], axis=0)
        return lax.dot_general(qq, k_ref[pl.ds(row0[rr], nk), sl], nt, preferred_element_type=F32)

    chains = [(rr, hp) for hp in range(NA_HEADS // 2) for rr in range(2)]
    s_next = scores(*chains[0])
    for n, (rr, hp) in enumerate(chains):
        s = s_next
        if n + 1 < len(chains):
            s_next = scores(*chains[n + 1])
        sl = slice(hp * 128, (hp + 1) * 128)
        bias_ref = bias_a_ref if rr == 0 else bias_b_ref
        s = s * (NA_HEAD_DIM ** -0.5 * LOG2E) + bias_ref[0, hp]
        m = jnp.max(s, axis=-1, keepdims=True)
        p = jnp.exp2(s - m)
        l = jnp.sum(p, axis=-1, keepdims=True)
        o = jnp.dot(p.astype(BF16), v_ref[pl.ds(row0[rr], nk), sl], preferred_element_type=F32) / l
        o_ref[rr * GRID_W:(rr + 1) * GRID_W, sl] = jnp.where(lo, o[:GRID_W], o[GRID_W:]).astype(o_ref.dtype)


def _na_bias_tables(rpb):
    cols = np.arange(GRID_W)
    col_start = np.clip(cols - NA_KW // 2, 0, GRID_W - NA_KW)
    col_valid = (cols[None, :] >= col_start[:, None]) & (cols[None, :] < col_start[:, None] + NA_KW)
    col_idx = np.clip(cols[None, :] - cols[:, None] + NA_KW - 1, 0, 2 * NA_KW - 2)
    onehot = (col_idx[None] == np.arange(2 * NA_KW - 1)[:, None, None]).astype(np.float32)
    toep = jnp.einsum("hdc,cqj->hdqj", rpb.astype(F32), jnp.asarray(onehot), precision=lax.Precision.HIGHEST)
    win = jnp.stack([toep[:, NA_KH - 1 - t:2 * NA_KH - 1 - t] for t in range(NA_KH)])
    win = jnp.where(col_valid[None, None, None], win * LOG2E, NEG_INF * LOG2E)
    win = jnp.transpose(win, (0, 1, 3, 2, 4))
    return win.reshape(NA_KH, NA_HEADS // 2, 2 * GRID_W, NA_KH * GRID_W).astype(F32)


def _na_attention(proj, bias, B, S):
    rows = S // GRID_W
    assert rows >= NA_KH and rows % 2 == 0
    T = B * S
    half = rows // 2

    def bias_spec(rr):
        def bias_map(b, i):
            r = 2 * i + rr
            return (r - jnp.clip(r - NA_KH // 2, 0, rows - NA_KH), 0, 0, 0)
        return pl.BlockSpec((1, NA_HEADS // 2, 2 * GRID_W, NA_KH * GRID_W), bias_map)

    return pl.pallas_call(
        functools.partial(_na_kernel, rows=rows),
        out_shape=jax.ShapeDtypeStruct((T, NA_WIDTH), BF16),
        grid=(B, half),
        in_specs=[
            pl.BlockSpec((2 * GRID_W, NA_WIDTH), lambda b, i: (b * half + i, 0)),
            pl.BlockSpec((S, NA_WIDTH), lambda b, i: (b, 1)),
            pl.BlockSpec((S, NA_WIDTH), lambda b, i: (b, 2)),
            bias_spec(0), bias_spec(1),
        ],
        out_specs=pl.BlockSpec((2 * GRID_W, NA_WIDTH), lambda b, i: (b * half + i, 0)),
        compiler_params=_cparams("parallel", "arbitrary"),
        name="na_attn",
    )(proj, proj, proj, bias, bias)


def _mla_up_kernel(lat_ref, cos_ref, sin_ref, cost_ref, sint_ref, qn_ref, kvn_ref, wq_ref, wrot_ref, wk_ref,
                   wvt_ref, q_ref, k_ref, vt_ref):
    cq = lat_ref[:, 0:CQ_PAD].astype(F32)
    ms = jnp.sum(cq * cq, axis=-1, keepdims=True) * (1.0 / MLA_Q_LORA)
    xq = (cq * lax.rsqrt(ms + RMS_EPS) * qn_ref[...]).astype(BF16)
    ckv = lat_ref[:, CQ_PAD:CQ_PAD + CKV_PAD].astype(F32)
    ms2 = jnp.sum(ckv * ckv, axis=-1, keepdims=True) * (1.0 / MLA_KV_LORA)
    xkv = (ckv * lax.rsqrt(ms2 + RMS_EPS) * kvn_ref[...]).astype(BF16)
    cos = cos_ref[...]
    sin = sin_ref[...]
    kpe = (lat_ref[:, 768:896].astype(F32) * cos + lat_ref[:, 896:1024].astype(F32) * sin).astype(BF16)
    scale = (MLA_NOPE + MLA_ROPE) ** -0.5 * LOG2E
    ones = jnp.ones((VT_ROWS - MLA_V, lat_ref.shape[0]), BF16)
    nt = (((1,), (1,)), ((), ()))
    for h in range(MLA_HEADS):
        a = lax.dot_general(wq_ref[h * 256:(h + 1) * 256, :], xq, nt, preferred_element_type=F32)
        rt = lax.dot_general(wrot_ref[h * 128:(h + 1) * 128, :], xq, nt, preferred_element_type=F32)
        q_ref[0, h, 0:128, :] = (a[0:128] * scale).astype(BF16)
        q_ref[0, h, 128:256, :] = ((a[128:256] * cost_ref[...] + rt * sint_ref[...]) * scale).astype(BF16)
        kn = jnp.dot(xkv, wk_ref[:, h * 128:(h + 1) * 128], preferred_element_type=F32)
        k_ref[0, h, :, 0:128] = kn.astype(BF16)
        k_ref[0, h, :, 128:256] = kpe
        vt = lax.dot_general(wvt_ref[h * 128:(h + 1) * 128, :], xkv, nt, preferred_element_type=F32)
        vt_ref[0, h, 0:MLA_V, :] = vt.astype(BF16)
        vt_ref[0, h, MLA_V:VT_ROWS, :] = ones


def _mla_up(proj, cos128, sin128, cos_t, sin_t, qn, kvn, wq, wrot, wk, wvt, B, S, tm=512):
    nt = S // tm
    lat_blk = COL_LAT // LAT_W
    const = lambda b, i: (0, 0)
    return pl.pallas_call(
        _mla_up_kernel,
        out_shape=(
            jax.ShapeDtypeStruct((B, MLA_HEADS, 256, S), BF16),
            jax.ShapeDtypeStruct((B, MLA_HEADS, S, 256), BF16),
            jax.ShapeDtypeStruct((B, MLA_HEADS, VT_ROWS, S), BF16),
        ),
        grid=(B, nt),
        in_specs=[
            pl.BlockSpec((tm, LAT_W), lambda b, i: (b * nt + i, lat_blk)),
            pl.BlockSpec((tm, 128), lambda b, i: (i, 0)),
            pl.BlockSpec((tm, 128), lambda b, i: (i, 0)),
            pl.BlockSpec((128, tm), lambda b, i: (0, i)),
            pl.BlockSpec((128, tm), lambda b, i: (0, i)),
            pl.BlockSpec((1, CQ_PAD), const),
            pl.BlockSpec((1, CKV_PAD), const),
            pl.BlockSpec((MLA_HEADS * 256, CQ_PAD), const),
            pl.BlockSpec((MLA_HEADS * 128, CQ_PAD), const),
            pl.BlockSpec((CKV_PAD, MLA_HEADS * MLA_NOPE), const),
            pl.BlockSpec((MLA_HEADS * MLA_V, CKV_PAD), const),
        ],
        out_specs=(
            pl.BlockSpec((1, MLA_HEADS, 256, tm), lambda b, i: (b, 0, 0, i)),
            pl.BlockSpec((1, MLA_HEADS, tm, 256), lambda b, i: (b, 0, i, 0)),
            pl.BlockSpec((1, MLA_HEADS, VT_ROWS, tm), lambda b, i: (b, 0, 0, i)),
        ),
        compiler_params=_cparams("parallel", "parallel"),
        name="mla_up",
    )(proj, cos128, sin128, cos_t, sin_t, qn, kvn, wq, wrot, wk, wvt)


def _mla_attn_kernel(q_ref, k_ref, vt_ref, o_ref, *, ck):
    qt = q_ref[0, 0]
    n_chunks = k_ref.shape[2] // ck
    m = acc = None

    def scores(c):
        return jnp.dot(k_ref[0, 0, c * ck:(c + 1) * ck, :], qt, preferred_element_type=F32)

    s_next = scores(0)
    for c in range(n_chunks):
        s = s_next
        if c + 1 < n_chunks:
            s_next = scores(c + 1)
        mc = jnp.max(s, axis=0, keepdims=True)
        m_new = mc if c == 0 else jnp.maximum(m, mc)
        p = jnp.exp2(s - m_new).astype(BF16)
        pv = jnp.dot(vt_ref[0, 0, :, c * ck:(c + 1) * ck], p, preferred_element_type=F32)
        acc = pv if c == 0 else acc * jnp.exp2(m - m_new) + pv
        m = m_new
    o = acc[0:MLA_V] / acc[MLA_V:MLA_V + 1]
    o_ref[...] = o.T.astype(o_ref.dtype)


def _mla_attn(q, k, vt, B, S, tq=1024, ck=512):
    nq = S // tq
    return pl.pallas_call(
        functools.partial(_mla_attn_kernel, ck=ck),
        out_shape=jax.ShapeDtypeStruct((B * S, MLA_HEADS * MLA_V), BF16),
        grid=(B, MLA_HEADS, nq),
        in_specs=[
            pl.BlockSpec((1, 1, 256, tq), lambda b, h, i: (b, h, 0, i)),
            pl.BlockSpec((1, 1, S, 256), lambda b, h, i: (b, h, 0, 0)),
            pl.BlockSpec((1, 1, VT_ROWS, S), lambda b, h, i: (b, h, 0, 0)),
        ],
        out_specs=pl.BlockSpec((tq, MLA_V), lambda b, h, i: (b * nq + i, h)),
        compiler_params=_cparams("parallel", "parallel", "arbitrary"),
        name="mla_attn",
    )(q, k, vt)


def _fnet_ch_kernel(lo_ref, hi_ref, w_ref, a_ref, b_ref):
    gd = FNET_GROUP_DIM
    lo = lo_ref[...].astype(F32)
    hi = hi_ref[...].astype(F32)
    for p, u in enumerate(((lo + hi).astype(BF16), (lo - hi).astype(BF16))):
        for g in range(FNET_GROUPS):
            ab = jnp.dot(u[:, g * gd:(g + 1) * gd], w_ref[...], preferred_element_type=F32)
            a_ref[p, :, g * gd:(g + 1) * gd] = ab[:, :gd].astype(a_ref.dtype)
            b_ref[p, :, g * gd:(g + 1) * gd] = ab[:, gd:].astype(b_ref.dtype)


def _fnet_channel(proj, w_cs, B, S, tm=1024):
    W = FNET_GROUPS * FNET_GROUP_DIM
    half = S // 2
    tm = min(tm, half)
    nh = half // tm
    out = jax.ShapeDtypeStruct((2, B * half, W), BF16)
    return pl.pallas_call(
        _fnet_ch_kernel,
        out_shape=(out, out),
        grid=(B, nh),
        in_specs=[
            pl.BlockSpec((tm, W), lambda b, i: (b * 2 * nh + i, COL_UF // W)),
            pl.BlockSpec((tm, W), lambda b, i: (b * 2 * nh + nh + i, COL_UF // W)),
            pl.BlockSpec((FNET_GROUP_DIM, 2 * FNET_GROUP_DIM), lambda b, i: (0, 0)),
        ],
        out_specs=(pl.BlockSpec((2, tm, W), lambda b, i: (0, b * nh + i, 0)),
                   pl.BlockSpec((2, tm, W), lambda b, i: (0, b * nh + i, 0))),
        compiler_params=_cparams("parallel", "parallel"),
        name="fnet_channel",
    )(proj, proj, w_cs)


def _fnet_pos_kernel(c_ref, s_ref, a_ref, b_ref, o_ref, *, scale):
    y = (jnp.dot(c_ref[0], a_ref[0], preferred_element_type=F32)
         + jnp.dot(s_ref[0], b_ref[0], preferred_element_type=F32))
    o_ref[...] = (y * scale).astype(o_ref.dtype)


def _fnet_position(c_tab, sn_tab, a, b, B, S, tm=1024, tn=512):
    W = FNET_GROUPS * FNET_GROUP_DIM
    half = S // 2
    tm = min(tm, half)
    nm = half // tm
    nn = W // tn
    scale = float((S * FNET_GROUP_DIM) ** -0.5)
    return pl.pallas_call(
        functools.partial(_fnet_pos_kernel, scale=scale),
        out_shape=jax.ShapeDtypeStruct((B * half, 2 * W), BF16),
        grid=(2, nm, B, nn),
        in_specs=[
            pl.BlockSpec((1, tm, half), lambda p, m, bb, n: (p, m, 0)),
            pl.BlockSpec((1, tm, half), lambda p, m, bb, n: (p, m, 0)),
            pl.BlockSpec((1, half, tn), lambda p, m, bb, n: (p, bb, n)),
            pl.BlockSpec((1, half, tn), lambda p, m, bb, n: (p, bb, n)),
        ],
        out_specs=pl.BlockSpec((tm, tn), lambda p, m, bb, n: (bb * nm + m, p * nn + n)),
        compiler_params=_cparams("parallel", "parallel", "parallel", "parallel"),
        name="fnet_position",
    )(c_tab, sn_tab, a, b)


def _dft_tables(S):
    gd = FNET_GROUP_DIM
    ck = (np.arange(gd)[:, None] * np.arange(gd)[None, :]) % gd
    ang = 2.0 * np.pi * ck / gd
    w_cs = jnp.asarray(np.concatenate([np.cos(ang), np.sin(ang)], axis=1), F32).astype(BF16)
    kb = 64
    half = S // 2
    n = lax.broadcasted_iota(jnp.int32, (1, half), 1)

    def thin(rows, period):
        ang_ = ((lax.broadcasted_iota(jnp.int32, (rows, 1), 0) * n) % period).astype(F32) * (2.0 * np.pi / period)
        return jnp.cos(ang_), jnp.sin(ang_)

    c_hi, s_hi = thin(S // kb, S // kb)
    c_lo, s_lo = thin(kb, S)
    c_hi, s_hi = c_hi[None, :, None, :], s_hi[None, :, None, :]
    split = lambda t: jnp.stack([t[0::2], t[1::2]])[:, None]
    c_lo, s_lo = split(c_lo), split(s_lo)
    cos_t = (c_hi * c_lo - s_hi * s_lo).reshape(2, half, half)
    nsin_t = -(s_hi * c_lo + c_hi * s_lo).reshape(2, half, half)
    return w_cs, cos_t.astype(BF16), nsin_t.astype(BF16)


def _merge_kernel(yn_ref, ym_ref, yf_ref, w_ref, g0_ref, g1_ref, g2_ref, o_ref, wbf_ref):
    @pl.when(pl.program_id(1) == 0)
    def _():
        wbf_ref[...] = w_ref[0].astype(BF16)

    acc = g0_ref[...].astype(F32) * jnp.dot(yn_ref[...], wbf_ref[0], preferred_element_type=F32)
    acc += g1_ref[...].astype(F32) * jnp.dot(ym_ref[...], wbf_ref[1], preferred_element_type=F32)
    acc += g2_ref[...].astype(F32) * jnp.dot(yf_ref[...], wbf_ref[2], preferred_element_type=F32)
    o_ref[...] = acc.astype(o_ref.dtype)


def _merge(y_na, y_mla, y_f, w_branch, layer, proj, tm=1024, tn=512):
    T = y_na.shape[0]
    ybs = pl.BlockSpec((tm, BRANCH_WIDTH), lambda j, i: (i, 0))

    def gate_spec(br):
        off = (COL_GATE + br * D_MODEL) // tn
        return pl.BlockSpec((tm, tn), lambda j, i: (i, off + j))

    return pl.pallas_call(
        _merge_kernel,
        out_shape=jax.ShapeDtypeStruct((T, D_MODEL), BF16),
        grid=(D_MODEL // tn, T // tm),
        in_specs=[ybs, ybs, ybs,
                  pl.BlockSpec((1, N_BRANCHES, BRANCH_WIDTH, tn), lambda j, i: (layer, 0, 0, j)),
                  gate_spec(0), gate_spec(1), gate_spec(2)],
        out_specs=pl.BlockSpec((tm, tn), lambda j, i: (i, j)),
        scratch_shapes=[pltpu.VMEM((N_BRANCHES, BRANCH_WIDTH, tn), BF16)],
        compiler_params=_cparams("parallel", "arbitrary"),
        name="merge",
    )(y_na, y_mla, y_f, w_branch, proj, proj, proj)


def _outproj_kernel(m_ref, w_ref, h_ref, o_ref, wbf_ref):
    @pl.when(pl.program_id(1) == 0)
    def _():
        wbf_ref[...] = w_ref[0].astype(BF16)

    o_ref[...] = h_ref[...] + jnp.dot(m_ref[...], wbf_ref[...], preferred_element_type=F32)


def _outproj(merged, w_o, layer, h2, tm=1024, tn=512):
    T = merged.shape[0]
    return pl.pallas_call(
        _outproj_kernel,
        out_shape=jax.ShapeDtypeStruct((T, D_MODEL), F32),
        grid=(D_MODEL // tn, T // tm),
        in_specs=[
            pl.BlockSpec((tm, D_MODEL), lambda j, i: (i, 0)),
            pl.BlockSpec((1, D_MODEL, tn), lambda j, i: (layer, 0, j)),
            pl.BlockSpec((tm, tn), lambda j, i: (i, j)),
        ],
        out_specs=pl.BlockSpec((tm, tn), lambda j, i: (i, j)),
        scratch_shapes=[pltpu.VMEM((D_MODEL, tn), BF16)],
        compiler_params=_cparams("parallel", "arbitrary"),
        name="outproj",
    )(merged, w_o, h2)


def _router_kernel(x_ref, g_ref, wrt_ref, aff_ref):
    x = x_ref[...]
    ms = jnp.mean(x * x, axis=-1, keepdims=True)
    xn = x * lax.rsqrt(ms + RMS_EPS) * g_ref[...]
    logits = lax.dot_general(wrt_ref[...], xn, (((1,), (1,)), ((), ())), preferred_element_type=F32,
                             precision=lax.Precision.HIGHEST)
    m = jnp.max(logits, axis=0, keepdims=True)
    e = jnp.exp(logits - m)
    aff_ref[0] = e / jnp.sum(e, axis=0, keepdims=True)


def _router(h2, gain, w_router_t, B, S, tm=1024):
    nt = S // tm
    return pl.pallas_call(
        _router_kernel,
        out_shape=jax.ShapeDtypeStruct((B, N_EXPERTS, S), F32),
        grid=(B, nt),
        in_specs=[
            pl.BlockSpec((tm, D_MODEL), lambda b, i: (b * nt + i, 0)),
            pl.BlockSpec((1, D_MODEL), lambda b, i: (0, 0)),
            pl.BlockSpec((N_EXPERTS, D_MODEL), lambda b, i: (0, 0)),
        ],
        out_specs=pl.BlockSpec((1, N_EXPERTS, tm), lambda b, i: (b, 0, i)),
        compiler_params=_cparams("parallel", "parallel"),
        name="router",
    )(h2, gain, w_router_t)


def _route_kernel(a_ref, val_ref, gate_ref, starts_ref, kmax_ref, pos_s, pv_s, a_s, *, cap, tt):
    E, S = a_ref.shape[1], a_ref.shape[2]
    a = a_ref[0]

    def as_float(b):
        return lax.bitcast_convert_type(b, F32)

    def bisect(_, carry):
        lo, hi = carry
        mid = lo + ((hi - lo + 1) >> 1)
        cnt = jnp.sum(jnp.where(a >= as_float(mid), 1.0, 0.0), axis=1, keepdims=True)
        ge = cnt >= cap
        return jnp.where(ge, mid, lo), jnp.where(ge, hi, mid - 1)

    lo0 = jnp.zeros((E, 1), I32)
    hi0 = jnp.full((E, 1), 0x7F7FFFFF, I32)
    thr_bits, _ = lax.fori_loop(0, 31, bisect, (lo0, hi0))
    thr, thr_up = as_float(thr_bits), as_float(thr_bits + 1)

    r_i = lax.broadcasted_iota(I32, (PFX_BLK, PFX_BLK), 0)
    c_i = lax.broadcasted_iota(I32, (PFX_BLK, PFX_BLK), 1)
    tri = jnp.where(r_i < c_i, 1.0, 0.0).astype(BF16)

    def excl_prefix(mask):
        x = jnp.where(mask, 1.0, 0.0).astype(BF16)
        carry = jnp.zeros((E, 1), F32)
        outs = []
        for j in range(S // PFX_BLK):
            blk = x[:, j * PFX_BLK:(j + 1) * PFX_BLK]
            outs.append(jnp.dot(blk, tri, preferred_element_type=F32) + carry)
            carry = carry + jnp.sum(blk.astype(F32), axis=1, keepdims=True)
        return jnp.concatenate(outs, axis=1)

    gt = a >= thr_up
    eq = (a >= thr) & (a < thr_up)
    need = cap - jnp.sum(jnp.where(gt, 1.0, 0.0), axis=1, keepdims=True)
    sel = gt | (eq & (excl_prefix(eq) < need))
    pos = excl_prefix(sel)

    self_ = jnp.where(sel, 1.0, 0.0)
    e_r = lax.broadcasted_iota(I32, (E, E), 0)
    e_c = lax.broadcasted_iota(I32, (E, E), 1)
    low = jnp.where(e_c < e_r, 1.0, 0.0).astype(BF16)
    rank = jnp.dot(low, self_.astype(BF16), preferred_element_type=F32)
    tok = lax.broadcasted_iota(I32, (E, S), 1)
    pos_s[...] = jnp.where(sel, pos.astype(I32), -1)
    pv_s[...] = (tok + (rank.astype(I32) << TOK_BITS)).astype(F32)
    a_s[...] = a

    lane = lax.broadcasted_iota(I32, (E, 128), 1)
    starts = jnp.full((E, 128), cap, I32)
    kcount = jnp.sum(self_, axis=0, keepdims=True)
    kmax = jnp.zeros((8, 128), I32)
    lane8 = lax.broadcasted_iota(I32, (8, 128), 1)
    posi = pos.astype(I32)
    for j in range(S // tt):
        starts = jnp.where(lane == j, posi[:, j * tt:j * tt + 1], starts)
        kj = jnp.max(kcount[:, j * tt:(j + 1) * tt], axis=1, keepdims=True).astype(I32)
        kmax = jnp.where(lane8 == j, kj, kmax)
    starts_ref[0] = starts
    kmax_ref[0] = kmax

    c_iota = lax.broadcasted_iota(I32, (cap, PFX_BLK), 0)
    lane_c = lax.broadcasted_iota(I32, (cap, 128), 1)

    def per_expert(e, carry):
        val_acc, gate_acc = carry
        v_part = jnp.zeros((cap, PFX_BLK), F32)
        g_part = jnp.zeros((cap, PFX_BLK), F32)
        for j in range(S // PFX_BLK):
            sl = pl.ds(j * PFX_BLK, PFX_BLK)
            hit = pos_s[pl.ds(e, 1), sl] == c_iota
            v_part = v_part + jnp.where(hit, pv_s[pl.ds(e, 1), sl], 0.0)
            g_part = g_part + jnp.where(hit, a_s[pl.ds(e, 1), sl], 0.0)
        v = jnp.sum(v_part, axis=1, keepdims=True).astype(I32)
        g = jnp.sum(g_part, axis=1, keepdims=True)
        return jnp.where(lane_c == e, v, val_acc), jnp.where(lane_c == e, g, gate_acc)

    val, gate = lax.fori_loop(0, E, per_expert, (jnp.zeros((cap, 128), I32), jnp.zeros((cap, 128), F32)))
    val_ref[0] = val
    gate_ref[0] = gate


def _route(aff_t, cap, tt=COMB_TT):
    B, E, S = aff_t.shape
    assert S <= (1 << TOK_BITS) and S // tt < 128 and S % PFX_BLK == 0
    return pl.pallas_call(
        functools.partial(_route_kernel, cap=cap, tt=tt),
        out_shape=(
            jax.ShapeDtypeStruct((B, cap, 128), I32),
            jax.ShapeDtypeStruct((B, cap, 128), F32),
            jax.ShapeDtypeStruct((B, E, 128), I32),
            jax.ShapeDtypeStruct((B, 8, 128), I32),
        ),
        grid=(B,),
        in_specs=[pl.BlockSpec((1, E, S), lambda b: (b, 0, 0))],
        out_specs=(
            pl.BlockSpec((1, cap, 128), lambda b: (b, 0, 0)),
            pl.BlockSpec((1, cap, 128), lambda b: (b, 0, 0)),
            pl.BlockSpec((1, E, 128), lambda b: (b, 0, 0)),
            pl.BlockSpec((1, 8, 128), lambda b: (b, 0, 0)),
        ),
        scratch_shapes=[pltpu.VMEM((E, S), I32), pltpu.VMEM((E, S), F32), pltpu.VMEM((E, S), F32)],
        compiler_params=_cparams("parallel"),
        name="route",
    )(aff_t)


def _gather_kernel(rows_ref, h_hbm, g_ref, o_ref, buf, sem, *, gt):
    i = pl.program_id(0)
    slot = i % 2

    def issue_tile(t, s):
        def issue(r, _):
            pltpu.make_async_copy(h_hbm.at[pl.ds(rows_ref[t * gt + r], 1)], buf.at[s, pl.ds(r, 1)], sem.at[s]).start()
            return 0

        lax.fori_loop(0, gt, issue, 0, unroll=8)

    @pl.when(i == 0)
    def _():
        issue_tile(0, 0)

    @pl.when(i + 1 < pl.num_programs(0))
    def _():
        issue_tile(i + 1, 1 - slot)

    pltpu.make_async_copy(h_hbm.at[pl.ds(0, gt)], buf.at[slot], sem.at[slot]).wait()
    x = buf[slot]
    ms = jnp.mean(x * x, axis=-1, keepdims=True)
    o_ref[...] = (x * lax.rsqrt(ms + RMS_EPS) * g_ref[...]).astype(o_ref.dtype)


def _gather_norm(rows, h2, gain, gt=1024):
    n = rows.shape[0]
    gt = min(gt, n)
    return pl.pallas_call(
        functools.partial(_gather_kernel, gt=gt),
        out_shape=jax.ShapeDtypeStruct((n, D_MODEL), BF16),
        grid_spec=pltpu.PrefetchScalarGridSpec(
            num_scalar_prefetch=1,
            grid=(n // gt,),
            in_specs=[pl.BlockSpec(memory_space=pl.ANY), pl.BlockSpec((1, D_MODEL), lambda i, rows: (0, 0))],
            out_specs=pl.BlockSpec((gt, D_MODEL), lambda i, rows: (i, 0)),
            scratch_shapes=[pltpu.VMEM((2, gt, D_MODEL), F32), pltpu.SemaphoreType.DMA((2,))],
        ),
        compiler_params=_cparams("arbitrary", disable_bounds_checks=True),
        name="gather_norm",
    )(rows, h2, gain)


def _combine_kernel(dst_ref, starts_ref, kmax_ref, h_ref, ye_hbm, fg_ref, o_ref, planes, sem, *,
                    B, E, cap, tt, nt_pad, final_norm):
    b = pl.program_id(0)
    j = pl.program_id(1)
    k = kmax_ref[b * nt_pad + j]

    def zero(p, _):
        planes[pl.ds(pl.multiple_of(p * tt, tt), tt), :] = jnp.zeros((tt, D_MODEL), F32)
        return 0

    lax.fori_loop(0, k, zero, 0)

    def row_copy(slot, dst_row):
        return pltpu.make_async_copy(ye_hbm.at[pl.ds(slot, 1)], planes.at[pl.ds(dst_row, 1)], sem)

    def per_expert(e, n):
        sbase = (b * E + e) * nt_pad + j
        c0 = starts_ref[sbase]
        c1 = starts_ref[sbase + 1]
        lbase = (e * B + b) * cap

        def issue(c):
            row_copy(lbase + c, dst_ref[lbase + c]).start()

        def issue4(i, _):
            for u in range(4):
                issue(c0 + 4 * i + u)
            return 0

        def issue1(i, _):
            issue(c1 - 1 - i)
            return 0

        cnt = c1 - c0
        lax.fori_loop(0, cnt >> 2, issue4, 0)
        lax.fori_loop(0, cnt & 3, issue1, 0)
        return n + cnt

    n = lax.fori_loop(0, E, per_expert, 0)

    for bit in range((E * tt).bit_length()):
        @pl.when(((n >> bit) & 1) == 1)
        def _():
            rows = 1 << bit
            pltpu.make_async_copy(ye_hbm.at[pl.ds(0, rows)], planes.at[pl.ds(0, rows)], sem).wait()

    for cs in range(D_MODEL // 128):
        cols = slice(cs * 128, (cs + 1) * 128)

        def add(p, acc):
            return acc + planes[pl.ds(pl.multiple_of(p * tt, tt), tt), cols]

        o_ref[:, cols] = lax.fori_loop(0, k, add, h_ref[:, cols])

    if final_norm:
        x = o_ref[...]
        ms = jnp.mean(x * x, axis=-1, keepdims=True)
        o_ref[...] = x * lax.rsqrt(ms + RMS_EPS) * fg_ref[...]


def _combine(h2, ye, dst_rows, starts, kmax, final_gain, B, S, cap, tt=COMB_TT):
    E = N_EXPERTS
    nt = S // tt
    nt_pad = starts.shape[0] // (B * E)
    final_norm = final_gain is not None
    fg = final_gain if final_norm else jnp.ones((1, D_MODEL), F32)
    return pl.pallas_call(
        functools.partial(_combine_kernel, B=B, E=E, cap=cap, tt=tt, nt_pad=nt_pad, final_norm=final_norm),
        out_shape=jax.ShapeDtypeStruct(h2.shape, F32),
        grid_spec=pltpu.PrefetchScalarGridSpec(
            num_scalar_prefetch=3,
            grid=(B, nt),
            in_specs=[pl.BlockSpec((tt, D_MODEL), lambda b, j, *_: (b * nt + j, 0)),
                      pl.BlockSpec(memory_space=pl.ANY),
                      pl.BlockSpec((1, D_MODEL), lambda b, j, *_: (0, 0))],
            out_specs=pl.BlockSpec((tt, D_MODEL), lambda b, j, *_: (b * nt + j, 0)),
            scratch_shapes=[pltpu.VMEM((E * tt, D_MODEL), F32), pltpu.SemaphoreType.DMA(())],
        ),
        compiler_params=_cparams("arbitrary", "arbitrary", disable_bounds_checks=True),
        name="combine",
    )(dst_rows, starts, kmax, h2, ye, fg)


def _ffn_up_kernel(x_ref, wg_ref, wu_ref, o_ref):
    x = x_ref[...]
    a = jnp.dot(x, wg_ref[0, 0].astype(BF16), preferred_element_type=F32)
    u = jnp.dot(x, wu_ref[0, 0].astype(BF16), preferred_element_type=F32)
    o_ref[...] = (a * (1.0 / (1.0 + jnp.exp(-a))) * u).astype(o_ref.dtype)


def _ffn_down_kernel(hid_ref, wd_ref, gate_ref, o_ref):
    y = jnp.dot(hid_ref[...], wd_ref[0, 0].astype(BF16), preferred_element_type=F32)
    o_ref[...] = y * gate_ref[...]


def _experts(xe, wg, wu, wd, layer, gate, tf=256, tn=512):
    n_tok = xe.shape[0]
    per_e = n_tok // N_EXPERTS
    ff = wg.shape[-1]
    hid = pl.pallas_call(
        _ffn_up_kernel,
        out_shape=jax.ShapeDtypeStruct((n_tok, ff), BF16),
        grid=(N_EXPERTS, ff // tf),
        in_specs=[
            pl.BlockSpec((per_e, D_MODEL), lambda e, f: (e, 0)),
            pl.BlockSpec((1, 1, D_MODEL, tf), lambda e, f: (layer, e, 0, f)),
            pl.BlockSpec((1, 1, D_MODEL, tf), lambda e, f: (layer, e, 0, f)),
        ],
        out_specs=pl.BlockSpec((per_e, tf), lambda e, f: (e, f)),
        compiler_params=_cparams("parallel", "arbitrary"),
        name="ffn_up",
    )(xe, wg, wu)
    return pl.pallas_call(
        _ffn_down_kernel,
        out_shape=jax.ShapeDtypeStruct((n_tok, D_MODEL), F32),
        grid=(N_EXPERTS, D_MODEL // tn),
        in_specs=[
            pl.BlockSpec((per_e, ff), lambda e, n: (e, 0)),
            pl.BlockSpec((1, 1, ff, tn), lambda e, n: (layer, e, 0, n)),
            pl.BlockSpec((per_e, 1), lambda e, n: (e, 0)),
        ],
        out_specs=pl.BlockSpec((per_e, tn), lambda e, n: (e, n)),
        compiler_params=_cparams("parallel", "arbitrary"),
        name="ffn_down",
    )(hid, wd, gate)


def _moe_layer(h2, gain, w_router, wg, wu, wd, layer, final_gain, B, S):
    E = N_EXPERTS
    cap = CAPACITY_FACTOR * S // E
    nt = S // COMB_TT
    aff_t = _router(h2, gain, jnp.transpose(w_router), B, S)
    val, gate, starts, kmax = _route(aff_t, cap)
    val_e = jnp.transpose(val[:, :, :E], (2, 0, 1))
    gate_e = jnp.transpose(gate[:, :, :E], (2, 0, 1)).reshape(-1, 1)
    tok_e = val_e & ((1 << TOK_BITS) - 1)
    rows = (tok_e + (jnp.arange(B, dtype=I32) * S)[None, :, None]).reshape(-1)
    xe = _gather_norm(rows, h2, gain)
    ye = _experts(xe, wg, wu, wd, layer, gate_e)
    dst = ((val_e >> TOK_BITS) * COMB_TT + tok_e % COMB_TT).reshape(-1)
    starts_flat = starts[:, :, :nt + 1].reshape(-1)
    kmax_flat = kmax[:, 0, :nt + 1].reshape(-1)
    return _combine(h2, ye, dst, starts_flat, kmax_flat, final_gain, B, S, cap)


def _pad_cols(w, n):
    return jnp.pad(w, ((0, 0), (0, n - w.shape[1])))


def _rot_cols(w):
    half = w.shape[1] // 2
    return jnp.concatenate([-w[:, half:], w[:, :half]], axis=1)


def _prep_in_weights(w_in, b_gate):
    o = np.cumsum([0, 3 * NA_WIDTH, MLA_Q_LORA, MLA_KV_LORA, MLA_ROPE, FNET_GROUPS * FNET_GROUP_DIM])
    qkv, cq, ckv, kr, uf, gl = (w_in[:, o[0]:o[1]], w_in[:, o[1]:o[2]], w_in[:, o[2]:o[3]], w_in[:, o[3]:o[4]],
                                w_in[:, o[4]:o[5]], w_in[:, o[5]:])
    w_all = jnp.concatenate([qkv, uf, _pad_cols(cq, CQ_PAD), _pad_cols(ckv, CKV_PAD), _pad_cols(kr, 128),
                             _pad_cols(_rot_cols(kr), 128), gl], axis=1).astype(BF16)
    b_all = jnp.concatenate([jnp.zeros((COL_GATE,), F32), b_gate]).reshape(1, PROJ_COLS)
    return w_all, b_all


def _prep_mla_weights(w_uq, q_norm, w_ukv, kv_norm):
    qd = MLA_NOPE + MLA_ROPE
    wq3 = w_uq.reshape(MLA_Q_LORA, MLA_HEADS, qd)
    nope, pe = wq3[:, :, :MLA_NOPE], wq3[:, :, MLA_NOPE:]
    z64 = jnp.zeros((MLA_Q_LORA, MLA_HEADS, 64), F32)
    wq = jnp.concatenate([nope, pe, z64], axis=2).reshape(MLA_Q_LORA, MLA_HEADS * 256)
    rot = jnp.concatenate([-pe[:, :, 32:], pe[:, :, :32], z64], axis=2).reshape(MLA_Q_LORA, MLA_HEADS * 128)
    rpad = ((0, CQ_PAD - MLA_Q_LORA), (0, 0))
    wq = jnp.transpose(jnp.pad(wq, rpad)).astype(BF16)
    rot = jnp.transpose(jnp.pad(rot, rpad)).astype(BF16)
    wkv3 = jnp.pad(w_ukv, ((0, CKV_PAD - MLA_KV_LORA), (0, 0))).reshape(CKV_PAD, MLA_HEADS, MLA_NOPE + MLA_V)
    wk = wkv3[:, :, :MLA_NOPE].reshape(CKV_PAD, MLA_HEADS * MLA_NOPE).astype(BF16)
    wvt = jnp.transpose(wkv3[:, :, MLA_NOPE:], (1, 2, 0)).reshape(MLA_HEADS * MLA_V, CKV_PAD).astype(BF16)
    qn = jnp.pad(q_norm, (0, CQ_PAD - MLA_Q_LORA)).reshape(1, CQ_PAD)
    kvn = jnp.pad(kv_norm, (0, CKV_PAD - MLA_KV_LORA)).reshape(1, CKV_PAD)
    return wq, rot, wk, wvt, qn, kvn


def _rope_tables128(S):
    pos = jnp.arange(S, dtype=F32)
    inv = 1.0 / (ROPE_THETA ** (jnp.arange(0, MLA_ROPE, 2, dtype=F32) / MLA_ROPE))
    ang = pos[:, None] * inv[None, :]
    z = jnp.zeros((S, 64), F32)
    cos, sin = jnp.cos(ang), jnp.sin(ang)
    return jnp.concatenate([cos, cos, z], axis=1), jnp.concatenate([sin, sin, z], axis=1)


def kernel(x, w_in, b_gate, w_uq, q_norm, w_ukv, kv_norm, na_rpb, w_branch, w_o, norm_mix, norm_moe,
           w_router, w_exp_gate, w_exp_up, w_exp_down, norm_final):
    B, S, D = x.shape
    T = B * S
    depth = w_in.shape[0]
    cos128, sin128 = _rope_tables128(S)
    cos_t, sin_t = jnp.transpose(cos128), jnp.transpose(sin128)
    w_cs, c_tab, sn_tab = _dft_tables(S)
    h = x.reshape(T, D)
    for l in range(depth):
        w_all, b_all = _prep_in_weights(w_in[l], b_gate[l])
        proj = _inproj(h, norm_mix[l].reshape(1, D), w_all, b_all)
        y_na = _na_attention(proj, _na_bias_tables(na_rpb[l]), B, S)
        wq, wrot, wk, wvt, qn, kvn = _prep_mla_weights(w_uq[l], q_norm[l], w_ukv[l], kv_norm[l])
        q, k, vt = _mla_up(proj, cos128, sin128, cos_t, sin_t, qn, kvn, wq, wrot, wk, wvt, B, S)
        y_mla = _mla_attn(q, k, vt, B, S)
        fa, fb = _fnet_channel(proj, w_cs, B, S)
        y_f = _fnet_position(c_tab, sn_tab, fa, fb, B, S).reshape(T, FNET_GROUPS * FNET_GROUP_DIM)
        merged = _merge(y_na, y_mla, y_f, w_branch, l, proj)
        h = _outproj(merged, w_o, l, h)
        final_gain = norm_final.reshape(1, D) if l == depth - 1 else None
        h = _moe_layer(h, norm_moe[l].reshape(1, D), w_router[l], w_exp_gate, w_exp_up, w_exp_down, l, final_gain,
                       B, S)
    return h.reshape(B, S, D)
```

```python
import functools

import numpy as np
import jax
import jax.numpy as jnp
from jax import lax
from jax.experimental import pallas as pl
from jax.experimental.pallas import tpu as pltpu

D_MODEL = 2048
GRID_W = 64
NA_HEADS = 16
NA_HEAD_DIM = 64
NA_WIDTH = NA_HEADS * NA_HEAD_DIM
NA_KH = 8
NA_KW = 16
MLA_HEADS = 8
MLA_NOPE = 128
MLA_ROPE = 64
MLA_V = 128
MLA_Q_LORA = 448
MLA_KV_LORA = 160
ROPE_THETA = 10000.0
FNET_GROUPS = 4
FNET_GROUP_DIM = 256
N_BRANCHES = 3
BRANCH_WIDTH = 1024
N_EXPERTS = 16
EXPERT_FF = 2048
CAPACITY_FACTOR = 2
RMS_EPS = 1e-6
NEG_INF = -1e30

F32 = jnp.float32
BF16 = jnp.bfloat16
I32 = jnp.int32

COL_QKV = 0
COL_UF = 3072
COL_LAT = 4096
COL_GATE = 5120
PROJ_COLS = COL_GATE + N_BRANCHES * D_MODEL
LAT_W = 1024
CQ_PAD = 512
CKV_PAD = 256
VT_ROWS = MLA_V + 16
LOG2E = 1.4426950408889634
TOK_BITS = 12
PFX_BLK = 512
COMB_TT = 128
NA_ROWS_PER_STEP = 2

VMEM_LIMIT = 56 * 1024 * 1024


def _cparams(*sem, **kw):
    return pltpu.CompilerParams(dimension_semantics=sem, vmem_limit_bytes=VMEM_LIMIT, **kw)


def _inproj_kernel(x_ref, g_ref, w_ref, b_ref, o_ref, xn_ref, *, gate_tile0):
    j = pl.program_id(1)

    @pl.when(j == 0)
    def _():
        x = x_ref[...]
        ms = jnp.mean(x * x, axis=-1, keepdims=True)
        xn_ref[...] = (x * lax.rsqrt(ms + RMS_EPS) * g_ref[...]).astype(BF16)

    acc = jnp.dot(xn_ref[...], w_ref[...], preferred_element_type=F32)

    @pl.when(j < gate_tile0)
    def _():
        o_ref[...] = acc.astype(o_ref.dtype)

    @pl.when(j >= gate_tile0)
    def _():
        z = acc + b_ref[...]
        o_ref[...] = (0.5 * jnp.tanh(0.5 * z) + 0.5).astype(o_ref.dtype)


def _inproj(h2, gain, w_all, b_all, tm=1024, tn=1024):
    T = h2.shape[0]
    return pl.pallas_call(
        functools.partial(_inproj_kernel, gate_tile0=COL_GATE // tn),
        out_shape=jax.ShapeDtypeStruct((T, PROJ_COLS), BF16),
        grid=(T // tm, PROJ_COLS // tn),
        in_specs=[
            pl.BlockSpec((tm, D_MODEL), lambda i, j: (i, 0)),
            pl.BlockSpec((1, D_MODEL), lambda i, j: (0, 0)),
            pl.BlockSpec((D_MODEL, tn), lambda i, j: (0, j)),
            pl.BlockSpec((1, tn), lambda i, j: (0, j)),
        ],
        out_specs=pl.BlockSpec((tm, tn), lambda i, j: (i, j)),
        scratch_shapes=[pltpu.VMEM((tm, D_MODEL), BF16)],
        compiler_params=_cparams("parallel", "arbitrary"),
        name="inproj",
    )(h2, gain, w_all, b_all)


def _na_kernel(q_ref, k_ref, v_ref, *rest, rows, rps):
    bias_refs, o_ref = rest[:rps], rest[rps]
    i = pl.program_id(1)
    nk = NA_KH * GRID_W
    lo = lax.broadcasted_iota(jnp.int32, (GRID_W, 128), 1) < NA_HEAD_DIM
    nt = (((1,), (1,)), ((), ()))
    row0 = []
    for rr in range(rps):
        start = jnp.clip(rps * i + rr - NA_KH // 2, 0, rows - NA_KH)
        row0.append(pl.multiple_of(start * GRID_W, GRID_W))

    def scores(rr, hp):
        sl = slice(hp * 128, (hp + 1) * 128)
        q2 = q_ref[rr * GRID_W:(rr + 1) * GRID_W, sl]
        zero = jnp.zeros_like(q2)
        qq = jnp.concatenate([jnp.where(lo, q2, zero), jnp.where(lo, zero, q2)], axis=0)
        return lax.dot_general(qq, k_ref[pl.ds(row0[rr], nk), sl], nt, preferred_element_type=F32)

    chains = [(rr, hp) for hp in range(NA_HEADS // 2) for rr in range(rps)]
    s_next = scores(*chains[0])
    for n, (rr, hp) in enumerate(chains):
        s = s_next
        if n + 1 < len(chains):
            s_next = scores(*chains[n + 1])
        sl = slice(hp * 128, (hp + 1) * 128)
        s = s * (NA_HEAD_DIM ** -0.5 * LOG2E) + bias_refs[rr][0, hp]
        m = jnp.max(s, axis=-1, keepdims=True)
        p = jnp.exp2(s - m)
        l = jnp.sum(p, axis=-1, keepdims=True)
        o = jnp.dot(p.astype(BF16), v_ref[pl.ds(row0[rr], nk), sl], preferred_element_type=F32) / l
        o_ref[rr * GRID_W:(rr + 1) * GRID_W, sl] = jnp.where(lo, o[:GRID_W], o[GRID_W:]).astype(o_ref.dtype)


def _na_bias_tables(rpb):
    cols = np.arange(GRID_W)
    col_start = np.clip(cols - NA_KW // 2, 0, GRID_W - NA_KW)
    col_valid = (cols[None, :] >= col_start[:, None]) & (cols[None, :] < col_start[:, None] + NA_KW)
    col_idx = np.clip(cols[None, :] - cols[:, None] + NA_KW - 1, 0, 2 * NA_KW - 2)
    onehot = (col_idx[None] == np.arange(2 * NA_KW - 1)[:, None, None]).astype(np.float32)
    toep = jnp.einsum("hdc,cqj->hdqj", rpb.astype(F32), jnp.asarray(onehot), precision=lax.Precision.HIGHEST)
    win = jnp.stack([toep[:, NA_KH - 1 - t:2 * NA_KH - 1 - t] for t in range(NA_KH)])
    win = jnp.where(col_valid[None, None, None], win * LOG2E, NEG_INF * LOG2E)
    win = jnp.transpose(win, (0, 1, 3, 2, 4))
    return win.reshape(NA_KH, NA_HEADS // 2, 2 * GRID_W, NA_KH * GRID_W).astype(F32)


def _na_attention(proj, bias, B, S):
    rows = S // GRID_W
    rps = NA_ROWS_PER_STEP
    assert rows >= NA_KH and rows % rps == 0
    T = B * S
    steps = rows // rps

    def bias_spec(rr):
        def bias_map(b, i):
            r = rps * i + rr
            return (r - jnp.clip(r - NA_KH // 2, 0, rows - NA_KH), 0, 0, 0)
        return pl.BlockSpec((1, NA_HEADS // 2, 2 * GRID_W, NA_KH * GRID_W), bias_map)

    return pl.pallas_call(
        functools.partial(_na_kernel, rows=rows, rps=rps),
        out_shape=jax.ShapeDtypeStruct((T, NA_WIDTH), BF16),
        grid=(B, steps),
        in_specs=[
            pl.BlockSpec((rps * GRID_W, NA_WIDTH), lambda b, i: (b * steps + i, 0)),
            pl.BlockSpec((S, NA_WIDTH), lambda b, i: (b, 1)),
            pl.BlockSpec((S, NA_WIDTH), lambda b, i: (b, 2)),
            *[bias_spec(rr) for rr in range(rps)],
        ],
        out_specs=pl.BlockSpec((rps * GRID_W, NA_WIDTH), lambda b, i: (b * steps + i, 0)),
        compiler_params=_cparams("parallel", "arbitrary"),
        name="na_attn",
    )(proj, proj, proj, *([bias] * rps))


def _mla_up_kernel(lat_ref, cos_ref, sin_ref, cost_ref, sint_ref, qn_ref, kvn_ref, wq_ref, wrot_ref, wk_ref,
                   wvt_ref, q_ref, k_ref, vt_ref):
    cq = lat_ref[:, 0:CQ_PAD].astype(F32)
    ms = jnp.sum(cq * cq, axis=-1, keepdims=True) * (1.0 / MLA_Q_LORA)
    xq = (cq * lax.rsqrt(ms + RMS_EPS) * qn_ref[...]).astype(BF16)
    ckv = lat_ref[:, CQ_PAD:CQ_PAD + CKV_PAD].astype(F32)
    ms2 = jnp.sum(ckv * ckv, axis=-1, keepdims=True) * (1.0 / MLA_KV_LORA)
    xkv = (ckv * lax.rsqrt(ms2 + RMS_EPS) * kvn_ref[...]).astype(BF16)
    cos = cos_ref[...]
    sin = sin_ref[...]
    kpe = (lat_ref[:, 768:896].astype(F32) * cos + lat_ref[:, 896:1024].astype(F32) * sin).astype(BF16)
    scale = (MLA_NOPE + MLA_ROPE) ** -0.5 * LOG2E
    ones = jnp.ones((VT_ROWS - MLA_V, lat_ref.shape[0]), BF16)
    nt = (((1,), (1,)), ((), ()))
    for h in range(MLA_HEADS):
        a = lax.dot_general(wq_ref[h * 256:(h + 1) * 256, :], xq, nt, preferred_element_type=F32)
        rt = lax.dot_general(wrot_ref[h * 128:(h + 1) * 128, :], xq, nt, preferred_element_type=F32)
        q_ref[0, h, 0:128, :] = (a[0:128] * scale).astype(BF16)
        q_ref[0, h, 128:256, :] = ((a[128:256] * cost_ref[...] + rt * sint_ref[...]) * scale).astype(BF16)
        kn = jnp.dot(xkv, wk_ref[:, h * 128:(h + 1) * 128], preferred_element_type=F32)
        k_ref[0, h, :, 0:128] = kn.astype(BF16)
        k_ref[0, h, :, 128:256] = kpe
        vt = lax.dot_general(wvt_ref[h * 128:(h + 1) * 128, :], xkv, nt, preferred_element_type=F32)
        vt_ref[0, h, 0:MLA_V, :] = vt.astype(BF16)
        vt_ref[0, h, MLA_V:VT_ROWS, :] = ones


def _mla_up(proj, cos128, sin128, cos_t, sin_t, qn, kvn, wq, wrot, wk, wvt, B, S, tm=512):
    nt = S // tm
    lat_blk = COL_LAT // LAT_W
    const = lambda b, i: (0, 0)
    return pl.pallas_call(
        _mla_up_kernel,
        out_shape=(
            jax.ShapeDtypeStruct((B, MLA_HEADS, 256, S), BF16),
            jax.ShapeDtypeStruct((B, MLA_HEADS, S, 256), BF16),
            jax.ShapeDtypeStruct((B, MLA_HEADS, VT_ROWS, S), BF16),
        ),
        grid=(B, nt),
        in_specs=[
            pl.BlockSpec((tm, LAT_W), lambda b, i: (b * nt + i, lat_blk)),
            pl.BlockSpec((tm, 128), lambda b, i: (i, 0)),
            pl.BlockSpec((tm, 128), lambda b, i: (i, 0)),
            pl.BlockSpec((128, tm), lambda b, i: (0, i)),
            pl.BlockSpec((128, tm), lambda b, i: (0, i)),
            pl.BlockSpec((1, CQ_PAD), const),
            pl.BlockSpec((1, CKV_PAD), const),
            pl.BlockSpec((MLA_HEADS * 256, CQ_PAD), const),
            pl.BlockSpec((MLA_HEADS * 128, CQ_PAD), const),
            pl.BlockSpec((CKV_PAD, MLA_HEADS * MLA_NOPE), const),
            pl.BlockSpec((MLA_HEADS * MLA_V, CKV_PAD), const),
        ],
        out_specs=(
            pl.BlockSpec((1, MLA_HEADS, 256, tm), lambda b, i: (b, 0, 0, i)),
            pl.BlockSpec((1, MLA_HEADS, tm, 256), lambda b, i: (b, 0, i, 0)),
            pl.BlockSpec((1, MLA_HEADS, VT_ROWS, tm), lambda b, i: (b, 0, 0, i)),
        ),
        compiler_params=_cparams("parallel", "parallel"),
        name="mla_up",
    )(proj, cos128, sin128, cos_t, sin_t, qn, kvn, wq, wrot, wk, wvt)


def _mla_attn_kernel(q_ref, k_ref, vt_ref, o_ref, *, ck):
    qt = q_ref[0, 0]
    n_chunks = k_ref.shape[2] // ck
    m = acc = None

    def scores(c):
        return jnp.dot(k_ref[0, 0, c * ck:(c + 1) * ck, :], qt, preferred_element_type=F32)

    s_next = scores(0)
    for c in range(n_chunks):
        s = s_next
        if c + 1 < n_chunks:
            s_next = scores(c + 1)
        mc = jnp.max(s, axis=0, keepdims=True)
        m_new = mc if c == 0 else jnp.maximum(m, mc)
        p = jnp.exp2(s - m_new).astype(BF16)
        pv = jnp.dot(vt_ref[0, 0, :, c * ck:(c + 1) * ck], p, preferred_element_type=F32)
        acc = pv if c == 0 else acc * jnp.exp2(m - m_new) + pv
        m = m_new
    o = acc[0:MLA_V] / acc[MLA_V:MLA_V + 1]
    o_ref[...] = o.T.astype(o_ref.dtype)


def _mla_attn(q, k, vt, B, S, tq=1024, ck=512):
    nq = S // tq
    return pl.pallas_call(
        functools.partial(_mla_attn_kernel, ck=ck),
        out_shape=jax.ShapeDtypeStruct((B * S, MLA_HEADS * MLA_V), BF16),
        grid=(B, MLA_HEADS, nq),
        in_specs=[
            pl.BlockSpec((1, 1, 256, tq), lambda b, h, i: (b, h, 0, i)),
            pl.BlockSpec((1, 1, S, 256), lambda b, h, i: (b, h, 0, 0)),
            pl.BlockSpec((1, 1, VT_ROWS, S), lambda b, h, i: (b, h, 0, 0)),
        ],
        out_specs=pl.BlockSpec((tq, MLA_V), lambda b, h, i: (b * nq + i, h)),
        compiler_params=_cparams("parallel", "parallel", "arbitrary"),
        name="mla_attn",
    )(q, k, vt)


def _fnet_ch_kernel(lo_ref, hi_ref, w_ref, a_ref, b_ref):
    gd = FNET_GROUP_DIM
    lo = lo_ref[...].astype(F32)
    hi = hi_ref[...].astype(F32)
    for p, u in enumerate(((lo + hi).astype(BF16), (lo - hi).astype(BF16))):
        for g in range(FNET_GROUPS):
            ab = jnp.dot(u[:, g * gd:(g + 1) * gd], w_ref[...], preferred_element_type=F32)
            a_ref[p, :, g * gd:(g + 1) * gd] = ab[:, :gd].astype(a_ref.dtype)
            b_ref[p, :, g * gd:(g + 1) * gd] = ab[:, gd:].astype(b_ref.dtype)


def _fnet_channel(proj, w_cs, B, S, tm=1024):
    W = FNET_GROUPS * FNET_GROUP_DIM
    half = S // 2
    tm = min(tm, half)
    nh = half // tm
    out = jax.ShapeDtypeStruct((2, B * half, W), BF16)
    return pl.pallas_call(
        _fnet_ch_kernel,
        out_shape=(out, out),
        grid=(B, nh),
        in_specs=[
            pl.BlockSpec((tm, W), lambda b, i: (b * 2 * nh + i, COL_UF // W)),
            pl.BlockSpec((tm, W), lambda b, i: (b * 2 * nh + nh + i, COL_UF // W)),
            pl.BlockSpec((FNET_GROUP_DIM, 2 * FNET_GROUP_DIM), lambda b, i: (0, 0)),
        ],
        out_specs=(pl.BlockSpec((2, tm, W), lambda b, i: (0, b * nh + i, 0)),
                   pl.BlockSpec((2, tm, W), lambda b, i: (0, b * nh + i, 0))),
        compiler_params=_cparams("parallel", "parallel"),
        name="fnet_channel",
    )(proj, proj, w_cs)


def _fnet_pos_kernel(c_ref, s_ref, a_ref, b_ref, o_ref, *, scale):
    y = (jnp.dot(c_ref[0], a_ref[0], preferred_element_type=F32)
         + jnp.dot(s_ref[0], b_ref[0], preferred_element_type=F32))
    o_ref[...] = (y * scale).astype(o_ref.dtype)


def _fnet_position(c_tab, sn_tab, a, b, B, S, tm=1024, tn=512):
    W = FNET_GROUPS * FNET_GROUP_DIM
    half = S // 2
    tm = min(tm, half)
    nm = half // tm
    nn = W // tn
    scale = float((S * FNET_GROUP_DIM) ** -0.5)
    return pl.pallas_call(
        functools.partial(_fnet_pos_kernel, scale=scale),
        out_shape=jax.ShapeDtypeStruct((B * half, 2 * W), BF16),
        grid=(2, nm, B, nn),
        in_specs=[
            pl.BlockSpec((1, tm, half), lambda p, m, bb, n: (p, m, 0)),
            pl.BlockSpec((1, tm, half), lambda p, m, bb, n: (p, m, 0)),
            pl.BlockSpec((1, half, tn), lambda p, m, bb, n: (p, bb, n)),
            pl.BlockSpec((1, half, tn), lambda p, m, bb, n: (p, bb, n)),
        ],
        out_specs=pl.BlockSpec((tm, tn), lambda p, m, bb, n: (bb * nm + m, p * nn + n)),
        compiler_params=_cparams("parallel", "parallel", "parallel", "parallel"),
        name="fnet_position",
    )(c_tab, sn_tab, a, b)


def _dft_tables(S):
    gd = FNET_GROUP_DIM
    ck = (np.arange(gd)[:, None] * np.arange(gd)[None, :]) % gd
    ang = 2.0 * np.pi * ck / gd
    w_cs = jnp.asarray(np.concatenate([np.cos(ang), np.sin(ang)], axis=1), F32).astype(BF16)
    kb = 64
    half = S // 2
    n = lax.broadcasted_iota(jnp.int32, (1, half), 1)

    def thin(rows, period):
        ang_ = ((lax.broadcasted_iota(jnp.int32, (rows, 1), 0) * n) % period).astype(F32) * (2.0 * np.pi / period)
        return jnp.cos(ang_), jnp.sin(ang_)

    c_hi, s_hi = thin(S // kb, S // kb)
    c_lo, s_lo = thin(kb, S)
    c_hi, s_hi = c_hi[None, :, None, :], s_hi[None, :, None, :]
    split = lambda t: jnp.stack([t[0::2], t[1::2]])[:, None]
    c_lo, s_lo = split(c_lo), split(s_lo)
    cos_t = (c_hi * c_lo - s_hi * s_lo).reshape(2, half, half)
    nsin_t = -(s_hi * c_lo + c_hi * s_lo).reshape(2, half, half)
    return w_cs, cos_t.astype(BF16), nsin_t.astype(BF16)


def _merge_kernel(yn_ref, ym_ref, yf_ref, w_ref, g0_ref, g1_ref, g2_ref, o_ref, wbf_ref):
    @pl.when(pl.program_id(1) == 0)
    def _():
        wbf_ref[...] = w_ref[0].astype(BF16)

    acc = g0_ref[...].astype(F32) * jnp.dot(yn_ref[...], wbf_ref[0], preferred_element_type=F32)
    acc += g1_ref[...].astype(F32) * jnp.dot(ym_ref[...], wbf_ref[1], preferred_element_type=F32)
    acc += g2_ref[...].astype(F32) * jnp.dot(yf_ref[...], wbf_ref[2], preferred_element_type=F32)
    o_ref[...] = acc.astype(o_ref.dtype)


def _merge(y_na, y_mla, y_f, w_branch, layer, proj, tm=1024, tn=512):
    T = y_na.shape[0]
    ybs = pl.BlockSpec((tm, BRANCH_WIDTH), lambda j, i: (i, 0))

    def gate_spec(br):
        off = (COL_GATE + br * D_MODEL) // tn
        return pl.BlockSpec((tm, tn), lambda j, i: (i, off + j))

    return pl.pallas_call(
        _merge_kernel,
        out_shape=jax.ShapeDtypeStruct((T, D_MODEL), BF16),
        grid=(D_MODEL // tn, T // tm),
        in_specs=[ybs, ybs, ybs,
                  pl.BlockSpec((1, N_BRANCHES, BRANCH_WIDTH, tn), lambda j, i: (layer, 0, 0, j)),
                  gate_spec(0), gate_spec(1), gate_spec(2)],
        out_specs=pl.BlockSpec((tm, tn), lambda j, i: (i, j)),
        scratch_shapes=[pltpu.VMEM((N_BRANCHES, BRANCH_WIDTH, tn), BF16)],
        compiler_params=_cparams("parallel", "arbitrary"),
        name="merge",
    )(y_na, y_mla, y_f, w_branch, proj, proj, proj)


def _outproj_kernel(m_ref, w_ref, h_ref, o_ref, wbf_ref):
    @pl.when(pl.program_id(1) == 0)
    def _():
        wbf_ref[...] = w_ref[0].astype(BF16)

    o_ref[...] = h_ref[...] + jnp.dot(m_ref[...], wbf_ref[...], preferred_element_type=F32)


def _outproj(merged, w_o, layer, h2, tm=1024, tn=512):
    T = merged.shape[0]
    return pl.pallas_call(
        _outproj_kernel,
        out_shape=jax.ShapeDtypeStruct((T, D_MODEL), F32),
        grid=(D_MODEL // tn, T // tm),
        in_specs=[
            pl.BlockSpec((tm, D_MODEL), lambda j, i: (i, 0)),
            pl.BlockSpec((1, D_MODEL, tn), lambda j, i: (layer, 0, j)),
            pl.BlockSpec((tm, tn), lambda j, i: (i, j)),
        ],
        out_specs=pl.BlockSpec((tm, tn), lambda j, i: (i, j)),
        scratch_shapes=[pltpu.VMEM((D_MODEL, tn), BF16)],
        compiler_params=_cparams("parallel", "arbitrary"),
        name="outproj",
    )(merged, w_o, h2)


def _router_kernel(x_ref, g_ref, wrt_ref, aff_ref):
    x = x_ref[...]
    ms = jnp.mean(x * x, axis=-1, keepdims=True)
    xn = x * lax.rsqrt(ms + RMS_EPS) * g_ref[...]
    logits = lax.dot_general(wrt_ref[...], xn, (((1,), (1,)), ((), ())), preferred_element_type=F32,
                             precision=lax.Precision.HIGHEST)
    m = jnp.max(logits, axis=0, keepdims=True)
    e = jnp.exp(logits - m)
    aff_ref[0] = e / jnp.sum(e, axis=0, keepdims=True)


def _router(h2, gain, w_router_t, B, S, tm=1024):
    nt = S // tm
    return pl.pallas_call(
        _router_kernel,
        out_shape=jax.ShapeDtypeStruct((B, N_EXPERTS, S), F32),
        grid=(B, nt),
        in_specs=[
            pl.BlockSpec((tm, D_MODEL), lambda b, i: (b * nt + i, 0)),
            pl.BlockSpec((1, D_MODEL), lambda b, i: (0, 0)),
            pl.BlockSpec((N_EXPERTS, D_MODEL), lambda b, i: (0, 0)),
        ],
        out_specs=pl.BlockSpec((1, N_EXPERTS, tm), lambda b, i: (b, 0, i)),
        compiler_params=_cparams("parallel", "parallel"),
        name="router",
    )(h2, gain, w_router_t)


def _route_kernel(a_ref, val_ref, gate_ref, starts_ref, kmax_ref, pos_s, pv_s, a_s, *, cap, tt):
    E, S = a_ref.shape[1], a_ref.shape[2]
    a = a_ref[0]

    def as_float(b):
        return lax.bitcast_convert_type(b, F32)

    def bisect(_, carry):
        lo, hi = carry
        mid = lo + ((hi - lo + 1) >> 1)
        cnt = jnp.sum(jnp.where(a >= as_float(mid), 1.0, 0.0), axis=1, keepdims=True)
        ge = cnt >= cap
        return jnp.where(ge, mid, lo), jnp.where(ge, hi, mid - 1)

    lo0 = jnp.zeros((E, 1), I32)
    hi0 = jnp.full((E, 1), 0x7F7FFFFF, I32)
    thr_bits, _ = lax.fori_loop(0, 31, bisect, (lo0, hi0))
    thr, thr_up = as_float(thr_bits), as_float(thr_bits + 1)

    r_i = lax.broadcasted_iota(I32, (PFX_BLK, PFX_BLK), 0)
    c_i = lax.broadcasted_iota(I32, (PFX_BLK, PFX_BLK), 1)
    tri = jnp.where(r_i < c_i, 1.0, 0.0).astype(BF16)

    def excl_prefix(mask):
        x = jnp.where(mask, 1.0, 0.0).astype(BF16)
        carry = jnp.zeros((E, 1), F32)
        outs = []
        for j in range(S // PFX_BLK):
            blk = x[:, j * PFX_BLK:(j + 1) * PFX_BLK]
            outs.append(jnp.dot(blk, tri, preferred_element_type=F32) + carry)
            carry = carry + jnp.sum(blk.astype(F32), axis=1, keepdims=True)
        return jnp.concatenate(outs, axis=1)

    gt = a >= thr_up
    eq = (a >= thr) & (a < thr_up)
    need = cap - jnp.sum(jnp.where(gt, 1.0, 0.0), axis=1, keepdims=True)
    sel = gt | (eq & (excl_prefix(eq) < need))
    pos = excl_prefix(sel)

    self_ = jnp.where(sel, 1.0, 0.0)
    e_r = lax.broadcasted_iota(I32, (E, E), 0)
    e_c = lax.broadcasted_iota(I32, (E, E), 1)
    low = jnp.where(e_c < e_r, 1.0, 0.0).astype(BF16)
    rank = jnp.dot(low, self_.astype(BF16), preferred_element_type=F32)
    tok = lax.broadcasted_iota(I32, (E, S), 1)
    pos_s[...] = jnp.where(sel, pos.astype(I32), -1)
    pv_s[...] = (tok + (rank.astype(I32) << TOK_BITS)).astype(F32)
    a_s[...] = a

    lane = lax.broadcasted_iota(I32, (E, 128), 1)
    starts = jnp.full((E, 128), cap, I32)
    kcount = jnp.sum(self_, axis=0, keepdims=True)
    kmax = jnp.zeros((8, 128), I32)
    lane8 = lax.broadcasted_iota(I32, (8, 128), 1)
    posi = pos.astype(I32)
    for j in range(S // tt):
        starts = jnp.where(lane == j, posi[:, j * tt:j * tt + 1], starts)
        kj = jnp.max(kcount[:, j * tt:(j + 1) * tt], axis=1, keepdims=True).astype(I32)
        kmax = jnp.where(lane8 == j, kj, kmax)
    starts_ref[0] = starts
    kmax_ref[0] = kmax

    c_iota = lax.broadcasted_iota(I32, (cap, PFX_BLK), 0)
    lane_c = lax.broadcasted_iota(I32, (cap, 128), 1)

    def per_expert(e, carry):
        val_acc, gate_acc = carry
        v_part = jnp.zeros((cap, PFX_BLK), F32)
        g_part = jnp.zeros((cap, PFX_BLK), F32)
        for j in range(S // PFX_BLK):
            sl = pl.ds(j * PFX_BLK, PFX_BLK)
            hit = pos_s[pl.ds(e, 1), sl] == c_iota
            v_part = v_part + jnp.where(hit, pv_s[pl.ds(e, 1), sl], 0.0)
            g_part = g_part + jnp.where(hit, a_s[pl.ds(e, 1), sl], 0.0)
        v = jnp.sum(v_part, axis=1, keepdims=True).astype(I32)
        g = jnp.sum(g_part, axis=1, keepdims=True)
        return jnp.where(lane_c == e, v, val_acc), jnp.where(lane_c == e, g, gate_acc)

    val, gate = lax.fori_loop(0, E, per_expert, (jnp.zeros((cap, 128), I32), jnp.zeros((cap, 128), F32)))
    val_ref[0] = val
    gate_ref[0] = gate


def _route(aff_t, cap, tt=COMB_TT):
    B, E, S = aff_t.shape
    assert S <= (1 << TOK_BITS) and S // tt < 128 and S % PFX_BLK == 0
    return pl.pallas_call(
        functools.partial(_route_kernel, cap=cap, tt=tt),
        out_shape=(
            jax.ShapeDtypeStruct((B, cap, 128), I32),
            jax.ShapeDtypeStruct((B, cap, 128), F32),
            jax.ShapeDtypeStruct((B, E, 128), I32),
            jax.ShapeDtypeStruct((B, 8, 128), I32),
        ),
        grid=(B,),
        in_specs=[pl.BlockSpec((1, E, S), lambda b: (b, 0, 0))],
        out_specs=(
            pl.BlockSpec((1, cap, 128), lambda b: (b, 0, 0)),
            pl.BlockSpec((1, cap, 128), lambda b: (b, 0, 0)),
            pl.BlockSpec((1, E, 128), lambda b: (b, 0, 0)),
            pl.BlockSpec((1, 8, 128), lambda b: (b, 0, 0)),
        ),
        scratch_shapes=[pltpu.VMEM((E, S), I32), pltpu.VMEM((E, S), F32), pltpu.VMEM((E, S), F32)],
        compiler_params=_cparams("parallel"),
        name="route",
    )(aff_t)


def _gather_kernel(rows_ref, h_hbm, g_ref, o_ref, buf, sem, *, gt):
    i = pl.program_id(0)
    slot = i % 2

    def issue_tile(t, s):
        def issue(r, _):
            pltpu.make_async_copy(h_hbm.at[pl.ds(rows_ref[t * gt + r], 1)], buf.at[s, pl.ds(r, 1)], sem.at[s]).start()
            return 0

        lax.fori_loop(0, gt, issue, 0, unroll=8)

    @pl.when(i == 0)
    def _():
        issue_tile(0, 0)

    @pl.when(i + 1 < pl.num_programs(0))
    def _():
        issue_tile(i + 1, 1 - slot)

    pltpu.make_async_copy(h_hbm.at[pl.ds(0, gt)], buf.at[slot], sem.at[slot]).wait()
    x = buf[slot]
    ms = jnp.mean(x * x, axis=-1, keepdims=True)
    o_ref[...] = (x * lax.rsqrt(ms + RMS_EPS) * g_ref[...]).astype(o_ref.dtype)


def _gather_norm(rows, h2, gain, gt=1024):
    n = rows.shape[0]
    gt = min(gt, n)
    return pl.pallas_call(
        functools.partial(_gather_kernel, gt=gt),
        out_shape=jax.ShapeDtypeStruct((n, D_MODEL), BF16),
        grid_spec=pltpu.PrefetchScalarGridSpec(
            num_scalar_prefetch=1,
            grid=(n // gt,),
            in_specs=[pl.BlockSpec(memory_space=pl.ANY), pl.BlockSpec((1, D_MODEL), lambda i, rows: (0, 0))],
            out_specs=pl.BlockSpec((gt, D_MODEL), lambda i, rows: (i, 0)),
            scratch_shapes=[pltpu.VMEM((2, gt, D_MODEL), F32), pltpu.SemaphoreType.DMA((2,))],
        ),
        compiler_params=_cparams("arbitrary", disable_bounds_checks=True),
        name="gather_norm",
    )(rows, h2, gain)


def _combine_kernel(dst_ref, starts_ref, kmax_ref, h_ref, ye_hbm, fg_ref, o_ref, planes, sem, *,
                    B, E, cap, tt, nt, final_norm):
    tile = pl.program_id(0)
    slot = tile % 2

    def tile_ranges(tl, fn):
        b, j = tl // nt, tl % nt

        def per_expert(e, n):
            sbase = (b * E + e) * (nt + 1) + j
            c0 = starts_ref[sbase]
            c1 = starts_ref[sbase + 1]
            fn((e * B + b) * cap, c0, c1)
            return n + (c1 - c0)

        return lax.fori_loop(0, E, per_expert, 0)

    def kmax_of(tl):
        return kmax_ref[(tl // nt) * (nt + 1) + tl % nt]

    def stage(tl, s):
        def zero(p, _):
            planes[s, pl.ds(pl.multiple_of(p * tt, tt), tt), :] = jnp.zeros((tt, D_MODEL), F32)
            return 0

        lax.fori_loop(0, kmax_of(tl), zero, 0)

        def issue_range(lbase, c0, c1):
            def issue(c):
                pltpu.make_async_copy(ye_hbm.at[pl.ds(lbase + c, 1)],
                                      planes.at[s, pl.ds(dst_ref[lbase + c], 1)], sem.at[s]).start()

            def issue4(i, _):
                for u in range(4):
                    issue(c0 + 4 * i + u)
                return 0

            def issue1(i, _):
                issue(c1 - 1 - i)
                return 0

            lax.fori_loop(0, (c1 - c0) >> 2, issue4, 0)
            lax.fori_loop(0, (c1 - c0) & 3, issue1, 0)

        tile_ranges(tl, issue_range)

    @pl.when(tile == 0)
    def _():
        stage(0, 0)

    @pl.when(tile + 1 < pl.num_programs(0))
    def _():
        stage(tile + 1, 1 - slot)

    n = tile_ranges(tile, lambda lbase, c0, c1: None)
    for bit in range((E * tt).bit_length()):
        @pl.when(((n >> bit) & 1) == 1)
        def _():
            rows = 1 << bit
            pltpu.make_async_copy(ye_hbm.at[pl.ds(0, rows)], planes.at[slot, pl.ds(0, rows)], sem.at[slot]).wait()

    k = kmax_of(tile)
    for cs in range(D_MODEL // 128):
        cols = slice(cs * 128, (cs + 1) * 128)

        def add(p, acc):
            return acc + planes[slot, pl.ds(pl.multiple_of(p * tt, tt), tt), cols]

        o_ref[:, cols] = lax.fori_loop(0, k, add, h_ref[:, cols])

    if final_norm:
        x = o_ref[...]
        ms = jnp.mean(x * x, axis=-1, keepdims=True)
        o_ref[...] = x * lax.rsqrt(ms + RMS_EPS) * fg_ref[...]


def _combine(h2, ye, dst_rows, starts, kmax, final_gain, B, S, cap, tt=COMB_TT):
    E = N_EXPERTS
    nt = S // tt
    assert starts.shape[0] == B * E * (nt + 1) and kmax.shape[0] == B * (nt + 1)
    final_norm = final_gain is not None
    fg = final_gain if final_norm else jnp.ones((1, D_MODEL), F32)
    return pl.pallas_call(
        functools.partial(_combine_kernel, B=B, E=E, cap=cap, tt=tt, nt=nt, final_norm=final_norm),
        out_shape=jax.ShapeDtypeStruct(h2.shape, F32),
        grid_spec=pltpu.PrefetchScalarGridSpec(
            num_scalar_prefetch=3,
            grid=(B * nt,),
            in_specs=[pl.BlockSpec((tt, D_MODEL), lambda t, *_: (t, 0)),
                      pl.BlockSpec(memory_space=pl.ANY),
                      pl.BlockSpec((1, D_MODEL), lambda t, *_: (0, 0))],
            out_specs=pl.BlockSpec((tt, D_MODEL), lambda t, *_: (t, 0)),
            scratch_shapes=[pltpu.VMEM((2, E * tt, D_MODEL), F32), pltpu.SemaphoreType.DMA((2,))],
        ),
        compiler_params=_cparams("arbitrary", disable_bounds_checks=True),
        name="combine",
    )(dst_rows, starts, kmax, h2, ye, fg)


def _ffn_up_kernel(x_ref, wg_ref, wu_ref, o_ref):
    x = x_ref[...]
    a = jnp.dot(x, wg_ref[0, 0].astype(BF16), preferred_element_type=F32)
    u = jnp.dot(x, wu_ref[0, 0].astype(BF16), preferred_element_type=F32)
    o_ref[...] = (a * (1.0 / (1.0 + jnp.exp(-a))) * u).astype(o_ref.dtype)


def _ffn_down_kernel(hid_ref, wd_ref, gate_ref, o_ref):
    y = jnp.dot(hid_ref[...], wd_ref[0, 0].astype(BF16), preferred_element_type=F32)
    o_ref[...] = y * gate_ref[...]


def _experts(xe, wg, wu, wd, layer, gate, tf=256, tn=512):
    n_tok = xe.shape[0]
    per_e = n_tok // N_EXPERTS
    ff = wg.shape[-1]
    hid = pl.pallas_call(
        _ffn_up_kernel,
        out_shape=jax.ShapeDtypeStruct((n_tok, ff), BF16),
        grid=(N_EXPERTS, ff // tf),
        in_specs=[
            pl.BlockSpec((per_e, D_MODEL), lambda e, f: (e, 0)),
            pl.BlockSpec((1, 1, D_MODEL, tf), lambda e, f: (layer, e, 0, f)),
            pl.BlockSpec((1, 1, D_MODEL, tf), lambda e, f: (layer, e, 0, f)),
        ],
        out_specs=pl.BlockSpec((per_e, tf), lambda e, f: (e, f)),
        compiler_params=_cparams("parallel", "arbitrary"),
        name="ffn_up",
    )(xe, wg, wu)
    return pl.pallas_call(
        _ffn_down_kernel,
        out_shape=jax.ShapeDtypeStruct((n_tok, D_MODEL), F32),
        grid=(N_EXPERTS, D_MODEL // tn),
        in_specs=[
            pl.BlockSpec((per_e, ff), lambda e, n: (e, 0)),
            pl.BlockSpec((1, 1, ff, tn), lambda e, n: (layer, e, 0, n)),
            pl.BlockSpec((per_e, 1), lambda e, n: (e, 0)),
        ],
        out_specs=pl.BlockSpec((per_e, tn), lambda e, n: (e, n)),
        compiler_params=_cparams("parallel", "arbitrary"),
        name="ffn_down",
    )(hid, wd, gate)


def _moe_layer(h2, gain, w_router, wg, wu, wd, layer, final_gain, B, S):
    E = N_EXPERTS
    cap = CAPACITY_FACTOR * S // E
    nt = S // COMB_TT
    aff_t = _router(h2, gain, jnp.transpose(w_router), B, S)
    val, gate, starts, kmax = _route(aff_t, cap)
    val_e = jnp.transpose(val[:, :, :E], (2, 0, 1))
    gate_e = jnp.transpose(gate[:, :, :E], (2, 0, 1)).reshape(-1, 1)
    tok_e = val_e & ((1 << TOK_BITS) - 1)
    rows = (tok_e + (jnp.arange(B, dtype=I32) * S)[None, :, None]).reshape(-1)
    xe = _gather_norm(rows, h2, gain)
    ye = _experts(xe, wg, wu, wd, layer, gate_e)
    dst = ((val_e >> TOK_BITS) * COMB_TT + tok_e % COMB_TT).reshape(-1)
    starts_flat = starts[:, :, :nt + 1].reshape(-1)
    kmax_flat = kmax[:, 0, :nt + 1].reshape(-1)
    return _combine(h2, ye, dst, starts_flat, kmax_flat, final_gain, B, S, cap)


def _pad_cols(w, n):
    return jnp.pad(w, ((0, 0), (0, n - w.shape[1])))


def _rot_cols(w):
    half = w.shape[1] // 2
    return jnp.concatenate([-w[:, half:], w[:, :half]], axis=1)


def _prep_in_weights(w_in, b_gate):
    o = np.cumsum([0, 3 * NA_WIDTH, MLA_Q_LORA, MLA_KV_LORA, MLA_ROPE, FNET_GROUPS * FNET_GROUP_DIM])
    qkv, cq, ckv, kr, uf, gl = (w_in[:, o[0]:o[1]], w_in[:, o[1]:o[2]], w_in[:, o[2]:o[3]], w_in[:, o[3]:o[4]],
                                w_in[:, o[4]:o[5]], w_in[:, o[5]:])
    w_all = jnp.concatenate([qkv, uf, _pad_cols(cq, CQ_PAD), _pad_cols(ckv, CKV_PAD), _pad_cols(kr, 128),
                             _pad_cols(_rot_cols(kr), 128), gl], axis=1).astype(BF16)
    b_all = jnp.concatenate([jnp.zeros((COL_GATE,), F32), b_gate]).reshape(1, PROJ_COLS)
    return w_all, b_all


def _prep_mla_weights(w_uq, q_norm, w_ukv, kv_norm):
    qd = MLA_NOPE + MLA_ROPE
    wq3 = w_uq.reshape(MLA_Q_LORA, MLA_HEADS, qd)
    nope, pe = wq3[:, :, :MLA_NOPE], wq3[:, :, MLA_NOPE:]
    z64 = jnp.zeros((MLA_Q_LORA, MLA_HEADS, 64), F32)
    wq = jnp.concatenate([nope, pe, z64], axis=2).reshape(MLA_Q_LORA, MLA_HEADS * 256)
    rot = jnp.concatenate([-pe[:, :, 32:], pe[:, :, :32], z64], axis=2).reshape(MLA_Q_LORA, MLA_HEADS * 128)
    rpad = ((0, CQ_PAD - MLA_Q_LORA), (0, 0))
    wq = jnp.transpose(jnp.pad(wq, rpad)).astype(BF16)
    rot = jnp.transpose(jnp.pad(rot, rpad)).astype(BF16)
    wkv3 = jnp.pad(w_ukv, ((0, CKV_PAD - MLA_KV_LORA), (0, 0))).reshape(CKV_PAD, MLA_HEADS, MLA_NOPE + MLA_V)
    wk = wkv3[:, :, :MLA_NOPE].reshape(CKV_PAD, MLA_HEADS * MLA_NOPE).astype(BF16)
    wvt = jnp.transpose(wkv3[:, :, MLA_NOPE:], (1, 2, 0)).reshape(MLA_HEADS * MLA_V, CKV_PAD).astype(BF16)
    qn = jnp.pad(q_norm, (0, CQ_PAD - MLA_Q_LORA)).reshape(1, CQ_PAD)
    kvn = jnp.pad(kv_norm, (0, CKV_PAD - MLA_KV_LORA)).reshape(1, CKV_PAD)
    return wq, rot, wk, wvt, qn, kvn


def _rope_tables128(S):
    pos = jnp.arange(S, dtype=F32)
    inv = 1.0 / (ROPE_THETA ** (jnp.arange(0, MLA_ROPE, 2, dtype=F32) / MLA_ROPE))
    ang = pos[:, None] * inv[None, :]
    z = jnp.zeros((S, 64), F32)
    cos, sin = jnp.cos(ang), jnp.sin(ang)
    return jnp.concatenate([cos, cos, z], axis=1), jnp.concatenate([sin, sin, z], axis=1)


def kernel(x, w_in, b_gate, w_uq, q_norm, w_ukv, kv_norm, na_rpb, w_branch, w_o, norm_mix, norm_moe,
           w_router, w_exp_gate, w_exp_up, w_exp_down, norm_final):
    B, S, D = x.shape
    T = B * S
    depth = w_in.shape[0]
    cos128, sin128 = _rope_tables128(S)
    cos_t, sin_t = jnp.transpose(cos128), jnp.transpose(sin128)
    w_cs, c_tab, sn_tab = _dft_tables(S)
    h = x.reshape(T, D)
    for l in range(depth):
        w_all, b_all = _prep_in_weights(w_in[l], b_gate[l])
        proj = _inproj(h, norm_mix[l].reshape(1, D), w_all, b_all)
        y_na = _na_attention(proj, _na_bias_tables(na_rpb[l]), B, S)
        wq, wrot, wk, wvt, qn, kvn = _prep_mla_weights(w_uq[l], q_norm[l], w_ukv[l], kv_norm[l])
        q, k, vt = _mla_up(proj, cos128, sin128, cos_t, sin_t, qn, kvn, wq, wrot, wk, wvt, B, S)
        y_mla = _mla_attn(q, k, vt, B, S)
        fa, fb = _fnet_channel(proj, w_cs, B, S)
        y_f = _fnet_position(c_tab, sn_tab, fa, fb, B, S).reshape(T, FNET_GROUPS * FNET_GROUP_DIM)
        merged = _merge(y_na, y_mla, y_f, w_branch, l, proj)
        h = _outproj(merged, w_o, l, h)
        final_gain = norm_final.reshape(1, D) if l == depth - 1 else None
        h = _moe_layer(h, norm_moe[l].reshape(1, D), w_router[l], w_exp_gate, w_exp_up, w_exp_down, l, final_gain,
                       B, S)
    return h.reshape(B, S, D)
```

```python
import functools

import numpy as np
import jax
import jax.numpy as jnp
from jax import lax
from jax.experimental import pallas as pl
from jax.experimental.pallas import tpu as pltpu

D_MODEL = 2048
GRID_W = 64
NA_HEADS = 16
NA_HEAD_DIM = 64
NA_WIDTH = NA_HEADS * NA_HEAD_DIM
NA_KH = 8
NA_KW = 16
MLA_HEADS = 8
MLA_NOPE = 128
MLA_ROPE = 64
MLA_V = 128
MLA_Q_LORA = 448
MLA_KV_LORA = 160
ROPE_THETA = 10000.0
FNET_GROUPS = 4
FNET_GROUP_DIM = 256
N_BRANCHES = 3
BRANCH_WIDTH = 1024
N_EXPERTS = 16
EXPERT_FF = 2048
CAPACITY_FACTOR = 2
RMS_EPS = 1e-6
NEG_INF = -1e30

F32 = jnp.float32
BF16 = jnp.bfloat16
I32 = jnp.int32

COL_QKV = 0
COL_UF = 3072
COL_LAT = 4096
COL_GATE = 5120
PROJ_COLS = COL_GATE + N_BRANCHES * D_MODEL
LAT_W = 1024
CQ_PAD = 512
CKV_PAD = 256
VT_ROWS = MLA_V + 16
LOG2E = 1.4426950408889634
TOK_BITS = 12
PFX_BLK = 512
COMB_TT = 256

VMEM_LIMIT = 56 * 1024 * 1024


def _cparams(*sem, **kw):
    return pltpu.CompilerParams(dimension_semantics=sem, vmem_limit_bytes=VMEM_LIMIT, **kw)


def _inproj_kernel(x_ref, g_ref, w_ref, b_ref, o_ref, xn_ref, *, gate_tile0):
    j = pl.program_id(1)

    @pl.when(j == 0)
    def _():
        x = x_ref[...]
        ms = jnp.mean(x * x, axis=-1, keepdims=True)
        xn_ref[...] = (x * lax.rsqrt(ms + RMS_EPS) * g_ref[...]).astype(BF16)

    acc = jnp.dot(xn_ref[...], w_ref[...], preferred_element_type=F32)

    @pl.when(j < gate_tile0)
    def _():
        o_ref[...] = acc.astype(o_ref.dtype)

    @pl.when(j >= gate_tile0)
    def _():
        z = acc + b_ref[...]
        o_ref[...] = (0.5 * jnp.tanh(0.5 * z) + 0.5).astype(o_ref.dtype)


def _inproj(h2, gain, w_all, b_all, tm=1024, tn=1024):
    T = h2.shape[0]
    return pl.pallas_call(
        functools.partial(_inproj_kernel, gate_tile0=COL_GATE // tn),
        out_shape=jax.ShapeDtypeStruct((T, PROJ_COLS), BF16),
        grid=(T // tm, PROJ_COLS // tn),
        in_specs=[
            pl.BlockSpec((tm, D_MODEL), lambda i, j: (i, 0)),
            pl.BlockSpec((1, D_MODEL), lambda i, j: (0, 0)),
            pl.BlockSpec((D_MODEL, tn), lambda i, j: (0, j)),
            pl.BlockSpec((1, tn), lambda i, j: (0, j)),
        ],
        out_specs=pl.BlockSpec((tm, tn), lambda i, j: (i, j)),
        scratch_shapes=[pltpu.VMEM((tm, D_MODEL), BF16)],
        compiler_params=_cparams("parallel", "arbitrary"),
        name="inproj",
    )(h2, gain, w_all, b_all)


def _na_kernel(q_ref, k_ref, v_ref, bias_a_ref, bias_b_ref, o_ref, *, rows):
    i = pl.program_id(1)
    nk = NA_KH * GRID_W
    lo = lax.broadcasted_iota(jnp.int32, (GRID_W, 128), 1) < NA_HEAD_DIM
    nt = (((1,), (1,)), ((), ()))
    row0 = []
    for rr in range(2):
        start = jnp.clip(2 * i + rr - NA_KH // 2, 0, rows - NA_KH)
        row0.append(pl.multiple_of(start * GRID_W, GRID_W))

    def scores(rr, hp):
        sl = slice(hp * 128, (hp + 1) * 128)
        q2 = q_ref[rr * GRID_W:(rr + 1) * GRID_W, sl]
        zero = jnp.zeros_like(q2)
        qq = jnp.concatenate([jnp.where(lo, q2, zero), jnp.where(lo, zero, q2)], axis=0)
        return lax.dot_general(qq, k_ref[pl.ds(row0[rr], nk), sl], nt, preferred_element_type=F32)

    chains = [(rr, hp) for hp in range(NA_HEADS // 2) for rr in range(2)]
    s_next = scores(*chains[0])
    for n, (rr, hp) in enumerate(chains):
        s = s_next
        if n + 1 < len(chains):
            s_next = scores(*chains[n + 1])
        sl = slice(hp * 128, (hp + 1) * 128)
        bias_ref = bias_a_ref if rr == 0 else bias_b_ref
        s = s * (NA_HEAD_DIM ** -0.5 * LOG2E) + bias_ref[0, hp]
        m = jnp.max(s, axis=-1, keepdims=True)
        p = jnp.exp2(s - m)
        l = jnp.sum(p, axis=-1, keepdims=True)
        o = jnp.dot(p.astype(BF16), v_ref[pl.ds(row0[rr], nk), sl], preferred_element_type=F32) / l
        o_ref[rr * GRID_W:(rr + 1) * GRID_W, sl] = jnp.where(lo, o[:GRID_W], o[GRID_W:]).astype(o_ref.dtype)


def _na_bias_kernel(toep_ref, o_ref):
    t = pl.program_id(0)
    hp = pl.program_id(1)
    for a in range(2):
        for i in range(NA_KH):
            o_ref[0, 0, a * GRID_W:(a + 1) * GRID_W, i * GRID_W:(i + 1) * GRID_W] = (
                toep_ref[2 * hp + a, i - t + NA_KH - 1])


def _na_bias_tables(rpb):
    cols = np.arange(GRID_W)
    col_start = np.clip(cols - NA_KW // 2, 0, GRID_W - NA_KW)
    col_valid = (cols[None, :] >= col_start[:, None]) & (cols[None, :] < col_start[:, None] + NA_KW)
    col_idx = np.clip(cols[None, :] - cols[:, None] + NA_KW - 1, 0, 2 * NA_KW - 2)
    onehot = (col_idx[None] == np.arange(2 * NA_KW - 1)[:, None, None]).astype(np.float32)
    toep = jnp.einsum("hdc,cqj->hdqj", rpb.astype(F32), jnp.asarray(onehot), precision=lax.Precision.HIGHEST)
    toep = jnp.where(col_valid[None, None], toep * LOG2E, NEG_INF * LOG2E)
    return pl.pallas_call(
        _na_bias_kernel,
        out_shape=jax.ShapeDtypeStruct((NA_KH, NA_HEADS // 2, 2 * GRID_W, NA_KH * GRID_W), F32),
        grid=(NA_KH, NA_HEADS // 2),
        in_specs=[pl.BlockSpec((NA_HEADS, 2 * NA_KH - 1, GRID_W, GRID_W), lambda t, hp: (0, 0, 0, 0))],
        out_specs=pl.BlockSpec((1, 1, 2 * GRID_W, NA_KH * GRID_W), lambda t, hp: (t, hp, 0, 0)),
        compiler_params=_cparams("parallel", "parallel"),
        name="na_bias",
    )(toep)


def _na_attention(proj, bias, B, S):
    rows = S // GRID_W
    assert rows >= NA_KH and rows % 2 == 0
    T = B * S
    half = rows // 2

    def bias_spec(rr):
        def bias_map(b, i):
            r = 2 * i + rr
            return (r - jnp.clip(r - NA_KH // 2, 0, rows - NA_KH), 0, 0, 0)
        return pl.BlockSpec((1, NA_HEADS // 2, 2 * GRID_W, NA_KH * GRID_W), bias_map)

    return pl.pallas_call(
        functools.partial(_na_kernel, rows=rows),
        out_shape=jax.ShapeDtypeStruct((T, NA_WIDTH), BF16),
        grid=(B, half),
        in_specs=[
            pl.BlockSpec((2 * GRID_W, NA_WIDTH), lambda b, i: (b * half + i, 0)),
            pl.BlockSpec((S, NA_WIDTH), lambda b, i: (b, 1)),
            pl.BlockSpec((S, NA_WIDTH), lambda b, i: (b, 2)),
            bias_spec(0), bias_spec(1),
        ],
        out_specs=pl.BlockSpec((2 * GRID_W, NA_WIDTH), lambda b, i: (b * half + i, 0)),
        compiler_params=_cparams("parallel", "arbitrary"),
        name="na_attn",
    )(proj, proj, proj, bias, bias)


def _mla_up_kernel(lat_ref, cos_ref, sin_ref, cost_ref, sint_ref, qn_ref, kvn_ref, wq_ref, wrot_ref, wk_ref,
                   wvt_ref, q_ref, k_ref, vt_ref):
    cq = lat_ref[:, 0:CQ_PAD].astype(F32)
    ms = jnp.sum(cq * cq, axis=-1, keepdims=True) * (1.0 / MLA_Q_LORA)
    xq = (cq * lax.rsqrt(ms + RMS_EPS) * qn_ref[...]).astype(BF16)
    ckv = lat_ref[:, CQ_PAD:CQ_PAD + CKV_PAD].astype(F32)
    ms2 = jnp.sum(ckv * ckv, axis=-1, keepdims=True) * (1.0 / MLA_KV_LORA)
    xkv = (ckv * lax.rsqrt(ms2 + RMS_EPS) * kvn_ref[...]).astype(BF16)
    cos = cos_ref[...]
    sin = sin_ref[...]
    kpe = (lat_ref[:, 768:896].astype(F32) * cos + lat_ref[:, 896:1024].astype(F32) * sin).astype(BF16)
    scale = (MLA_NOPE + MLA_ROPE) ** -0.5 * LOG2E
    ones = jnp.ones((VT_ROWS - MLA_V, lat_ref.shape[0]), BF16)
    nt = (((1,), (1,)), ((), ()))
    for h in range(MLA_HEADS):
        a = lax.dot_general(wq_ref[h * 256:(h + 1) * 256, :], xq, nt, preferred_element_type=F32)
        rt = lax.dot_general(wrot_ref[h * 128:(h + 1) * 128, :], xq, nt, preferred_element_type=F32)
        q_ref[0, h, 0:128, :] = (a[0:128] * scale).astype(BF16)
        q_ref[0, h, 128:256, :] = ((a[128:256] * cost_ref[...] + rt * sint_ref[...]) * scale).astype(BF16)
        kn = jnp.dot(xkv, wk_ref[:, h * 128:(h + 1) * 128], preferred_element_type=F32)
        k_ref[0, h, :, 0:128] = kn.astype(BF16)
        k_ref[0, h, :, 128:256] = kpe
        vt = lax.dot_general(wvt_ref[h * 128:(h + 1) * 128, :], xkv, nt, preferred_element_type=F32)
        vt_ref[0, h, 0:MLA_V, :] = vt.astype(BF16)
        vt_ref[0, h, MLA_V:VT_ROWS, :] = ones


def _mla_up(proj, cos128, sin128, cos_t, sin_t, qn, kvn, wq, wrot, wk, wvt, B, S, tm=512):
    nt = S // tm
    lat_blk = COL_LAT // LAT_W
    const = lambda b, i: (0, 0)
    return pl.pallas_call(
        _mla_up_kernel,
        out_shape=(
            jax.ShapeDtypeStruct((B, MLA_HEADS, 256, S), BF16),
            jax.ShapeDtypeStruct((B, MLA_HEADS, S, 256), BF16),
            jax.ShapeDtypeStruct((B, MLA_HEADS, VT_ROWS, S), BF16),
        ),
        grid=(B, nt),
        in_specs=[
            pl.BlockSpec((tm, LAT_W), lambda b, i: (b * nt + i, lat_blk)),
            pl.BlockSpec((tm, 128), lambda b, i: (i, 0)),
            pl.BlockSpec((tm, 128), lambda b, i: (i, 0)),
            pl.BlockSpec((128, tm), lambda b, i: (0, i)),
            pl.BlockSpec((128, tm), lambda b, i: (0, i)),
            pl.BlockSpec((1, CQ_PAD), const),
            pl.BlockSpec((1, CKV_PAD), const),
            pl.BlockSpec((MLA_HEADS * 256, CQ_PAD), const),
            pl.BlockSpec((MLA_HEADS * 128, CQ_PAD), const),
            pl.BlockSpec((CKV_PAD, MLA_HEADS * MLA_NOPE), const),
            pl.BlockSpec((MLA_HEADS * MLA_V, CKV_PAD), const),
        ],
        out_specs=(
            pl.BlockSpec((1, MLA_HEADS, 256, tm), lambda b, i: (b, 0, 0, i)),
            pl.BlockSpec((1, MLA_HEADS, tm, 256), lambda b, i: (b, 0, i, 0)),
            pl.BlockSpec((1, MLA_HEADS, VT_ROWS, tm), lambda b, i: (b, 0, 0, i)),
        ),
        compiler_params=_cparams("parallel", "parallel"),
        name="mla_up",
    )(proj, cos128, sin128, cos_t, sin_t, qn, kvn, wq, wrot, wk, wvt)


def _mla_attn_kernel(q_ref, k_ref, vt_ref, o_ref, *, ck):
    qt = q_ref[0, 0]
    n_chunks = k_ref.shape[2] // ck
    m = acc = None

    def scores(c):
        return jnp.dot(k_ref[0, 0, c * ck:(c + 1) * ck, :], qt, preferred_element_type=F32)

    s_next = scores(0)
    for c in range(n_chunks):
        s = s_next
        if c + 1 < n_chunks:
            s_next = scores(c + 1)
        mc = jnp.max(s, axis=0, keepdims=True)
        m_new = mc if c == 0 else jnp.maximum(m, mc)
        p = jnp.exp2(s - m_new).astype(BF16)
        pv = jnp.dot(vt_ref[0, 0, :, c * ck:(c + 1) * ck], p, preferred_element_type=F32)
        acc = pv if c == 0 else acc * jnp.exp2(m - m_new) + pv
        m = m_new
    o = acc[0:MLA_V] / acc[MLA_V:MLA_V + 1]
    o_ref[...] = o.T.astype(o_ref.dtype)


def _mla_attn(q, k, vt, B, S, tq=1024, ck=512):
    nq = S // tq
    return pl.pallas_call(
        functools.partial(_mla_attn_kernel, ck=ck),
        out_shape=jax.ShapeDtypeStruct((B * S, MLA_HEADS * MLA_V), BF16),
        grid=(B, MLA_HEADS, nq),
        in_specs=[
            pl.BlockSpec((1, 1, 256, tq), lambda b, h, i: (b, h, 0, i)),
            pl.BlockSpec((1, 1, S, 256), lambda b, h, i: (b, h, 0, 0)),
            pl.BlockSpec((1, 1, VT_ROWS, S), lambda b, h, i: (b, h, 0, 0)),
        ],
        out_specs=pl.BlockSpec((tq, MLA_V), lambda b, h, i: (b * nq + i, h)),
        compiler_params=_cparams("parallel", "parallel", "arbitrary"),
        name="mla_attn",
    )(q, k, vt)


def _fnet_ch_kernel(lo_ref, hi_ref, w_ref, a_ref, b_ref):
    gd = FNET_GROUP_DIM
    lo = lo_ref[...].astype(F32)
    hi = hi_ref[...].astype(F32)
    for p, u in enumerate(((lo + hi).astype(BF16), (lo - hi).astype(BF16))):
        for g in range(FNET_GROUPS):
            ab = jnp.dot(u[:, g * gd:(g + 1) * gd], w_ref[...], preferred_element_type=F32)
            a_ref[p, :, g * gd:(g + 1) * gd] = ab[:, :gd].astype(a_ref.dtype)
            b_ref[p, :, g * gd:(g + 1) * gd] = ab[:, gd:].astype(b_ref.dtype)


def _fnet_channel(proj, w_cs, B, S, tm=1024):
    W = FNET_GROUPS * FNET_GROUP_DIM
    half = S // 2
    tm = min(tm, half)
    nh = half // tm
    out = jax.ShapeDtypeStruct((2, B * half, W), BF16)
    return pl.pallas_call(
        _fnet_ch_kernel,
        out_shape=(out, out),
        grid=(B, nh),
        in_specs=[
            pl.BlockSpec((tm, W), lambda b, i: (b * 2 * nh + i, COL_UF // W)),
            pl.BlockSpec((tm, W), lambda b, i: (b * 2 * nh + nh + i, COL_UF // W)),
            pl.BlockSpec((FNET_GROUP_DIM, 2 * FNET_GROUP_DIM), lambda b, i: (0, 0)),
        ],
        out_specs=(pl.BlockSpec((2, tm, W), lambda b, i: (0, b * nh + i, 0)),
                   pl.BlockSpec((2, tm, W), lambda b, i: (0, b * nh + i, 0))),
        compiler_params=_cparams("parallel", "parallel"),
        name="fnet_channel",
    )(proj, proj, w_cs)


def _fnet_pos_kernel(c_ref, s_ref, a_ref, b_ref, o_ref, *, scale):
    y = (jnp.dot(c_ref[0], a_ref[0], preferred_element_type=F32)
         + jnp.dot(s_ref[0], b_ref[0], preferred_element_type=F32))
    o_ref[...] = (y * scale).astype(o_ref.dtype)


def _fnet_position(c_tab, sn_tab, a, b, B, S, tm=1024, tn=512):
    W = FNET_GROUPS * FNET_GROUP_DIM
    half = S // 2
    tm = min(tm, half)
    nm = half // tm
    nn = W // tn
    scale = float((S * FNET_GROUP_DIM) ** -0.5)
    return pl.pallas_call(
        functools.partial(_fnet_pos_kernel, scale=scale),
        out_shape=jax.ShapeDtypeStruct((B * half, 2 * W), BF16),
        grid=(2, nm, B, nn),
        in_specs=[
            pl.BlockSpec((1, tm, half), lambda p, m, bb, n: (p, m, 0)),
            pl.BlockSpec((1, tm, half), lambda p, m, bb, n: (p, m, 0)),
            pl.BlockSpec((1, half, tn), lambda p, m, bb, n: (p, bb, n)),
            pl.BlockSpec((1, half, tn), lambda p, m, bb, n: (p, bb, n)),
        ],
        out_specs=pl.BlockSpec((tm, tn), lambda p, m, bb, n: (bb * nm + m, p * nn + n)),
        compiler_params=_cparams("parallel", "parallel", "parallel", "parallel"),
        name="fnet_position",
    )(c_tab, sn_tab, a, b)


def _dft_tables(S):
    gd = FNET_GROUP_DIM
    ck = (np.arange(gd)[:, None] * np.arange(gd)[None, :]) % gd
    ang = 2.0 * np.pi * ck / gd
    w_cs = jnp.asarray(np.concatenate([np.cos(ang), np.sin(ang)], axis=1), F32).astype(BF16)
    kb = 64
    half = S // 2
    n = lax.broadcasted_iota(jnp.int32, (1, half), 1)

    def thin(rows, period):
        ang_ = ((lax.broadcasted_iota(jnp.int32, (rows, 1), 0) * n) % period).astype(F32) * (2.0 * np.pi / period)
        return jnp.cos(ang_), jnp.sin(ang_)

    c_hi, s_hi = thin(S // kb, S // kb)
    c_lo, s_lo = thin(kb, S)
    c_hi, s_hi = c_hi[None, :, None, :], s_hi[None, :, None, :]
    split = lambda t: jnp.stack([t[0::2], t[1::2]])[:, None]
    c_lo, s_lo = split(c_lo), split(s_lo)
    cos_t = (c_hi * c_lo - s_hi * s_lo).reshape(2, half, half)
    nsin_t = -(s_hi * c_lo + c_hi * s_lo).reshape(2, half, half)
    return w_cs, cos_t.astype(BF16), nsin_t.astype(BF16)


def _merge_kernel(yn_ref, ym_ref, yf_ref, w_ref, g0_ref, g1_ref, g2_ref, o_ref, wbf_ref):
    @pl.when(pl.program_id(1) == 0)
    def _():
        wbf_ref[...] = w_ref[0].astype(BF16)

    acc = g0_ref[...].astype(F32) * jnp.dot(yn_ref[...], wbf_ref[0], preferred_element_type=F32)
    acc += g1_ref[...].astype(F32) * jnp.dot(ym_ref[...], wbf_ref[1], preferred_element_type=F32)
    acc += g2_ref[...].astype(F32) * jnp.dot(yf_ref[...], wbf_ref[2], preferred_element_type=F32)
    o_ref[...] = acc.astype(o_ref.dtype)


def _merge(y_na, y_mla, y_f, w_branch, layer, proj, tm=1024, tn=512):
    T = y_na.shape[0]
    ybs = pl.BlockSpec((tm, BRANCH_WIDTH), lambda j, i: (i, 0))

    def gate_spec(br):
        off = (COL_GATE + br * D_MODEL) // tn
        return pl.BlockSpec((tm, tn), lambda j, i: (i, off + j))

    return pl.pallas_call(
        _merge_kernel,
        out_shape=jax.ShapeDtypeStruct((T, D_MODEL), BF16),
        grid=(D_MODEL // tn, T // tm),
        in_specs=[ybs, ybs, ybs,
                  pl.BlockSpec((1, N_BRANCHES, BRANCH_WIDTH, tn), lambda j, i: (layer, 0, 0, j)),
                  gate_spec(0), gate_spec(1), gate_spec(2)],
        out_specs=pl.BlockSpec((tm, tn), lambda j, i: (i, j)),
        scratch_shapes=[pltpu.VMEM((N_BRANCHES, BRANCH_WIDTH, tn), BF16)],
        compiler_params=_cparams("parallel", "arbitrary"),
        name="merge",
    )(y_na, y_mla, y_f, w_branch, proj, proj, proj)


def _outproj_kernel(m_ref, w_ref, h_ref, o_ref, wbf_ref):
    @pl.when(pl.program_id(1) == 0)
    def _():
        wbf_ref[...] = w_ref[0].astype(BF16)

    o_ref[...] = h_ref[...] + jnp.dot(m_ref[...], wbf_ref[...], preferred_element_type=F32)


def _outproj(merged, w_o, layer, h2, tm=1024, tn=512):
    T = merged.shape[0]
    return pl.pallas_call(
        _outproj_kernel,
        out_shape=jax.ShapeDtypeStruct((T, D_MODEL), F32),
        grid=(D_MODEL // tn, T // tm),
        in_specs=[
            pl.BlockSpec((tm, D_MODEL), lambda j, i: (i, 0)),
            pl.BlockSpec((1, D_MODEL, tn), lambda j, i: (layer, 0, j)),
            pl.BlockSpec((tm, tn), lambda j, i: (i, j)),
        ],
        out_specs=pl.BlockSpec((tm, tn), lambda j, i: (i, j)),
        scratch_shapes=[pltpu.VMEM((D_MODEL, tn), BF16)],
        compiler_params=_cparams("parallel", "arbitrary"),
        name="outproj",
    )(merged, w_o, h2)


def _router_kernel(x_ref, g_ref, wrt_ref, aff_ref):
    x = x_ref[...]
    ms = jnp.mean(x * x, axis=-1, keepdims=True)
    xn = x * lax.rsqrt(ms + RMS_EPS) * g_ref[...]
    logits = lax.dot_general(wrt_ref[...], xn, (((1,), (1,)), ((), ())), preferred_element_type=F32,
                             precision=lax.Precision.HIGHEST)
    m = jnp.max(logits, axis=0, keepdims=True)
    e = jnp.exp(logits - m)
    aff_ref[0] = e / jnp.sum(e, axis=0, keepdims=True)


def _router(h2, gain, w_router_t, B, S, tm=1024):
    nt = S // tm
    return pl.pallas_call(
        _router_kernel,
        out_shape=jax.ShapeDtypeStruct((B, N_EXPERTS, S), F32),
        grid=(B, nt),
        in_specs=[
            pl.BlockSpec((tm, D_MODEL), lambda b, i: (b * nt + i, 0)),
            pl.BlockSpec((1, D_MODEL), lambda b, i: (0, 0)),
            pl.BlockSpec((N_EXPERTS, D_MODEL), lambda b, i: (0, 0)),
        ],
        out_specs=pl.BlockSpec((1, N_EXPERTS, tm), lambda b, i: (b, 0, i)),
        compiler_params=_cparams("parallel", "parallel"),
        name="router",
    )(h2, gain, w_router_t)


def _route_kernel(a_ref, val_ref, gate_ref, starts_ref, kmax_ref, pos_s, pv_s, a_s, *, cap, tt):
    E, S = a_ref.shape[1], a_ref.shape[2]
    a = a_ref[0]

    def as_float(b):
        return lax.bitcast_convert_type(b, F32)

    def bisect(_, carry):
        lo, hi = carry
        mid = lo + ((hi - lo + 1) >> 1)
        cnt = jnp.sum(jnp.where(a >= as_float(mid), 1.0, 0.0), axis=1, keepdims=True)
        ge = cnt >= cap
        return jnp.where(ge, mid, lo), jnp.where(ge, hi, mid - 1)

    lo0 = jnp.zeros((E, 1), I32)
    hi0 = jnp.full((E, 1), 0x7F7FFFFF, I32)
    thr_bits, _ = lax.fori_loop(0, 31, bisect, (lo0, hi0))
    thr, thr_up = as_float(thr_bits), as_float(thr_bits + 1)

    r_i = lax.broadcasted_iota(I32, (PFX_BLK, PFX_BLK), 0)
    c_i = lax.broadcasted_iota(I32, (PFX_BLK, PFX_BLK), 1)
    tri = jnp.where(r_i < c_i, 1.0, 0.0).astype(BF16)

    def excl_prefix(mask):
        x = jnp.where(mask, 1.0, 0.0).astype(BF16)
        carry = jnp.zeros((E, 1), F32)
        outs = []
        for j in range(S // PFX_BLK):
            blk = x[:, j * PFX_BLK:(j + 1) * PFX_BLK]
            outs.append(jnp.dot(blk, tri, preferred_element_type=F32) + carry)
            carry = carry + jnp.sum(blk.astype(F32), axis=1, keepdims=True)
        return jnp.concatenate(outs, axis=1)

    gt = a >= thr_up
    eq = (a >= thr) & (a < thr_up)
    need = cap - jnp.sum(jnp.where(gt, 1.0, 0.0), axis=1, keepdims=True)
    sel = gt | (eq & (excl_prefix(eq) < need))
    pos = excl_prefix(sel)

    self_ = jnp.where(sel, 1.0, 0.0)
    e_r = lax.broadcasted_iota(I32, (E, E), 0)
    e_c = lax.broadcasted_iota(I32, (E, E), 1)
    low = jnp.where(e_c < e_r, 1.0, 0.0).astype(BF16)
    rank = jnp.dot(low, self_.astype(BF16), preferred_element_type=F32)
    tok = lax.broadcasted_iota(I32, (E, S), 1)
    pos_s[...] = jnp.where(sel, pos.astype(I32), -1)
    pv_s[...] = (tok + (rank.astype(I32) << TOK_BITS)).astype(F32)
    a_s[...] = a

    lane = lax.broadcasted_iota(I32, (E, 128), 1)
    starts = jnp.full((E, 128), cap, I32)
    kcount = jnp.sum(self_, axis=0, keepdims=True)
    kmax = jnp.zeros((8, 128), I32)
    lane8 = lax.broadcasted_iota(I32, (8, 128), 1)
    posi = pos.astype(I32)
    for j in range(S // tt):
        starts = jnp.where(lane == j, posi[:, j * tt:j * tt + 1], starts)
        kj = jnp.max(kcount[:, j * tt:(j + 1) * tt], axis=1, keepdims=True).astype(I32)
        kmax = jnp.where(lane8 == j, kj, kmax)
    starts_ref[0] = starts
    kmax_ref[0] = kmax

    c_iota = lax.broadcasted_iota(I32, (cap, PFX_BLK), 0)
    lane_c = lax.broadcasted_iota(I32, (cap, 128), 1)

    def per_expert(e, carry):
        val_acc, gate_acc = carry
        v_part = jnp.zeros((cap, PFX_BLK), F32)
        g_part = jnp.zeros((cap, PFX_BLK), F32)
        for j in range(S // PFX_BLK):
            sl = pl.ds(j * PFX_BLK, PFX_BLK)
            hit = pos_s[pl.ds(e, 1), sl] == c_iota
            v_part = v_part + jnp.where(hit, pv_s[pl.ds(e, 1), sl], 0.0)
            g_part = g_part + jnp.where(hit, a_s[pl.ds(e, 1), sl], 0.0)
        v = jnp.sum(v_part, axis=1, keepdims=True).astype(I32)
        g = jnp.sum(g_part, axis=1, keepdims=True)
        return jnp.where(lane_c == e, v, val_acc), jnp.where(lane_c == e, g, gate_acc)

    val, gate = lax.fori_loop(0, E, per_expert, (jnp.zeros((cap, 128), I32), jnp.zeros((cap, 128), F32)))
    val_ref[0] = val
    gate_ref[0] = gate


def _route(aff_t, cap, tt=COMB_TT):
    B, E, S = aff_t.shape
    assert S <= (1 << TOK_BITS) and S // tt < 128 and S % PFX_BLK == 0
    return pl.pallas_call(
        functools.partial(_route_kernel, cap=cap, tt=tt),
        out_shape=(
            jax.ShapeDtypeStruct((B, cap, 128), I32),
            jax.ShapeDtypeStruct((B, cap, 128), F32),
            jax.ShapeDtypeStruct((B, E, 128), I32),
            jax.ShapeDtypeStruct((B, 8, 128), I32),
        ),
        grid=(B,),
        in_specs=[pl.BlockSpec((1, E, S), lambda b: (b, 0, 0))],
        out_specs=(
            pl.BlockSpec((1, cap, 128), lambda b: (b, 0, 0)),
            pl.BlockSpec((1, cap, 128), lambda b: (b, 0, 0)),
            pl.BlockSpec((1, E, 128), lambda b: (b, 0, 0)),
            pl.BlockSpec((1, 8, 128), lambda b: (b, 0, 0)),
        ),
        scratch_shapes=[pltpu.VMEM((E, S), I32), pltpu.VMEM((E, S), F32), pltpu.VMEM((E, S), F32)],
        compiler_params=_cparams("parallel"),
        name="route",
    )(aff_t)


def _gather_kernel(rows_ref, h_hbm, g_ref, o_ref, buf, sem, *, gt):
    i = pl.program_id(0)
    slot = i % 2

    def issue_tile(t, s):
        def issue(r, _):
            pltpu.make_async_copy(h_hbm.at[pl.ds(rows_ref[t * gt + r], 1)], buf.at[s, pl.ds(r, 1)], sem.at[s]).start()
            return 0

        lax.fori_loop(0, gt, issue, 0, unroll=8)

    @pl.when(i == 0)
    def _():
        issue_tile(0, 0)

    @pl.when(i + 1 < pl.num_programs(0))
    def _():
        issue_tile(i + 1, 1 - slot)

    pltpu.make_async_copy(h_hbm.at[pl.ds(0, gt)], buf.at[slot], sem.at[slot]).wait()
    x = buf[slot]
    ms = jnp.mean(x * x, axis=-1, keepdims=True)
    o_ref[...] = (x * lax.rsqrt(ms + RMS_EPS) * g_ref[...]).astype(o_ref.dtype)


def _gather_norm(rows, h2, gain, gt=1024):
    n = rows.shape[0]
    gt = min(gt, n)
    return pl.pallas_call(
        functools.partial(_gather_kernel, gt=gt),
        out_shape=jax.ShapeDtypeStruct((n, D_MODEL), BF16),
        grid_spec=pltpu.PrefetchScalarGridSpec(
            num_scalar_prefetch=1,
            grid=(n // gt,),
            in_specs=[pl.BlockSpec(memory_space=pl.ANY), pl.BlockSpec((1, D_MODEL), lambda i, rows: (0, 0))],
            out_specs=pl.BlockSpec((gt, D_MODEL), lambda i, rows: (i, 0)),
            scratch_shapes=[pltpu.VMEM((2, gt, D_MODEL), F32), pltpu.SemaphoreType.DMA((2,))],
        ),
        compiler_params=_cparams("arbitrary", disable_bounds_checks=True),
        name="gather_norm",
    )(rows, h2, gain)


def _combine_kernel(dst_ref, starts_ref, kmax_ref, h_ref, ye_hbm, fg_ref, o_ref, planes, sem, *,
                    B, E, cap, tt, nt_pad, final_norm):
    b = pl.program_id(0)
    j = pl.program_id(1)
    k = kmax_ref[b * nt_pad + j]

    def zero(p, _):
        planes[pl.ds(pl.multiple_of(p * tt, tt), tt), :] = jnp.zeros((tt, D_MODEL), F32)
        return 0

    lax.fori_loop(0, k, zero, 0)

    def row_copy(slot, dst_row):
        return pltpu.make_async_copy(ye_hbm.at[pl.ds(slot, 1)], planes.at[pl.ds(dst_row, 1)], sem)

    def per_expert(e, n):
        sbase = (b * E + e) * nt_pad + j
        c0 = starts_ref[sbase]
        c1 = starts_ref[sbase + 1]
        lbase = (e * B + b) * cap

        def issue(c):
            row_copy(lbase + c, dst_ref[lbase + c]).start()

        def issue4(i, _):
            for u in range(4):
                issue(c0 + 4 * i + u)
            return 0

        def issue1(i, _):
            issue(c1 - 1 - i)
            return 0

        cnt = c1 - c0
        lax.fori_loop(0, cnt >> 2, issue4, 0)
        lax.fori_loop(0, cnt & 3, issue1, 0)
        return n + cnt

    n = lax.fori_loop(0, E, per_expert, 0)

    for bit in range((E * tt).bit_length()):
        @pl.when(((n >> bit) & 1) == 1)
        def _():
            rows = 1 << bit
            pltpu.make_async_copy(ye_hbm.at[pl.ds(0, rows)], planes.at[pl.ds(0, rows)], sem).wait()

    for cs in range(D_MODEL // 128):
        cols = slice(cs * 128, (cs + 1) * 128)

        def add(p, acc):
            return acc + planes[pl.ds(pl.multiple_of(p * tt, tt), tt), cols]

        o_ref[:, cols] = lax.fori_loop(0, k, add, h_ref[:, cols])

    if final_norm:
        x = o_ref[...]
        ms = jnp.mean(x * x, axis=-1, keepdims=True)
        o_ref[...] = x * lax.rsqrt(ms + RMS_EPS) * fg_ref[...]


def _combine(h2, ye, dst_rows, starts, kmax, final_gain, B, S, cap, tt=COMB_TT):
    E = N_EXPERTS
    nt = S // tt
    nt_pad = starts.shape[0] // (B * E)
    final_norm = final_gain is not None
    fg = final_gain if final_norm else jnp.ones((1, D_MODEL), F32)
    return pl.pallas_call(
        functools.partial(_combine_kernel, B=B, E=E, cap=cap, tt=tt, nt_pad=nt_pad, final_norm=final_norm),
        out_shape=jax.ShapeDtypeStruct(h2.shape, F32),
        grid_spec=pltpu.PrefetchScalarGridSpec(
            num_scalar_prefetch=3,
            grid=(B, nt),
            in_specs=[pl.BlockSpec((tt, D_MODEL), lambda b, j, *_: (b * nt + j, 0)),
                      pl.BlockSpec(memory_space=pl.ANY),
                      pl.BlockSpec((1, D_MODEL), lambda b, j, *_: (0, 0))],
            out_specs=pl.BlockSpec((tt, D_MODEL), lambda b, j, *_: (b * nt + j, 0)),
            scratch_shapes=[pltpu.VMEM((E * tt, D_MODEL), F32), pltpu.SemaphoreType.DMA(())],
        ),
        compiler_params=_cparams("arbitrary", "arbitrary", disable_bounds_checks=True),
        name="combine",
    )(dst_rows, starts, kmax, h2, ye, fg)


def _ffn_up_kernel(x_ref, wg_ref, wu_ref, o_ref):
    x = x_ref[...]
    a = jnp.dot(x, wg_ref[0, 0].astype(BF16), preferred_element_type=F32)
    u = jnp.dot(x, wu_ref[0, 0].astype(BF16), preferred_element_type=F32)
    o_ref[...] = (a * (1.0 / (1.0 + jnp.exp(-a))) * u).astype(o_ref.dtype)


def _ffn_down_kernel(hid_ref, wd_ref, gate_ref, o_ref):
    y = jnp.dot(hid_ref[...], wd_ref[0, 0].astype(BF16), preferred_element_type=F32)
    o_ref[...] = y * gate_ref[...]


def _experts(xe, wg, wu, wd, layer, gate, tf=256, tn=512):
    n_tok = xe.shape[0]
    per_e = n_tok // N_EXPERTS
    ff = wg.shape[-1]
    hid = pl.pallas_call(
        _ffn_up_kernel,
        out_shape=jax.ShapeDtypeStruct((n_tok, ff), BF16),
        grid=(N_EXPERTS, ff // tf),
        in_specs=[
            pl.BlockSpec((per_e, D_MODEL), lambda e, f: (e, 0)),
            pl.BlockSpec((1, 1, D_MODEL, tf), lambda e, f: (layer, e, 0, f)),
            pl.BlockSpec((1, 1, D_MODEL, tf), lambda e, f: (layer, e, 0, f)),
        ],
        out_specs=pl.BlockSpec((per_e, tf), lambda e, f: (e, f)),
        compiler_params=_cparams("parallel", "arbitrary"),
        name="ffn_up",
    )(xe, wg, wu)
    return pl.pallas_call(
        _ffn_down_kernel,
        out_shape=jax.ShapeDtypeStruct((n_tok, D_MODEL), F32),
        grid=(N_EXPERTS, D_MODEL // tn),
        in_specs=[
            pl.BlockSpec((per_e, ff), lambda e, n: (e, 0)),
            pl.BlockSpec((1, 1, ff, tn), lambda e, n: (layer, e, 0, n)),
            pl.BlockSpec((per_e, 1), lambda e, n: (e, 0)),
        ],
        out_specs=pl.BlockSpec((per_e, tn), lambda e, n: (e, n)),
        compiler_params=_cparams("parallel", "arbitrary"),
        name="ffn_down",
    )(hid, wd, gate)


def _moe_layer(h2, gain, w_router, wg, wu, wd, layer, final_gain, B, S):
    E = N_EXPERTS
    cap = CAPACITY_FACTOR * S // E
    nt = S // COMB_TT
    aff_t = _router(h2, gain, jnp.transpose(w_router), B, S)
    val, gate, starts, kmax = _route(aff_t, cap)
    val_e = jnp.transpose(val[:, :, :E], (2, 0, 1))
    gate_e = jnp.transpose(gate[:, :, :E], (2, 0, 1)).reshape(-1, 1)
    tok_e = val_e & ((1 << TOK_BITS) - 1)
    rows = (tok_e + (jnp.arange(B, dtype=I32) * S)[None, :, None]).reshape(-1)
    xe = _gather_norm(rows, h2, gain)
    ye = _experts(xe, wg, wu, wd, layer, gate_e)
    dst = ((val_e >> TOK_BITS) * COMB_TT + tok_e % COMB_TT).reshape(-1)
    starts_flat = starts[:, :, :nt + 1].reshape(-1)
    kmax_flat = kmax[:, 0, :nt + 1].reshape(-1)
    return _combine(h2, ye, dst, starts_flat, kmax_flat, final_gain, B, S, cap)


def _pad_cols(w, n):
    return jnp.pad(w, ((0, 0), (0, n - w.shape[1])))


def _rot_cols(w):
    half = w.shape[1] // 2
    return jnp.concatenate([-w[:, half:], w[:, :half]], axis=1)


def _prep_in_weights(w_in, b_gate):
    o = np.cumsum([0, 3 * NA_WIDTH, MLA_Q_LORA, MLA_KV_LORA, MLA_ROPE, FNET_GROUPS * FNET_GROUP_DIM])
    qkv, cq, ckv, kr, uf, gl = (w_in[:, o[0]:o[1]], w_in[:, o[1]:o[2]], w_in[:, o[2]:o[3]], w_in[:, o[3]:o[4]],
                                w_in[:, o[4]:o[5]], w_in[:, o[5]:])
    w_all = jnp.concatenate([qkv, uf, _pad_cols(cq, CQ_PAD), _pad_cols(ckv, CKV_PAD), _pad_cols(kr, 128),
                             _pad_cols(_rot_cols(kr), 128), gl], axis=1).astype(BF16)
    b_all = jnp.concatenate([jnp.zeros((COL_GATE,), F32), b_gate]).reshape(1, PROJ_COLS)
    return w_all, b_all


def _prep_mla_weights(w_uq, q_norm, w_ukv, kv_norm):
    qd = MLA_NOPE + MLA_ROPE
    wq3 = w_uq.reshape(MLA_Q_LORA, MLA_HEADS, qd)
    nope, pe = wq3[:, :, :MLA_NOPE], wq3[:, :, MLA_NOPE:]
    z64 = jnp.zeros((MLA_Q_LORA, MLA_HEADS, 64), F32)
    wq = jnp.concatenate([nope, pe, z64], axis=2).reshape(MLA_Q_LORA, MLA_HEADS * 256)
    rot = jnp.concatenate([-pe[:, :, 32:], pe[:, :, :32], z64], axis=2).reshape(MLA_Q_LORA, MLA_HEADS * 128)
    rpad = ((0, CQ_PAD - MLA_Q_LORA), (0, 0))
    wq = jnp.transpose(jnp.pad(wq, rpad)).astype(BF16)
    rot = jnp.transpose(jnp.pad(rot, rpad)).astype(BF16)
    wkv3 = jnp.pad(w_ukv, ((0, CKV_PAD - MLA_KV_LORA), (0, 0))).reshape(CKV_PAD, MLA_HEADS, MLA_NOPE + MLA_V)
    wk = wkv3[:, :, :MLA_NOPE].reshape(CKV_PAD, MLA_HEADS * MLA_NOPE).astype(BF16)
    wvt = jnp.transpose(wkv3[:, :, MLA_NOPE:], (1, 2, 0)).reshape(MLA_HEADS * MLA_V, CKV_PAD).astype(BF16)
    qn = jnp.pad(q_norm, (0, CQ_PAD - MLA_Q_LORA)).reshape(1, CQ_PAD)
    kvn = jnp.pad(kv_norm, (0, CKV_PAD - MLA_KV_LORA)).reshape(1, CKV_PAD)
    return wq, rot, wk, wvt, qn, kvn


def _rope_tables128(S):
    pos = jnp.arange(S, dtype=F32)
    inv = 1.0 / (ROPE_THETA ** (jnp.arange(0, MLA_ROPE, 2, dtype=F32) / MLA_ROPE))
    ang = pos[:, None] * inv[None, :]
    z = jnp.zeros((S, 64), F32)
    cos, sin = jnp.cos(ang), jnp.sin(ang)
    return jnp.concatenate([cos, cos, z], axis=1), jnp.concatenate([sin, sin, z], axis=1)


def kernel(x, w_in, b_gate, w_uq, q_norm, w_ukv, kv_norm, na_rpb, w_branch, w_o, norm_mix, norm_moe,
           w_router, w_exp_gate, w_exp_up, w_exp_down, norm_final):
    B, S, D = x.shape
    T = B * S
    depth = w_in.shape[0]
    cos128, sin128 = _rope_tables128(S)
    cos_t, sin_t = jnp.transpose(cos128), jnp.transpose(sin128)
    w_cs, c_tab, sn_tab = _dft_tables(S)
    h = x.reshape(T, D)
    for l in range(depth):
        w_all, b_all = _prep_in_weights(w_in[l], b_gate[l])
        proj = _inproj(h, norm_mix[l].reshape(1, D), w_all, b_all)
        y_na = _na_attention(proj, _na_bias_tables(na_rpb[l]), B, S)
        wq, wrot, wk, wvt, qn, kvn = _prep_mla_weights(w_uq[l], q_norm[l], w_ukv[l], kv_norm[l])
        q, k, vt = _mla_up(proj, cos128, sin128, cos_t, sin_t, qn, kvn, wq, wrot, wk, wvt, B, S)
        y_mla = _mla_attn(q, k, vt, B, S)
        fa, fb = _fnet_channel(proj, w_cs, B, S)
        y_f = _fnet_position(c_tab, sn_tab, fa, fb, B, S).reshape(T, FNET_GROUPS * FNET_GROUP_DIM)
        merged = _merge(y_na, y_mla, y_f, w_branch, l, proj)
        h = _outproj(merged, w_o, l, h)
        final_gain = norm_final.reshape(1, D) if l == depth - 1 else None
        h = _moe_layer(h, norm_moe[l].reshape(1, D), w_router[l], w_exp_gate, w_exp_up, w_exp_down, l, final_gain,
                       B, S)
    return h.reshape(B, S, D)
```

```python
import functools

import numpy as np
import jax
import jax.numpy as jnp
from jax import lax
from jax.experimental import pallas as pl
from jax.experimental.pallas import tpu as pltpu

D_MODEL = 2048
GRID_W = 64
NA_HEADS = 16
NA_HEAD_DIM = 64
NA_WIDTH = NA_HEADS * NA_HEAD_DIM
NA_KH = 8
NA_KW = 16
MLA_HEADS = 8
MLA_NOPE = 128
MLA_ROPE = 64
MLA_V = 128
MLA_Q_LORA = 448
MLA_KV_LORA = 160
ROPE_THETA = 10000.0
FNET_GROUPS = 4
FNET_GROUP_DIM = 256
N_BRANCHES = 3
BRANCH_WIDTH = 1024
N_EXPERTS = 16
EXPERT_FF = 2048
CAPACITY_FACTOR = 2
RMS_EPS = 1e-6
NEG_INF = -1e30

F32 = jnp.float32
BF16 = jnp.bfloat16
I32 = jnp.int32

COL_QKV = 0
COL_UF = 3072
COL_LAT = 4096
COL_GATE = 5120
PROJ_COLS = COL_GATE + N_BRANCHES * D_MODEL
LAT_W = 1024
CQ_PAD = 512
CKV_PAD = 256
VT_ROWS = MLA_V + 16
LOG2E = 1.4426950408889634
TOK_BITS = 12
PFX_BLK = 512
COMB_TT = 256

VMEM_LIMIT = 56 * 1024 * 1024


def _cparams(*sem, **kw):
    return pltpu.CompilerParams(dimension_semantics=sem, vmem_limit_bytes=VMEM_LIMIT, **kw)


def _inproj_kernel(x_ref, g_ref, w_ref, b_ref, o_ref, xn_ref, *, gate_tile0):
    j = pl.program_id(1)

    @pl.when(j == 0)
    def _():
        x = x_ref[...]
        ms = jnp.mean(x * x, axis=-1, keepdims=True)
        xn_ref[...] = (x * lax.rsqrt(ms + RMS_EPS) * g_ref[...]).astype(BF16)

    acc = jnp.dot(xn_ref[...], w_ref[...], preferred_element_type=F32)

    @pl.when(j < gate_tile0)
    def _():
        o_ref[...] = acc.astype(o_ref.dtype)

    @pl.when(j >= gate_tile0)
    def _():
        z = acc + b_ref[...]
        o_ref[...] = (0.5 * jnp.tanh(0.5 * z) + 0.5).astype(o_ref.dtype)


def _inproj(h2, gain, w_all, b_all, tm=1024, tn=1024):
    T = h2.shape[0]
    return pl.pallas_call(
        functools.partial(_inproj_kernel, gate_tile0=COL_GATE // tn),
        out_shape=jax.ShapeDtypeStruct((T, PROJ_COLS), BF16),
        grid=(T // tm, PROJ_COLS // tn),
        in_specs=[
            pl.BlockSpec((tm, D_MODEL), lambda i, j: (i, 0)),
            pl.BlockSpec((1, D_MODEL), lambda i, j: (0, 0)),
            pl.BlockSpec((D_MODEL, tn), lambda i, j: (0, j)),
            pl.BlockSpec((1, tn), lambda i, j: (0, j)),
        ],
        out_specs=pl.BlockSpec((tm, tn), lambda i, j: (i, j)),
        scratch_shapes=[pltpu.VMEM((tm, D_MODEL), BF16)],
        compiler_params=_cparams("parallel", "arbitrary"),
        name="inproj",
    )(h2, gain, w_all, b_all)


def _na_kernel(q_ref, k_ref, v_ref, bias_a_ref, bias_b_ref, o_ref, *, rows):
    i = pl.program_id(1)
    nk = NA_KH * GRID_W
    lo = lax.broadcasted_iota(jnp.int32, (GRID_W, 128), 1) < NA_HEAD_DIM
    nt = (((1,), (1,)), ((), ()))
    row0 = []
    for rr in range(2):
        start = jnp.clip(2 * i + rr - NA_KH // 2, 0, rows - NA_KH)
        row0.append(pl.multiple_of(start * GRID_W, GRID_W))

    def scores(rr, hp):
        sl = slice(hp * 128, (hp + 1) * 128)
        q2 = q_ref[rr * GRID_W:(rr + 1) * GRID_W, sl]
        zero = jnp.zeros_like(q2)
        qq = jnp.concatenate([jnp.where(lo, q2, zero), jnp.where(lo, zero, q2)], axis=0)
        return lax.dot_general(qq, k_ref[pl.ds(row0[rr], nk), sl], nt, preferred_element_type=F32)

    chains = [(rr, hp) for hp in range(NA_HEADS // 2) for rr in range(2)]
    s_next = scores(*chains[0])
    for n, (rr, hp) in enumerate(chains):
        s = s_next
        if n + 1 < len(chains):
            s_next = scores(*chains[n + 1])
        sl = slice(hp * 128, (hp + 1) * 128)
        bias_ref = bias_a_ref if rr == 0 else bias_b_ref
        s = s * (NA_HEAD_DIM ** -0.5 * LOG2E) + bias_ref[0, hp]
        m = jnp.max(s, axis=-1, keepdims=True)
        p = jnp.exp2(s - m)
        l = jnp.sum(p, axis=-1, keepdims=True)
        o = jnp.dot(p.astype(BF16), v_ref[pl.ds(row0[rr], nk), sl], preferred_element_type=F32) / l
        o_ref[rr * GRID_W:(rr + 1) * GRID_W, sl] = jnp.where(lo, o[:GRID_W], o[GRID_W:]).astype(o_ref.dtype)


def _na_bias_kernel(toep_ref, o_ref):
    t = pl.program_id(0)
    hp = pl.program_id(1)
    for a in range(2):
        for i in range(NA_KH):
            o_ref[0, 0, a * GRID_W:(a + 1) * GRID_W, i * GRID_W:(i + 1) * GRID_W] = (
                toep_ref[2 * hp + a, i - t + NA_KH - 1])


def _na_bias_tables(rpb):
    cols = np.arange(GRID_W)
    col_start = np.clip(cols - NA_KW // 2, 0, GRID_W - NA_KW)
    col_valid = (cols[None, :] >= col_start[:, None]) & (cols[None, :] < col_start[:, None] + NA_KW)
    col_idx = np.clip(cols[None, :] - cols[:, None] + NA_KW - 1, 0, 2 * NA_KW - 2)
    onehot = (col_idx[None] == np.arange(2 * NA_KW - 1)[:, None, None]).astype(np.float32)
    toep = jnp.einsum("hdc,cqj->hdqj", rpb.astype(F32), jnp.asarray(onehot), precision=lax.Precision.HIGHEST)
    toep = jnp.where(col_valid[None, None], toep * LOG2E, NEG_INF * LOG2E)
    return pl.pallas_call(
        _na_bias_kernel,
        out_shape=jax.ShapeDtypeStruct((NA_KH, NA_HEADS // 2, 2 * GRID_W, NA_KH * GRID_W), F32),
        grid=(NA_KH, NA_HEADS // 2),
        in_specs=[pl.BlockSpec((NA_HEADS, 2 * NA_KH - 1, GRID_W, GRID_W), lambda t, hp: (0, 0, 0, 0))],
        out_specs=pl.BlockSpec((1, 1, 2 * GRID_W, NA_KH * GRID_W), lambda t, hp: (t, hp, 0, 0)),
        compiler_params=_cparams("parallel", "parallel"),
        name="na_bias",
    )(toep)


def _na_attention(proj, bias, B, S):
    rows = S // GRID_W
    assert rows >= NA_KH and rows % 2 == 0
    T = B * S
    half = rows // 2

    def bias_spec(rr):
        def bias_map(b, i):
            r = 2 * i + rr
            return (r - jnp.clip(r - NA_KH // 2, 0, rows - NA_KH), 0, 0, 0)
        return pl.BlockSpec((1, NA_HEADS // 2, 2 * GRID_W, NA_KH * GRID_W), bias_map)

    return pl.pallas_call(
        functools.partial(_na_kernel, rows=rows),
        out_shape=jax.ShapeDtypeStruct((T, NA_WIDTH), BF16),
        grid=(B, half),
        in_specs=[
            pl.BlockSpec((2 * GRID_W, NA_WIDTH), lambda b, i: (b * half + i, 0)),
            pl.BlockSpec((S, NA_WIDTH), lambda b, i: (b, 1)),
            pl.BlockSpec((S, NA_WIDTH), lambda b, i: (b, 2)),
            bias_spec(0), bias_spec(1),
        ],
        out_specs=pl.BlockSpec((2 * GRID_W, NA_WIDTH), lambda b, i: (b * half + i, 0)),
        compiler_params=_cparams("parallel", "arbitrary"),
        name="na_attn",
    )(proj, proj, proj, bias, bias)


def _mla_up_kernel(lat_ref, cos_ref, sin_ref, cost_ref, sint_ref, qn_ref, kvn_ref, wq_ref, wrot_ref, wk_ref,
                   wvt_ref, q_ref, k_ref, vt_ref):
    cq = lat_ref[:, 0:CQ_PAD].astype(F32)
    ms = jnp.sum(cq * cq, axis=-1, keepdims=True) * (1.0 / MLA_Q_LORA)
    xq = (cq * lax.rsqrt(ms + RMS_EPS) * qn_ref[...]).astype(BF16)
    ckv = lat_ref[:, CQ_PAD:CQ_PAD + CKV_PAD].astype(F32)
    ms2 = jnp.sum(ckv * ckv, axis=-1, keepdims=True) * (1.0 / MLA_KV_LORA)
    xkv = (ckv * lax.rsqrt(ms2 + RMS_EPS) * kvn_ref[...]).astype(BF16)
    cos = cos_ref[...]
    sin = sin_ref[...]
    kpe = (lat_ref[:, 768:896].astype(F32) * cos + lat_ref[:, 896:1024].astype(F32) * sin).astype(BF16)
    scale = (MLA_NOPE + MLA_ROPE) ** -0.5 * LOG2E
    ones = jnp.ones((VT_ROWS - MLA_V, lat_ref.shape[0]), BF16)
    nt = (((1,), (1,)), ((), ()))
    for h in range(MLA_HEADS):
        a = lax.dot_general(wq_ref[h * 256:(h + 1) * 256, :], xq, nt, preferred_element_type=F32)
        rt = lax.dot_general(wrot_ref[h * 128:(h + 1) * 128, :], xq, nt, preferred_element_type=F32)
        q_ref[0, h, 0:128, :] = (a[0:128] * scale).astype(BF16)
        q_ref[0, h, 128:256, :] = ((a[128:256] * cost_ref[...] + rt * sint_ref[...]) * scale).astype(BF16)
        kn = jnp.dot(xkv, wk_ref[:, h * 128:(h + 1) * 128], preferred_element_type=F32)
        k_ref[0, h, :, 0:128] = kn.astype(BF16)
        k_ref[0, h, :, 128:256] = kpe
        vt = lax.dot_general(wvt_ref[h * 128:(h + 1) * 128, :], xkv, nt, preferred_element_type=F32)
        vt_ref[0, h, 0:MLA_V, :] = vt.astype(BF16)
        vt_ref[0, h, MLA_V:VT_ROWS, :] = ones


def _mla_up(proj, cos128, sin128, cos_t, sin_t, qn, kvn, wq, wrot, wk, wvt, B, S, tm=512):
    nt = S // tm
    lat_blk = COL_LAT // LAT_W
    const = lambda b, i: (0, 0)
    return pl.pallas_call(
        _mla_up_kernel,
        out_shape=(
            jax.ShapeDtypeStruct((B, MLA_HEADS, 256, S), BF16),
            jax.ShapeDtypeStruct((B, MLA_HEADS, S, 256), BF16),
            jax.ShapeDtypeStruct((B, MLA_HEADS, VT_ROWS, S), BF16),
        ),
        grid=(B, nt),
        in_specs=[
            pl.BlockSpec((tm, LAT_W), lambda b, i: (b * nt + i, lat_blk)),
            pl.BlockSpec((tm, 128), lambda b, i: (i, 0)),
            pl.BlockSpec((tm, 128), lambda b, i: (i, 0)),
            pl.BlockSpec((128, tm), lambda b, i: (0, i)),
            pl.BlockSpec((128, tm), lambda b, i: (0, i)),
            pl.BlockSpec((1, CQ_PAD), const),
            pl.BlockSpec((1, CKV_PAD), const),
            pl.BlockSpec((MLA_HEADS * 256, CQ_PAD), const),
            pl.BlockSpec((MLA_HEADS * 128, CQ_PAD), const),
            pl.BlockSpec((CKV_PAD, MLA_HEADS * MLA_NOPE), const),
            pl.BlockSpec((MLA_HEADS * MLA_V, CKV_PAD), const),
        ],
        out_specs=(
            pl.BlockSpec((1, MLA_HEADS, 256, tm), lambda b, i: (b, 0, 0, i)),
            pl.BlockSpec((1, MLA_HEADS, tm, 256), lambda b, i: (b, 0, i, 0)),
            pl.BlockSpec((1, MLA_HEADS, VT_ROWS, tm), lambda b, i: (b, 0, 0, i)),
        ),
        compiler_params=_cparams("parallel", "parallel"),
        name="mla_up",
    )(proj, cos128, sin128, cos_t, sin_t, qn, kvn, wq, wrot, wk, wvt)


def _mla_attn_kernel(q_ref, k_ref, vt_ref, o_ref, *, ck):
    qt = q_ref[0, 0]
    n_chunks = k_ref.shape[2] // ck
    m = acc = None

    def scores(c):
        return jnp.dot(k_ref[0, 0, c * ck:(c + 1) * ck, :], qt, preferred_element_type=F32)

    s_next = scores(0)
    for c in range(n_chunks):
        s = s_next
        if c + 1 < n_chunks:
            s_next = scores(c + 1)
        mc = jnp.max(s, axis=0, keepdims=True)
        m_new = mc if c == 0 else jnp.maximum(m, mc)
        p = jnp.exp2(s - m_new).astype(BF16)
        pv = jnp.dot(vt_ref[0, 0, :, c * ck:(c + 1) * ck], p, preferred_element_type=F32)
        acc = pv if c == 0 else acc * jnp.exp2(m - m_new) + pv
        m = m_new
    o = acc[0:MLA_V] / acc[MLA_V:MLA_V + 1]
    o_ref[...] = o.T.astype(o_ref.dtype)


def _mla_attn(q, k, vt, B, S, tq=1024, ck=512):
    nq = S // tq
    return pl.pallas_call(
        functools.partial(_mla_attn_kernel, ck=ck),
        out_shape=jax.ShapeDtypeStruct((B * S, MLA_HEADS * MLA_V), BF16),
        grid=(B, MLA_HEADS, nq),
        in_specs=[
            pl.BlockSpec((1, 1, 256, tq), lambda b, h, i: (b, h, 0, i)),
            pl.BlockSpec((1, 1, S, 256), lambda b, h, i: (b, h, 0, 0)),
            pl.BlockSpec((1, 1, VT_ROWS, S), lambda b, h, i: (b, h, 0, 0)),
        ],
        out_specs=pl.BlockSpec((tq, MLA_V), lambda b, h, i: (b * nq + i, h)),
        compiler_params=_cparams("parallel", "parallel", "arbitrary"),
        name="mla_attn",
    )(q, k, vt)


def _fnet_ch_kernel(lo_ref, hi_ref, w_ref, a_ref, b_ref):
    gd = FNET_GROUP_DIM
    lo = lo_ref[...].astype(F32)
    hi = hi_ref[...].astype(F32)
    for p, u in enumerate(((lo + hi).astype(BF16), (lo - hi).astype(BF16))):
        for g in range(FNET_GROUPS):
            ab = jnp.dot(u[:, g * gd:(g + 1) * gd], w_ref[...], preferred_element_type=F32)
            a_ref[p, :, g * gd:(g + 1) * gd] = ab[:, :gd].astype(a_ref.dtype)
            b_ref[p, :, g * gd:(g + 1) * gd] = ab[:, gd:].astype(b_ref.dtype)


def _fnet_channel(proj, w_cs, B, S, tm=1024):
    W = FNET_GROUPS * FNET_GROUP_DIM
    half = S // 2
    tm = min(tm, half)
    nh = half // tm
    out = jax.ShapeDtypeStruct((2, B * half, W), BF16)
    return pl.pallas_call(
        _fnet_ch_kernel,
        out_shape=(out, out),
        grid=(B, nh),
        in_specs=[
            pl.BlockSpec((tm, W), lambda b, i: (b * 2 * nh + i, COL_UF // W)),
            pl.BlockSpec((tm, W), lambda b, i: (b * 2 * nh + nh + i, COL_UF // W)),
            pl.BlockSpec((FNET_GROUP_DIM, 2 * FNET_GROUP_DIM), lambda b, i: (0, 0)),
        ],
        out_specs=(pl.BlockSpec((2, tm, W), lambda b, i: (0, b * nh + i, 0)),
                   pl.BlockSpec((2, tm, W), lambda b, i: (0, b * nh + i, 0))),
        compiler_params=_cparams("parallel", "parallel"),
        name="fnet_channel",
    )(proj, proj, w_cs)


def _fnet_pos_kernel(c_ref, s_ref, a_ref, b_ref, o_ref, *, scale):
    y = (jnp.dot(c_ref[0], a_ref[0], preferred_element_type=F32)
         + jnp.dot(s_ref[0], b_ref[0], preferred_element_type=F32))
    o_ref[...] = (y * scale).astype(o_ref.dtype)


def _fnet_position(c_tab, sn_tab, a, b, B, S, tm=1024, tn=512):
    W = FNET_GROUPS * FNET_GROUP_DIM
    half = S // 2
    tm = min(tm, half)
    nm = half // tm
    nn = W // tn
    scale = float((S * FNET_GROUP_DIM) ** -0.5)
    return pl.pallas_call(
        functools.partial(_fnet_pos_kernel, scale=scale),
        out_shape=jax.ShapeDtypeStruct((B * half, 2 * W), BF16),
        grid=(2, nm, B, nn),
        in_specs=[
            pl.BlockSpec((1, tm, half), lambda p, m, bb, n: (p, m, 0)),
            pl.BlockSpec((1, tm, half), lambda p, m, bb, n: (p, m, 0)),
            pl.BlockSpec((1, half, tn), lambda p, m, bb, n: (p, bb, n)),
            pl.BlockSpec((1, half, tn), lambda p, m, bb, n: (p, bb, n)),
        ],
        out_specs=pl.BlockSpec((tm, tn), lambda p, m, bb, n: (bb * nm + m, p * nn + n)),
        compiler_params=_cparams("parallel", "parallel", "parallel", "parallel"),
        name="fnet_position",
    )(c_tab, sn_tab, a, b)


def _dft_table_kernel(chi_ref, shi_ref, clo_ref, slo_ref, cos_ref, nsin_ref, *, hb, nlo):
    cl, sl = clo_ref[0], slo_ref[0]
    for u in range(hb):
        ch, sh = chi_ref[u:u + 1, :], shi_ref[u:u + 1, :]
        cos_ref[0, u * nlo:(u + 1) * nlo, :] = (ch * cl - sh * sl).astype(cos_ref.dtype)
        nsin_ref[0, u * nlo:(u + 1) * nlo, :] = (-(sh * cl + ch * sl)).astype(nsin_ref.dtype)


def _dft_tables(S):
    gd = FNET_GROUP_DIM
    ck = (np.arange(gd)[:, None] * np.arange(gd)[None, :]) % gd
    ang = 2.0 * np.pi * ck / gd
    w_cs = jnp.asarray(np.concatenate([np.cos(ang), np.sin(ang)], axis=1), F32).astype(BF16)
    kb = 64
    half = S // 2
    n = lax.broadcasted_iota(jnp.int32, (1, half), 1)

    def thin(rows, period):
        ang_ = ((lax.broadcasted_iota(jnp.int32, (rows, 1), 0) * n) % period).astype(F32) * (2.0 * np.pi / period)
        return jnp.cos(ang_), jnp.sin(ang_)

    c_hi, s_hi = thin(S // kb, S // kb)
    c_lo, s_lo = thin(kb, S)
    split = lambda t: jnp.stack([t[0::2], t[1::2]])
    c_lo, s_lo = split(c_lo), split(s_lo)
    nlo = kb // 2
    hb = 8
    assert (S // kb) % hb == 0
    tab = jax.ShapeDtypeStruct((2, half, half), BF16)
    cos_t, nsin_t = pl.pallas_call(
        functools.partial(_dft_table_kernel, hb=hb, nlo=nlo),
        out_shape=(tab, tab),
        grid=(2, S // kb // hb),
        in_specs=[pl.BlockSpec((hb, half), lambda p, g: (g, 0)), pl.BlockSpec((hb, half), lambda p, g: (g, 0)),
                  pl.BlockSpec((1, nlo, half), lambda p, g: (p, 0, 0)),
                  pl.BlockSpec((1, nlo, half), lambda p, g: (p, 0, 0))],
        out_specs=(pl.BlockSpec((1, hb * nlo, half), lambda p, g: (p, g, 0)),
                   pl.BlockSpec((1, hb * nlo, half), lambda p, g: (p, g, 0))),
        compiler_params=_cparams("parallel", "parallel"),
        name="dft_tables",
    )(c_hi, s_hi, c_lo, s_lo)
    return w_cs, cos_t, nsin_t


def _merge_kernel(yn_ref, ym_ref, yf_ref, w_ref, g0_ref, g1_ref, g2_ref, o_ref, wbf_ref):
    @pl.when(pl.program_id(1) == 0)
    def _():
        wbf_ref[...] = w_ref[0].astype(BF16)

    acc = g0_ref[...].astype(F32) * jnp.dot(yn_ref[...], wbf_ref[0], preferred_element_type=F32)
    acc += g1_ref[...].astype(F32) * jnp.dot(ym_ref[...], wbf_ref[1], preferred_element_type=F32)
    acc += g2_ref[...].astype(F32) * jnp.dot(yf_ref[...], wbf_ref[2], preferred_element_type=F32)
    o_ref[...] = acc.astype(o_ref.dtype)


def _merge(y_na, y_mla, y_f, w_branch, layer, proj, tm=1024, tn=512):
    T = y_na.shape[0]
    ybs = pl.BlockSpec((tm, BRANCH_WIDTH), lambda j, i: (i, 0))

    def gate_spec(br):
        off = (COL_GATE + br * D_MODEL) // tn
        return pl.BlockSpec((tm, tn), lambda j, i: (i, off + j))

    return pl.pallas_call(
        _merge_kernel,
        out_shape=jax.ShapeDtypeStruct((T, D_MODEL), BF16),
        grid=(D_MODEL // tn, T // tm),
        in_specs=[ybs, ybs, ybs,
                  pl.BlockSpec((1, N_BRANCHES, BRANCH_WIDTH, tn), lambda j, i: (layer, 0, 0, j)),
                  gate_spec(0), gate_spec(1), gate_spec(2)],
        out_specs=pl.BlockSpec((tm, tn), lambda j, i: (i, j)),
        scratch_shapes=[pltpu.VMEM((N_BRANCHES, BRANCH_WIDTH, tn), BF16)],
        compiler_params=_cparams("parallel", "arbitrary"),
        name="merge",
    )(y_na, y_mla, y_f, w_branch, proj, proj, proj)


def _outproj_kernel(m_ref, w_ref, h_ref, o_ref, wbf_ref):
    @pl.when(pl.program_id(1) == 0)
    def _():
        wbf_ref[...] = w_ref[0].astype(BF16)

    o_ref[...] = h_ref[...] + jnp.dot(m_ref[...], wbf_ref[...], preferred_element_type=F32)


def _outproj(merged, w_o, layer, h2, tm=1024, tn=512):
    T = merged.shape[0]
    return pl.pallas_call(
        _outproj_kernel,
        out_shape=jax.ShapeDtypeStruct((T, D_MODEL), F32),
        grid=(D_MODEL // tn, T // tm),
        in_specs=[
            pl.BlockSpec((tm, D_MODEL), lambda j, i: (i, 0)),
            pl.BlockSpec((1, D_MODEL, tn), lambda j, i: (layer, 0, j)),
            pl.BlockSpec((tm, tn), lambda j, i: (i, j)),
        ],
        out_specs=pl.BlockSpec((tm, tn), lambda j, i: (i, j)),
        scratch_shapes=[pltpu.VMEM((D_MODEL, tn), BF16)],
        compiler_params=_cparams("parallel", "arbitrary"),
        name="outproj",
    )(merged, w_o, h2)


def _router_kernel(x_ref, g_ref, wrt_ref, aff_ref):
    x = x_ref[...]
    ms = jnp.mean(x * x, axis=-1, keepdims=True)
    xn = x * lax.rsqrt(ms + RMS_EPS) * g_ref[...]
    logits = lax.dot_general(wrt_ref[...], xn, (((1,), (1,)), ((), ())), preferred_element_type=F32,
                             precision=lax.Precision.HIGHEST)
    m = jnp.max(logits, axis=0, keepdims=True)
    e = jnp.exp(logits - m)
    aff_ref[0] = e / jnp.sum(e, axis=0, keepdims=True)


def _router(h2, gain, w_router_t, B, S, tm=1024):
    nt = S // tm
    return pl.pallas_call(
        _router_kernel,
        out_shape=jax.ShapeDtypeStruct((B, N_EXPERTS, S), F32),
        grid=(B, nt),
        in_specs=[
            pl.BlockSpec((tm, D_MODEL), lambda b, i: (b * nt + i, 0)),
            pl.BlockSpec((1, D_MODEL), lambda b, i: (0, 0)),
            pl.BlockSpec((N_EXPERTS, D_MODEL), lambda b, i: (0, 0)),
        ],
        out_specs=pl.BlockSpec((1, N_EXPERTS, tm), lambda b, i: (b, 0, i)),
        compiler_params=_cparams("parallel", "parallel"),
        name="router",
    )(h2, gain, w_router_t)


def _route_kernel(a_ref, val_ref, gate_ref, starts_ref, kmax_ref, pos_s, pv_s, a_s, *, cap, tt):
    E, S = a_ref.shape[1], a_ref.shape[2]
    a = a_ref[0]

    def as_float(b):
        return lax.bitcast_convert_type(b, F32)

    def bisect(_, carry):
        lo, hi = carry
        mid = lo + ((hi - lo + 1) >> 1)
        cnt = jnp.sum(jnp.where(a >= as_float(mid), 1.0, 0.0), axis=1, keepdims=True)
        ge = cnt >= cap
        return jnp.where(ge, mid, lo), jnp.where(ge, hi, mid - 1)

    lo0 = jnp.zeros((E, 1), I32)
    hi0 = jnp.full((E, 1), 0x7F7FFFFF, I32)
    thr_bits, _ = lax.fori_loop(0, 31, bisect, (lo0, hi0))
    thr, thr_up = as_float(thr_bits), as_float(thr_bits + 1)

    r_i = lax.broadcasted_iota(I32, (PFX_BLK, PFX_BLK), 0)
    c_i = lax.broadcasted_iota(I32, (PFX_BLK, PFX_BLK), 1)
    tri = jnp.where(r_i < c_i, 1.0, 0.0).astype(BF16)

    def excl_prefix(mask):
        x = jnp.where(mask, 1.0, 0.0).astype(BF16)
        carry = jnp.zeros((E, 1), F32)
        outs = []
        for j in range(S // PFX_BLK):
            blk = x[:, j * PFX_BLK:(j + 1) * PFX_BLK]
            outs.append(jnp.dot(blk, tri, preferred_element_type=F32) + carry)
            carry = carry + jnp.sum(blk.astype(F32), axis=1, keepdims=True)
        return jnp.concatenate(outs, axis=1)

    gt = a >= thr_up
    eq = (a >= thr) & (a < thr_up)
    need = cap - jnp.sum(jnp.where(gt, 1.0, 0.0), axis=1, keepdims=True)
    sel = gt | (eq & (excl_prefix(eq) < need))
    pos = excl_prefix(sel)

    self_ = jnp.where(sel, 1.0, 0.0)
    e_r = lax.broadcasted_iota(I32, (E, E), 0)
    e_c = lax.broadcasted_iota(I32, (E, E), 1)
    low = jnp.where(e_c < e_r, 1.0, 0.0).astype(BF16)
    rank = jnp.dot(low, self_.astype(BF16), preferred_element_type=F32)
    tok = lax.broadcasted_iota(I32, (E, S), 1)
    pos_s[...] = jnp.where(sel, pos.astype(I32), -1)
    pv_s[...] = (tok + (rank.astype(I32) << TOK_BITS)).astype(F32)
    a_s[...] = a

    lane = lax.broadcasted_iota(I32, (E, 128), 1)
    starts = jnp.full((E, 128), cap, I32)
    kcount = jnp.sum(self_, axis=0, keepdims=True)
    kmax = jnp.zeros((8, 128), I32)
    lane8 = lax.broadcasted_iota(I32, (8, 128), 1)
    posi = pos.astype(I32)
    for j in range(S // tt):
        starts = jnp.where(lane == j, posi[:, j * tt:j * tt + 1], starts)
        kj = jnp.max(kcount[:, j * tt:(j + 1) * tt], axis=1, keepdims=True).astype(I32)
        kmax = jnp.where(lane8 == j, kj, kmax)
    starts_ref[0] = starts
    kmax_ref[0] = kmax

    c_iota = lax.broadcasted_iota(I32, (cap, PFX_BLK), 0)
    lane_c = lax.broadcasted_iota(I32, (cap, 128), 1)

    def per_expert(e, carry):
        val_acc, gate_acc = carry
        v_part = jnp.zeros((cap, PFX_BLK), F32)
        g_part = jnp.zeros((cap, PFX_BLK), F32)
        for j in range(S // PFX_BLK):
            sl = pl.ds(j * PFX_BLK, PFX_BLK)
            hit = pos_s[pl.ds(e, 1), sl] == c_iota
            v_part = v_part + jnp.where(hit, pv_s[pl.ds(e, 1), sl], 0.0)
            g_part = g_part + jnp.where(hit, a_s[pl.ds(e, 1), sl], 0.0)
        v = jnp.sum(v_part, axis=1, keepdims=True).astype(I32)
        g = jnp.sum(g_part, axis=1, keepdims=True)
        return jnp.where(lane_c == e, v, val_acc), jnp.where(lane_c == e, g, gate_acc)

    val, gate = lax.fori_loop(0, E, per_expert, (jnp.zeros((cap, 128), I32), jnp.zeros((cap, 128), F32)))
    val_ref[0] = val
    gate_ref[0] = gate


def _route(aff_t, cap, tt=COMB_TT):
    B, E, S = aff_t.shape
    assert S <= (1 << TOK_BITS) and S // tt < 128 and S % PFX_BLK == 0
    return pl.pallas_call(
        functools.partial(_route_kernel, cap=cap, tt=tt),
        out_shape=(
            jax.ShapeDtypeStruct((B, cap, 128), I32),
            jax.ShapeDtypeStruct((B, cap, 128), F32),
            jax.ShapeDtypeStruct((B, E, 128), I32),
            jax.ShapeDtypeStruct((B, 8, 128), I32),
        ),
        grid=(B,),
        in_specs=[pl.BlockSpec((1, E, S), lambda b: (b, 0, 0))],
        out_specs=(
            pl.BlockSpec((1, cap, 128), lambda b: (b, 0, 0)),
            pl.BlockSpec((1, cap, 128), lambda b: (b, 0, 0)),
            pl.BlockSpec((1, E, 128), lambda b: (b, 0, 0)),
            pl.BlockSpec((1, 8, 128), lambda b: (b, 0, 0)),
        ),
        scratch_shapes=[pltpu.VMEM((E, S), I32), pltpu.VMEM((E, S), F32), pltpu.VMEM((E, S), F32)],
        compiler_params=_cparams("parallel"),
        name="route",
    )(aff_t)


def _gather_kernel(rows_ref, h_hbm, g_ref, o_ref, buf, sem, *, gt):
    i = pl.program_id(0)
    slot = i % 2

    def issue_tile(t, s):
        def issue(r, _):
            pltpu.make_async_copy(h_hbm.at[pl.ds(rows_ref[t * gt + r], 1)], buf.at[s, pl.ds(r, 1)], sem.at[s]).start()
            return 0

        lax.fori_loop(0, gt, issue, 0, unroll=8)

    @pl.when(i == 0)
    def _():
        issue_tile(0, 0)

    @pl.when(i + 1 < pl.num_programs(0))
    def _():
        issue_tile(i + 1, 1 - slot)

    pltpu.make_async_copy(h_hbm.at[pl.ds(0, gt)], buf.at[slot], sem.at[slot]).wait()
    x = buf[slot]
    ms = jnp.mean(x * x, axis=-1, keepdims=True)
    o_ref[...] = (x * lax.rsqrt(ms + RMS_EPS) * g_ref[...]).astype(o_ref.dtype)


def _gather_norm(rows, h2, gain, gt=1024):
    n = rows.shape[0]
    gt = min(gt, n)
    return pl.pallas_call(
        functools.partial(_gather_kernel, gt=gt),
        out_shape=jax.ShapeDtypeStruct((n, D_MODEL), BF16),
        grid_spec=pltpu.PrefetchScalarGridSpec(
            num_scalar_prefetch=1,
            grid=(n // gt,),
            in_specs=[pl.BlockSpec(memory_space=pl.ANY), pl.BlockSpec((1, D_MODEL), lambda i, rows: (0, 0))],
            out_specs=pl.BlockSpec((gt, D_MODEL), lambda i, rows: (i, 0)),
            scratch_shapes=[pltpu.VMEM((2, gt, D_MODEL), F32), pltpu.SemaphoreType.DMA((2,))],
        ),
        compiler_params=_cparams("arbitrary", disable_bounds_checks=True),
        name="gather_norm",
    )(rows, h2, gain)


def _combine_kernel(dst_ref, starts_ref, kmax_ref, h_ref, ye_hbm, fg_ref, o_ref, planes, sem, *,
                    B, E, cap, tt, nt_pad, final_norm):
    b = pl.program_id(0)
    j = pl.program_id(1)
    k = kmax_ref[b * nt_pad + j]

    def zero(p, _):
        planes[pl.ds(pl.multiple_of(p * tt, tt), tt), :] = jnp.zeros((tt, D_MODEL), F32)
        return 0

    lax.fori_loop(0, k, zero, 0)

    def row_copy(slot, dst_row):
        return pltpu.make_async_copy(ye_hbm.at[pl.ds(slot, 1)], planes.at[pl.ds(dst_row, 1)], sem)

    def per_expert(e, n):
        sbase = (b * E + e) * nt_pad + j
        c0 = starts_ref[sbase]
        c1 = starts_ref[sbase + 1]
        lbase = (e * B + b) * cap

        def issue(c):
            row_copy(lbase + c, dst_ref[lbase + c]).start()

        def issue4(i, _):
            for u in range(4):
                issue(c0 + 4 * i + u)
            return 0

        def issue1(i, _):
            issue(c1 - 1 - i)
            return 0

        cnt = c1 - c0
        lax.fori_loop(0, cnt >> 2, issue4, 0)
        lax.fori_loop(0, cnt & 3, issue1, 0)
        return n + cnt

    n = lax.fori_loop(0, E, per_expert, 0)

    for bit in range((E * tt).bit_length()):
        @pl.when(((n >> bit) & 1) == 1)
        def _():
            rows = 1 << bit
            pltpu.make_async_copy(ye_hbm.at[pl.ds(0, rows)], planes.at[pl.ds(0, rows)], sem).wait()

    for cs in range(D_MODEL // 128):
        cols = slice(cs * 128, (cs + 1) * 128)

        def add(p, acc):
            return acc + planes[pl.ds(pl.multiple_of(p * tt, tt), tt), cols]

        o_ref[:, cols] = lax.fori_loop(0, k, add, h_ref[:, cols])

    if final_norm:
        x = o_ref[...]
        ms = jnp.mean(x * x, axis=-1, keepdims=True)
        o_ref[...] = x * lax.rsqrt(ms + RMS_EPS) * fg_ref[...]


def _combine(h2, ye, dst_rows, starts, kmax, final_gain, B, S, cap, tt=COMB_TT):
    E = N_EXPERTS
    nt = S // tt
    nt_pad = starts.shape[0] // (B * E)
    final_norm = final_gain is not None
    fg = final_gain if final_norm else jnp.ones((1, D_MODEL), F32)
    return pl.pallas_call(
        functools.partial(_combine_kernel, B=B, E=E, cap=cap, tt=tt, nt_pad=nt_pad, final_norm=final_norm),
        out_shape=jax.ShapeDtypeStruct(h2.shape, F32),
        grid_spec=pltpu.PrefetchScalarGridSpec(
            num_scalar_prefetch=3,
            grid=(B, nt),
            in_specs=[pl.BlockSpec((tt, D_MODEL), lambda b, j, *_: (b * nt + j, 0)),
                      pl.BlockSpec(memory_space=pl.ANY),
                      pl.BlockSpec((1, D_MODEL), lambda b, j, *_: (0, 0))],
            out_specs=pl.BlockSpec((tt, D_MODEL), lambda b, j, *_: (b * nt + j, 0)),
            scratch_shapes=[pltpu.VMEM((E * tt, D_MODEL), F32), pltpu.SemaphoreType.DMA(())],
        ),
        compiler_params=_cparams("arbitrary", "arbitrary", disable_bounds_checks=True),
        name="combine",
    )(dst_rows, starts, kmax, h2, ye, fg)


def _ffn_up_kernel(x_ref, wg_ref, wu_ref, o_ref):
    x = x_ref[...]
    a = jnp.dot(x, wg_ref[0, 0].astype(BF16), preferred_element_type=F32)
    u = jnp.dot(x, wu_ref[0, 0].astype(BF16), preferred_element_type=F32)
    o_ref[...] = (a * (1.0 / (1.0 + jnp.exp(-a))) * u).astype(o_ref.dtype)


def _ffn_down_kernel(hid_ref, wd_ref, gate_ref, o_ref):
    y = jnp.dot(hid_ref[...], wd_ref[0, 0].astype(BF16), preferred_element_type=F32)
    o_ref[...] = y * gate_ref[...]


def _experts(xe, wg, wu, wd, layer, gate, tf=256, tn=512):
    n_tok = xe.shape[0]
    per_e = n_tok // N_EXPERTS
    ff = wg.shape[-1]
    hid = pl.pallas_call(
        _ffn_up_kernel,
        out_shape=jax.ShapeDtypeStruct((n_tok, ff), BF16),
        grid=(N_EXPERTS, ff // tf),
        in_specs=[
            pl.BlockSpec((per_e, D_MODEL), lambda e, f: (e, 0)),
            pl.BlockSpec((1, 1, D_MODEL, tf), lambda e, f: (layer, e, 0, f)),
            pl.BlockSpec((1, 1, D_MODEL, tf), lambda e, f: (layer, e, 0, f)),
        ],
        out_specs=pl.BlockSpec((per_e, tf), lambda e, f: (e, f)),
        compiler_params=_cparams("parallel", "arbitrary"),
        name="ffn_up",
    )(xe, wg, wu)
    return pl.pallas_call(
        _ffn_down_kernel,
        out_shape=jax.ShapeDtypeStruct((n_tok, D_MODEL), F32),
        grid=(N_EXPERTS, D_MODEL // tn),
        in_specs=[
            pl.BlockSpec((per_e, ff), lambda e, n: (e, 0)),
            pl.BlockSpec((1, 1, ff, tn), lambda e, n: (layer, e, 0, n)),
            pl.BlockSpec((per_e, 1), lambda e, n: (e, 0)),
        ],
        out_specs=pl.BlockSpec((per_e, tn), lambda e, n: (e, n)),
        compiler_params=_cparams("parallel", "arbitrary"),
        name="ffn_down",
    )(hid, wd, gate)


def _moe_layer(h2, gain, w_router, wg, wu, wd, layer, final_gain, B, S):
    E = N_EXPERTS
    cap = CAPACITY_FACTOR * S // E
    nt = S // COMB_TT
    aff_t = _router(h2, gain, jnp.transpose(w_router), B, S)
    val, gate, starts, kmax = _route(aff_t, cap)
    val_e = jnp.transpose(val[:, :, :E], (2, 0, 1))
    gate_e = jnp.transpose(gate[:, :, :E], (2, 0, 1)).reshape(-1, 1)
    tok_e = val_e & ((1 << TOK_BITS) - 1)
    rows = (tok_e + (jnp.arange(B, dtype=I32) * S)[None, :, None]).reshape(-1)
    xe = _gather_norm(rows, h2, gain)
    ye = _experts(xe, wg, wu, wd, layer, gate_e)
    dst = ((val_e >> TOK_BITS) * COMB_TT + tok_e % COMB_TT).reshape(-1)
    starts_flat = starts[:, :, :nt + 1].reshape(-1)
    kmax_flat = kmax[:, 0, :nt + 1].reshape(-1)
    return _combine(h2, ye, dst, starts_flat, kmax_flat, final_gain, B, S, cap)


def _pad_cols(w, n):
    return jnp.pad(w, ((0, 0), (0, n - w.shape[1])))


def _rot_cols(w):
    half = w.shape[1] // 2
    return jnp.concatenate([-w[:, half:], w[:, :half]], axis=1)


def _prep_in_weights(w_in, b_gate):
    o = np.cumsum([0, 3 * NA_WIDTH, MLA_Q_LORA, MLA_KV_LORA, MLA_ROPE, FNET_GROUPS * FNET_GROUP_DIM])
    qkv, cq, ckv, kr, uf, gl = (w_in[:, o[0]:o[1]], w_in[:, o[1]:o[2]], w_in[:, o[2]:o[3]], w_in[:, o[3]:o[4]],
                                w_in[:, o[4]:o[5]], w_in[:, o[5]:])
    w_all = jnp.concatenate([qkv, uf, _pad_cols(cq, CQ_PAD), _pad_cols(ckv, CKV_PAD), _pad_cols(kr, 128),
                             _pad_cols(_rot_cols(kr), 128), gl], axis=1).astype(BF16)
    b_all = jnp.concatenate([jnp.zeros((COL_GATE,), F32), b_gate]).reshape(1, PROJ_COLS)
    return w_all, b_all


def _prep_mla_weights(w_uq, q_norm, w_ukv, kv_norm):
    qd = MLA_NOPE + MLA_ROPE
    wq3 = w_uq.reshape(MLA_Q_LORA, MLA_HEADS, qd)
    nope, pe = wq3[:, :, :MLA_NOPE], wq3[:, :, MLA_NOPE:]
    z64 = jnp.zeros((MLA_Q_LORA, MLA_HEADS, 64), F32)
    wq = jnp.concatenate([nope, pe, z64], axis=2).reshape(MLA_Q_LORA, MLA_HEADS * 256)
    rot = jnp.concatenate([-pe[:, :, 32:], pe[:, :, :32], z64], axis=2).reshape(MLA_Q_LORA, MLA_HEADS * 128)
    rpad = ((0, CQ_PAD - MLA_Q_LORA), (0, 0))
    wq = jnp.transpose(jnp.pad(wq, rpad)).astype(BF16)
    rot = jnp.transpose(jnp.pad(rot, rpad)).astype(BF16)
    wkv3 = jnp.pad(w_ukv, ((0, CKV_PAD - MLA_KV_LORA), (0, 0))).reshape(CKV_PAD, MLA_HEADS, MLA_NOPE + MLA_V)
    wk = wkv3[:, :, :MLA_NOPE].reshape(CKV_PAD, MLA_HEADS * MLA_NOPE).astype(BF16)
    wvt = jnp.transpose(wkv3[:, :, MLA_NOPE:], (1, 2, 0)).reshape(MLA_HEADS * MLA_V, CKV_PAD).astype(BF16)
    qn = jnp.pad(q_norm, (0, CQ_PAD - MLA_Q_LORA)).reshape(1, CQ_PAD)
    kvn = jnp.pad(kv_norm, (0, CKV_PAD - MLA_KV_LORA)).reshape(1, CKV_PAD)
    return wq, rot, wk, wvt, qn, kvn


def _rope_tables128(S):
    pos = jnp.arange(S, dtype=F32)
    inv = 1.0 / (ROPE_THETA ** (jnp.arange(0, MLA_ROPE, 2, dtype=F32) / MLA_ROPE))
    ang = pos[:, None] * inv[None, :]
    z = jnp.zeros((S, 64), F32)
    cos, sin = jnp.cos(ang), jnp.sin(ang)
    return jnp.concatenate([cos, cos, z], axis=1), jnp.concatenate([sin, sin, z], axis=1)


def kernel(x, w_in, b_gate, w_uq, q_norm, w_ukv, kv_norm, na_rpb, w_branch, w_o, norm_mix, norm_moe,
           w_router, w_exp_gate, w_exp_up, w_exp_down, norm_final):
    B, S, D = x.shape
    T = B * S
    depth = w_in.shape[0]
    cos128, sin128 = _rope_tables128(S)
    cos_t, sin_t = jnp.transpose(cos128), jnp.transpose(sin128)
    w_cs, c_tab, sn_tab = _dft_tables(S)
    h = x.reshape(T, D)
    for l in range(depth):
        w_all, b_all = _prep_in_weights(w_in[l], b_gate[l])
        proj = _inproj(h, norm_mix[l].reshape(1, D), w_all, b_all)
        y_na = _na_attention(proj, _na_bias_tables(na_rpb[l]), B, S)
        wq, wrot, wk, wvt, qn, kvn = _prep_mla_weights(w_uq[l], q_norm[l], w_ukv[l], kv_norm[l])
        q, k, vt = _mla_up(proj, cos128, sin128, cos_t, sin_t, qn, kvn, wq, wrot, wk, wvt, B, S)
        y_mla = _mla_attn(q, k, vt, B, S)
        fa, fb = _fnet_channel(proj, w_cs, B, S)
        y_f = _fnet_position(c_tab, sn_tab, fa, fb, B, S).reshape(T, FNET_GROUPS * FNET_GROUP_DIM)
        merged = _merge(y_na, y_mla, y_f, w_branch, l, proj)
        h = _outproj(merged, w_o, l, h)
        final_gain = norm_final.reshape(1, D) if l == depth - 1 else None
        h = _moe_layer(h, norm_moe[l].reshape(1, D), w_router[l], w_exp_gate, w_exp_up, w_exp_down, l, final_gain,
                       B, S)
    return h.reshape(B, S, D)
```

```python
import functools

import numpy as np
import jax
import jax.numpy as jnp
from jax import lax
from jax.experimental import pallas as pl
from jax.experimental.pallas import tpu as pltpu

D_MODEL = 2048
GRID_W = 64
NA_HEADS = 16
NA_HEAD_DIM = 64
NA_WIDTH = NA_HEADS * NA_HEAD_DIM
NA_KH = 8
NA_KW = 16
MLA_HEADS = 8
MLA_NOPE = 128
MLA_ROPE = 64
MLA_V = 128
MLA_Q_LORA = 448
MLA_KV_LORA = 160
ROPE_THETA = 10000.0
FNET_GROUPS = 4
FNET_GROUP_DIM = 256
N_BRANCHES = 3
BRANCH_WIDTH = 1024
N_EXPERTS = 16
EXPERT_FF = 2048
CAPACITY_FACTOR = 2
RMS_EPS = 1e-6
NEG_INF = -1e30

F32 = jnp.float32
BF16 = jnp.bfloat16
I32 = jnp.int32

COL_QKV = 0
COL_UF = 3072
COL_LAT = 4096
COL_GATE = 5120
PROJ_COLS = COL_GATE + N_BRANCHES * D_MODEL
LAT_W = 1024
CQ_PAD = 512
CKV_PAD = 256
VT_ROWS = MLA_V + 16
LOG2E = 1.4426950408889634
TOK_BITS = 12
PFX_BLK = 512
COMB_TT = 256

VMEM_LIMIT = 56 * 1024 * 1024


def _cparams(*sem, **kw):
    return pltpu.CompilerParams(dimension_semantics=sem, vmem_limit_bytes=VMEM_LIMIT, **kw)


def _inproj_kernel(x_ref, g_ref, w_ref, b_ref, o_ref, xn_ref, *, gate_tile0):
    j = pl.program_id(1)

    @pl.when(j == 0)
    def _():
        x = x_ref[...]
        ms = jnp.mean(x * x, axis=-1, keepdims=True)
        xn_ref[...] = (x * lax.rsqrt(ms + RMS_EPS) * g_ref[...]).astype(BF16)

    acc = jnp.dot(xn_ref[...], w_ref[...], preferred_element_type=F32)

    @pl.when(j < gate_tile0)
    def _():
        o_ref[...] = acc.astype(o_ref.dtype)

    @pl.when(j >= gate_tile0)
    def _():
        z = acc + b_ref[...]
        o_ref[...] = (0.5 * jnp.tanh(0.5 * z) + 0.5).astype(o_ref.dtype)


def _inproj(h2, gain, w_all, b_all, tm=1024, tn=1024):
    T = h2.shape[0]
    return pl.pallas_call(
        functools.partial(_inproj_kernel, gate_tile0=COL_GATE // tn),
        out_shape=jax.ShapeDtypeStruct((T, PROJ_COLS), BF16),
        grid=(T // tm, PROJ_COLS // tn),
        in_specs=[
            pl.BlockSpec((tm, D_MODEL), lambda i, j: (i, 0)),
            pl.BlockSpec((1, D_MODEL), lambda i, j: (0, 0)),
            pl.BlockSpec((D_MODEL, tn), lambda i, j: (0, j)),
            pl.BlockSpec((1, tn), lambda i, j: (0, j)),
        ],
        out_specs=pl.BlockSpec((tm, tn), lambda i, j: (i, j)),
        scratch_shapes=[pltpu.VMEM((tm, D_MODEL), BF16)],
        compiler_params=_cparams("parallel", "arbitrary"),
        name="inproj",
    )(h2, gain, w_all, b_all)


def _na_kernel(q_ref, k_ref, v_ref, bias_a_ref, bias_b_ref, o_ref, *, rows):
    i = pl.program_id(1)
    nk = NA_KH * GRID_W
    lo = lax.broadcasted_iota(jnp.int32, (GRID_W, 128), 1) < NA_HEAD_DIM
    nt = (((1,), (1,)), ((), ()))
    row0 = []
    for rr in range(2):
        start = jnp.clip(2 * i + rr - NA_KH // 2, 0, rows - NA_KH)
        row0.append(pl.multiple_of(start * GRID_W, GRID_W))

    def scores(rr, hp):
        sl = slice(hp * 128, (hp + 1) * 128)
        q2 = q_ref[rr * GRID_W:(rr + 1) * GRID_W, sl]
        zero = jnp.zeros_like(q2)
        qq = jnp.concatenate([jnp.where(lo, q2, zero), jnp.where(lo, zero, q2)], axis=0)
        return lax.dot_general(qq, k_ref[pl.ds(row0[rr], nk), sl], nt, preferred_element_type=F32)

    chains = [(rr, hp) for hp in range(NA_HEADS // 2) for rr in range(2)]
    s_next = scores(*chains[0])
    for n, (rr, hp) in enumerate(chains):
        s = s_next
        if n + 1 < len(chains):
            s_next = scores(*chains[n + 1])
        sl = slice(hp * 128, (hp + 1) * 128)
        bias_ref = bias_a_ref if rr == 0 else bias_b_ref
        s = s * (NA_HEAD_DIM ** -0.5 * LOG2E) + bias_ref[0, hp]
        m = jnp.max(s, axis=-1, keepdims=True)
        p = jnp.exp2(s - m)
        l = jnp.sum(p, axis=-1, keepdims=True)
        o = jnp.dot(p.astype(BF16), v_ref[pl.ds(row0[rr], nk), sl], preferred_element_type=F32) / l
        o_ref[rr * GRID_W:(rr + 1) * GRID_W, sl] = jnp.where(lo, o[:GRID_W], o[GRID_W:]).astype(o_ref.dtype)


def _na_bias_kernel(toep_ref, o_ref):
    t = pl.program_id(0)
    hp = pl.program_id(1)
    for a in range(2):
        for i in range(NA_KH):
            o_ref[0, 0, a * GRID_W:(a + 1) * GRID_W, i * GRID_W:(i + 1) * GRID_W] = (
                toep_ref[2 * hp + a, i - t + NA_KH - 1])


def _na_bias_tables(rpb):
    cols = np.arange(GRID_W)
    col_start = np.clip(cols - NA_KW // 2, 0, GRID_W - NA_KW)
    col_valid = (cols[None, :] >= col_start[:, None]) & (cols[None, :] < col_start[:, None] + NA_KW)
    col_idx = np.clip(cols[None, :] - cols[:, None] + NA_KW - 1, 0, 2 * NA_KW - 2)
    onehot = (col_idx[None] == np.arange(2 * NA_KW - 1)[:, None, None]).astype(np.float32)
    toep = jnp.einsum("hdc,cqj->hdqj", rpb.astype(F32), jnp.asarray(onehot), precision=lax.Precision.HIGHEST)
    toep = jnp.where(col_valid[None, None], toep * LOG2E, NEG_INF * LOG2E)
    return pl.pallas_call(
        _na_bias_kernel,
        out_shape=jax.ShapeDtypeStruct((NA_KH, NA_HEADS // 2, 2 * GRID_W, NA_KH * GRID_W), F32),
        grid=(NA_KH, NA_HEADS // 2),
        in_specs=[pl.BlockSpec((NA_HEADS, 2 * NA_KH - 1, GRID_W, GRID_W), lambda t, hp: (0, 0, 0, 0))],
        out_specs=pl.BlockSpec((1, 1, 2 * GRID_W, NA_KH * GRID_W), lambda t, hp: (t, hp, 0, 0)),
        compiler_params=_cparams("parallel", "parallel"),
        name="na_bias",
    )(toep)


def _na_attention(proj, bias, B, S):
    rows = S // GRID_W
    assert rows >= NA_KH and rows % 2 == 0
    T = B * S
    half = rows // 2

    def bias_spec(rr):
        def bias_map(b, i):
            r = 2 * i + rr
            return (r - jnp.clip(r - NA_KH // 2, 0, rows - NA_KH), 0, 0, 0)
        return pl.BlockSpec((1, NA_HEADS // 2, 2 * GRID_W, NA_KH * GRID_W), bias_map)

    return pl.pallas_call(
        functools.partial(_na_kernel, rows=rows),
        out_shape=jax.ShapeDtypeStruct((T, NA_WIDTH), BF16),
        grid=(B, half),
        in_specs=[
            pl.BlockSpec((2 * GRID_W, NA_WIDTH), lambda b, i: (b * half + i, 0)),
            pl.BlockSpec((S, NA_WIDTH), lambda b, i: (b, 1)),
            pl.BlockSpec((S, NA_WIDTH), lambda b, i: (b, 2)),
            bias_spec(0), bias_spec(1),
        ],
        out_specs=pl.BlockSpec((2 * GRID_W, NA_WIDTH), lambda b, i: (b * half + i, 0)),
        compiler_params=_cparams("parallel", "arbitrary"),
        name="na_attn",
    )(proj, proj, proj, bias, bias)


def _mla_up_kernel(lat_ref, cos_ref, sin_ref, cost_ref, sint_ref, qn_ref, kvn_ref, wq_ref, wrot_ref, wk_ref,
                   wvt_ref, q_ref, k_ref, vt_ref):
    cq = lat_ref[:, 0:CQ_PAD].astype(F32)
    ms = jnp.sum(cq * cq, axis=-1, keepdims=True) * (1.0 / MLA_Q_LORA)
    xq = (cq * lax.rsqrt(ms + RMS_EPS) * qn_ref[...]).astype(BF16)
    ckv = lat_ref[:, CQ_PAD:CQ_PAD + CKV_PAD].astype(F32)
    ms2 = jnp.sum(ckv * ckv, axis=-1, keepdims=True) * (1.0 / MLA_KV_LORA)
    xkv = (ckv * lax.rsqrt(ms2 + RMS_EPS) * kvn_ref[...]).astype(BF16)
    cos = cos_ref[...]
    sin = sin_ref[...]
    kpe = (lat_ref[:, 768:896].astype(F32) * cos + lat_ref[:, 896:1024].astype(F32) * sin).astype(BF16)
    scale = (MLA_NOPE + MLA_ROPE) ** -0.5 * LOG2E
    ones = jnp.ones((VT_ROWS - MLA_V, lat_ref.shape[0]), BF16)
    nt = (((1,), (1,)), ((), ()))
    for h in range(MLA_HEADS):
        a = lax.dot_general(wq_ref[h * 256:(h + 1) * 256, :], xq, nt, preferred_element_type=F32)
        rt = lax.dot_general(wrot_ref[h * 128:(h + 1) * 128, :], xq, nt, preferred_element_type=F32)
        q_ref[0, h, 0:128, :] = (a[0:128] * scale).astype(BF16)
        q_ref[0, h, 128:256, :] = ((a[128:256] * cost_ref[...] + rt * sint_ref[...]) * scale).astype(BF16)
        kn = jnp.dot(xkv, wk_ref[:, h * 128:(h + 1) * 128], preferred_element_type=F32)
        k_ref[0, h, :, 0:128] = kn.astype(BF16)
        k_ref[0, h, :, 128:256] = kpe
        vt = lax.dot_general(wvt_ref[h * 128:(h + 1) * 128, :], xkv, nt, preferred_element_type=F32)
        vt_ref[0, h, 0:MLA_V, :] = vt.astype(BF16)
        vt_ref[0, h, MLA_V:VT_ROWS, :] = ones


def _mla_up(proj, cos128, sin128, cos_t, sin_t, qn, kvn, wq, wrot, wk, wvt, B, S, tm=512):
    nt = S // tm
    lat_blk = COL_LAT // LAT_W
    const = lambda b, i: (0, 0)
    return pl.pallas_call(
        _mla_up_kernel,
        out_shape=(
            jax.ShapeDtypeStruct((B, MLA_HEADS, 256, S), BF16),
            jax.ShapeDtypeStruct((B, MLA_HEADS, S, 256), BF16),
            jax.ShapeDtypeStruct((B, MLA_HEADS, VT_ROWS, S), BF16),
        ),
        grid=(B, nt),
        in_specs=[
            pl.BlockSpec((tm, LAT_W), lambda b, i: (b * nt + i, lat_blk)),
            pl.BlockSpec((tm, 128), lambda b, i: (i, 0)),
            pl.BlockSpec((tm, 128), lambda b, i: (i, 0)),
            pl.BlockSpec((128, tm), lambda b, i: (0, i)),
            pl.BlockSpec((128, tm), lambda b, i: (0, i)),
            pl.BlockSpec((1, CQ_PAD), const),
            pl.BlockSpec((1, CKV_PAD), const),
            pl.BlockSpec((MLA_HEADS * 256, CQ_PAD), const),
            pl.BlockSpec((MLA_HEADS * 128, CQ_PAD), const),
            pl.BlockSpec((CKV_PAD, MLA_HEADS * MLA_NOPE), const),
            pl.BlockSpec((MLA_HEADS * MLA_V, CKV_PAD), const),
        ],
        out_specs=(
            pl.BlockSpec((1, MLA_HEADS, 256, tm), lambda b, i: (b, 0, 0, i)),
            pl.BlockSpec((1, MLA_HEADS, tm, 256), lambda b, i: (b, 0, i, 0)),
            pl.BlockSpec((1, MLA_HEADS, VT_ROWS, tm), lambda b, i: (b, 0, 0, i)),
        ),
        compiler_params=_cparams("parallel", "parallel"),
        name="mla_up",
    )(proj, cos128, sin128, cos_t, sin_t, qn, kvn, wq, wrot, wk, wvt)


def _mla_attn_kernel(q_ref, k_ref, vt_ref, o_ref, *, ck):
    qt = q_ref[0, 0]
    n_chunks = k_ref.shape[2] // ck
    m = acc = None

    def scores(c):
        return jnp.dot(k_ref[0, 0, c * ck:(c + 1) * ck, :], qt, preferred_element_type=F32)

    s_next = scores(0)
    for c in range(n_chunks):
        s = s_next
        if c + 1 < n_chunks:
            s_next = scores(c + 1)
        mc = jnp.max(s, axis=0, keepdims=True)
        m_new = mc if c == 0 else jnp.maximum(m, mc)
        p = jnp.exp2(s - m_new).astype(BF16)
        pv = jnp.dot(vt_ref[0, 0, :, c * ck:(c + 1) * ck], p, preferred_element_type=F32)
        acc = pv if c == 0 else acc * jnp.exp2(m - m_new) + pv
        m = m_new
    o = acc[0:MLA_V] / acc[MLA_V:MLA_V + 1]
    o_ref[...] = o.T.astype(o_ref.dtype)


def _mla_attn(q, k, vt, B, S, tq=1024, ck=512):
    nq = S // tq
    return pl.pallas_call(
        functools.partial(_mla_attn_kernel, ck=ck),
        out_shape=jax.ShapeDtypeStruct((B * S, MLA_HEADS * MLA_V), BF16),
        grid=(B, MLA_HEADS, nq),
        in_specs=[
            pl.BlockSpec((1, 1, 256, tq), lambda b, h, i: (b, h, 0, i)),
            pl.BlockSpec((1, 1, S, 256), lambda b, h, i: (b, h, 0, 0)),
            pl.BlockSpec((1, 1, VT_ROWS, S), lambda b, h, i: (b, h, 0, 0)),
        ],
        out_specs=pl.BlockSpec((tq, MLA_V), lambda b, h, i: (b * nq + i, h)),
        compiler_params=_cparams("parallel", "parallel", "arbitrary"),
        name="mla_attn",
    )(q, k, vt)


def _fnet_ch_kernel(lo_ref, hi_ref, w_ref, a_ref, b_ref):
    gd = FNET_GROUP_DIM
    lo = lo_ref[...].astype(F32)
    hi = hi_ref[...].astype(F32)
    for p, u in enumerate(((lo + hi).astype(BF16), (lo - hi).astype(BF16))):
        for g in range(FNET_GROUPS):
            ab = jnp.dot(u[:, g * gd:(g + 1) * gd], w_ref[...], preferred_element_type=F32)
            a_ref[p, :, g * gd:(g + 1) * gd] = ab[:, :gd].astype(a_ref.dtype)
            b_ref[p, :, g * gd:(g + 1) * gd] = ab[:, gd:].astype(b_ref.dtype)


def _fnet_channel(proj, w_cs, B, S, tm=1024):
    W = FNET_GROUPS * FNET_GROUP_DIM
    half = S // 2
    tm = min(tm, half)
    nh = half // tm
    out = jax.ShapeDtypeStruct((2, B * half, W), BF16)
    return pl.pallas_call(
        _fnet_ch_kernel,
        out_shape=(out, out),
        grid=(B, nh),
        in_specs=[
            pl.BlockSpec((tm, W), lambda b, i: (b * 2 * nh + i, COL_UF // W)),
            pl.BlockSpec((tm, W), lambda b, i: (b * 2 * nh + nh + i, COL_UF // W)),
            pl.BlockSpec((FNET_GROUP_DIM, 2 * FNET_GROUP_DIM), lambda b, i: (0, 0)),
        ],
        out_specs=(pl.BlockSpec((2, tm, W), lambda b, i: (0, b * nh + i, 0)),
                   pl.BlockSpec((2, tm, W), lambda b, i: (0, b * nh + i, 0))),
        compiler_params=_cparams("parallel", "parallel"),
        name="fnet_channel",
    )(proj, proj, w_cs)


def _fnet_pos_kernel(c_ref, s_ref, a_ref, b_ref, o_ref, *, scale):
    y = (jnp.dot(c_ref[0], a_ref[0], preferred_element_type=F32)
         + jnp.dot(s_ref[0], b_ref[0], preferred_element_type=F32))
    o_ref[...] = (y * scale).astype(o_ref.dtype)


def _fnet_position(c_tab, sn_tab, a, b, B, S, tm=1024, tn=512):
    W = FNET_GROUPS * FNET_GROUP_DIM
    half = S // 2
    tm = min(tm, half)
    nm = half // tm
    nn = W // tn
    scale = float((S * FNET_GROUP_DIM) ** -0.5)
    return pl.pallas_call(
        functools.partial(_fnet_pos_kernel, scale=scale),
        out_shape=jax.ShapeDtypeStruct((B * half, 2 * W), BF16),
        grid=(2, nm, B, nn),
        in_specs=[
            pl.BlockSpec((1, tm, half), lambda p, m, bb, n: (p, m, 0)),
            pl.BlockSpec((1, tm, half), lambda p, m, bb, n: (p, m, 0)),
            pl.BlockSpec((1, half, tn), lambda p, m, bb, n: (p, bb, n)),
            pl.BlockSpec((1, half, tn), lambda p, m, bb, n: (p, bb, n)),
        ],
        out_specs=pl.BlockSpec((tm, tn), lambda p, m, bb, n: (bb * nm + m, p * nn + n)),
        compiler_params=_cparams("parallel", "parallel", "parallel", "parallel"),
        name="fnet_position",
    )(c_tab, sn_tab, a, b)


def _dft_table_kernel(chi_ref, shi_ref, clo_ref, slo_ref, cos_ref, nsin_ref, *, hb, nlo):
    cl, sl = clo_ref[0], slo_ref[0]
    for u in range(hb):
        ch, sh = chi_ref[u:u + 1, :], shi_ref[u:u + 1, :]
        cos_ref[0, u * nlo:(u + 1) * nlo, :] = (ch * cl - sh * sl).astype(cos_ref.dtype)
        nsin_ref[0, u * nlo:(u + 1) * nlo, :] = (-(sh * cl + ch * sl)).astype(nsin_ref.dtype)


def _dft_tables(S):
    gd = FNET_GROUP_DIM
    ck = (np.arange(gd)[:, None] * np.arange(gd)[None, :]) % gd
    ang = 2.0 * np.pi * ck / gd
    w_cs = jnp.asarray(np.concatenate([np.cos(ang), np.sin(ang)], axis=1), F32).astype(BF16)
    kb = 64
    half = S // 2
    n = lax.broadcasted_iota(jnp.int32, (1, half), 1)

    def thin(rows, period):
        ang_ = ((lax.broadcasted_iota(jnp.int32, (rows, 1), 0) * n) % period).astype(F32) * (2.0 * np.pi / period)
        return jnp.cos(ang_), jnp.sin(ang_)

    c_hi, s_hi = thin(S // kb, S // kb)
    c_lo, s_lo = thin(kb, S)
    split = lambda t: jnp.stack([t[0::2], t[1::2]])
    c_lo, s_lo = split(c_lo), split(s_lo)
    nlo = kb // 2
    hb = 8
    assert (S // kb) % hb == 0
    tab = jax.ShapeDtypeStruct((2, half, half), BF16)
    cos_t, nsin_t = pl.pallas_call(
        functools.partial(_dft_table_kernel, hb=hb, nlo=nlo),
        out_shape=(tab, tab),
        grid=(2, S // kb // hb),
        in_specs=[pl.BlockSpec((hb, half), lambda p, g: (g, 0)), pl.BlockSpec((hb, half), lambda p, g: (g, 0)),
                  pl.BlockSpec((1, nlo, half), lambda p, g: (p, 0, 0)),
                  pl.BlockSpec((1, nlo, half), lambda p, g: (p, 0, 0))],
        out_specs=(pl.BlockSpec((1, hb * nlo, half), lambda p, g: (p, g, 0)),
                   pl.BlockSpec((1, hb * nlo, half), lambda p, g: (p, g, 0))),
        compiler_params=_cparams("parallel", "parallel"),
        name="dft_tables",
    )(c_hi, s_hi, c_lo, s_lo)
    return w_cs, cos_t, nsin_t


def _merge_kernel(yn_ref, ym_ref, yf_ref, w_ref, g0_ref, g1_ref, g2_ref, o_ref, wbf_ref):
    @pl.when(pl.program_id(1) == 0)
    def _():
        wbf_ref[...] = w_ref[0].astype(BF16)

    acc = g0_ref[...].astype(F32) * jnp.dot(yn_ref[...], wbf_ref[0], preferred_element_type=F32)
    acc += g1_ref[...].astype(F32) * jnp.dot(ym_ref[...], wbf_ref[1], preferred_element_type=F32)
    acc += g2_ref[...].astype(F32) * jnp.dot(yf_ref[...], wbf_ref[2], preferred_element_type=F32)
    o_ref[...] = acc.astype(o_ref.dtype)


def _merge(y_na, y_mla, y_f, w_branch, layer, proj, tm=1024, tn=512):
    T = y_na.shape[0]
    ybs = pl.BlockSpec((tm, BRANCH_WIDTH), lambda j, i: (i, 0))

    def gate_spec(br):
        off = (COL_GATE + br * D_MODEL) // tn
        return pl.BlockSpec((tm, tn), lambda j, i: (i, off + j))

    return pl.pallas_call(
        _merge_kernel,
        out_shape=jax.ShapeDtypeStruct((T, D_MODEL), BF16),
        grid=(D_MODEL // tn, T // tm),
        in_specs=[ybs, ybs, ybs,
                  pl.BlockSpec((1, N_BRANCHES, BRANCH_WIDTH, tn), lambda j, i: (layer, 0, 0, j)),
                  gate_spec(0), gate_spec(1), gate_spec(2)],
        out_specs=pl.BlockSpec((tm, tn), lambda j, i: (i, j)),
        scratch_shapes=[pltpu.VMEM((N_BRANCHES, BRANCH_WIDTH, tn), BF16)],
        compiler_params=_cparams("parallel", "arbitrary"),
        name="merge",
    )(y_na, y_mla, y_f, w_branch, proj, proj, proj)


def _outproj_kernel(m_ref, w_ref, h_ref, o_ref, wbf_ref):
    @pl.when(pl.program_id(1) == 0)
    def _():
        wbf_ref[...] = w_ref[0].astype(BF16)

    o_ref[...] = h_ref[...] + jnp.dot(m_ref[...], wbf_ref[...], preferred_element_type=F32)


def _outproj(merged, w_o, layer, h2, tm=1024, tn=512):
    T = merged.shape[0]
    return pl.pallas_call(
        _outproj_kernel,
        out_shape=jax.ShapeDtypeStruct((T, D_MODEL), F32),
        grid=(D_MODEL // tn, T // tm),
        in_specs=[
            pl.BlockSpec((tm, D_MODEL), lambda j, i: (i, 0)),
            pl.BlockSpec((1, D_MODEL, tn), lambda j, i: (layer, 0, j)),
            pl.BlockSpec((tm, tn), lambda j, i: (i, j)),
        ],
        out_specs=pl.BlockSpec((tm, tn), lambda j, i: (i, j)),
        scratch_shapes=[pltpu.VMEM((D_MODEL, tn), BF16)],
        compiler_params=_cparams("parallel", "arbitrary"),
        name="outproj",
    )(merged, w_o, h2)


def _router_kernel(x_ref, g_ref, wrt_ref, aff_ref):
    x = x_ref[...]
    ms = jnp.mean(x * x, axis=-1, keepdims=True)
    xn = x * lax.rsqrt(ms + RMS_EPS) * g_ref[...]
    logits = lax.dot_general(wrt_ref[...], xn, (((1,), (1,)), ((), ())), preferred_element_type=F32,
                             precision=lax.Precision.HIGHEST)
    m = jnp.max(logits, axis=0, keepdims=True)
    e = jnp.exp(logits - m)
    aff_ref[0] = e / jnp.sum(e, axis=0, keepdims=True)


def _router(h2, gain, w_router_t, B, S, tm=1024):
    nt = S // tm
    return pl.pallas_call(
        _router_kernel,
        out_shape=jax.ShapeDtypeStruct((B, N_EXPERTS, S), F32),
        grid=(B, nt),
        in_specs=[
            pl.BlockSpec((tm, D_MODEL), lambda b, i: (b * nt + i, 0)),
            pl.BlockSpec((1, D_MODEL), lambda b, i: (0, 0)),
            pl.BlockSpec((N_EXPERTS, D_MODEL), lambda b, i: (0, 0)),
        ],
        out_specs=pl.BlockSpec((1, N_EXPERTS, tm), lambda b, i: (b, 0, i)),
        compiler_params=_cparams("parallel", "parallel"),
        name="router",
    )(h2, gain, w_router_t)


def _route_kernel(a_ref, val_ref, gate_ref, starts_ref, kmax_ref, pos_s, pv_s, a_s, *, cap, tt):
    E, S = a_ref.shape[1], a_ref.shape[2]
    a = a_ref[0]

    def as_float(b):
        return lax.bitcast_convert_type(b, F32)

    def bisect(_, carry):
        lo, hi = carry
        mid = lo + ((hi - lo + 1) >> 1)
        cnt = jnp.sum(jnp.where(a >= as_float(mid), 1.0, 0.0), axis=1, keepdims=True)
        ge = cnt >= cap
        return jnp.where(ge, mid, lo), jnp.where(ge, hi, mid - 1)

    lo0 = jnp.zeros((E, 1), I32)
    hi0 = jnp.full((E, 1), 0x7F7FFFFF, I32)
    thr_bits, _ = lax.fori_loop(0, 31, bisect, (lo0, hi0))
    thr, thr_up = as_float(thr_bits), as_float(thr_bits + 1)

    r_i = lax.broadcasted_iota(I32, (PFX_BLK, PFX_BLK), 0)
    c_i = lax.broadcasted_iota(I32, (PFX_BLK, PFX_BLK), 1)
    tri = jnp.where(r_i < c_i, 1.0, 0.0).astype(BF16)

    def excl_prefix(mask):
        x = jnp.where(mask, 1.0, 0.0).astype(BF16)
        carry = jnp.zeros((E, 1), F32)
        outs = []
        for j in range(S // PFX_BLK):
            blk = x[:, j * PFX_BLK:(j + 1) * PFX_BLK]
            outs.append(jnp.dot(blk, tri, preferred_element_type=F32) + carry)
            carry = carry + jnp.sum(blk.astype(F32), axis=1, keepdims=True)
        return jnp.concatenate(outs, axis=1)

    gt = a >= thr_up
    eq = (a >= thr) & (a < thr_up)
    need = cap - jnp.sum(jnp.where(gt, 1.0, 0.0), axis=1, keepdims=True)
    sel = gt | (eq & (excl_prefix(eq) < need))
    pos = excl_prefix(sel)

    self_ = jnp.where(sel, 1.0, 0.0)
    e_r = lax.broadcasted_iota(I32, (E, E), 0)
    e_c = lax.broadcasted_iota(I32, (E, E), 1)
    low = jnp.where(e_c < e_r, 1.0, 0.0).astype(BF16)
    rank = jnp.dot(low, self_.astype(BF16), preferred_element_type=F32)
    tok = lax.broadcasted_iota(I32, (E, S), 1)
    pos_s[...] = jnp.where(sel, pos.astype(I32), -1)
    pv_s[...] = (tok + (rank.astype(I32) << TOK_BITS)).astype(F32)
    a_s[...] = a

    lane = lax.broadcasted_iota(I32, (E, 128), 1)
    starts = jnp.full((E, 128), cap, I32)
    kcount = jnp.sum(self_, axis=0, keepdims=True)
    kmax = jnp.zeros((8, 128), I32)
    lane8 = lax.broadcasted_iota(I32, (8, 128), 1)
    posi = pos.astype(I32)
    for j in range(S // tt):
        starts = jnp.where(lane == j, posi[:, j * tt:j * tt + 1], starts)
        kj = jnp.max(kcount[:, j * tt:(j + 1) * tt], axis=1, keepdims=True).astype(I32)
        kmax = jnp.where(lane8 == j, kj, kmax)
    starts_ref[0] = starts
    kmax_ref[0] = kmax

    c_iota = lax.broadcasted_iota(I32, (cap, PFX_BLK), 0)
    lane_c = lax.broadcasted_iota(I32, (cap, 128), 1)

    def per_expert(e, carry):
        val_acc, gate_acc = carry
        v_part = jnp.zeros((cap, PFX_BLK), F32)
        g_part = jnp.zeros((cap, PFX_BLK), F32)
        for j in range(S // PFX_BLK):
            sl = pl.ds(j * PFX_BLK, PFX_BLK)
            hit = pos_s[pl.ds(e, 1), sl] == c_iota
            v_part = v_part + jnp.where(hit, pv_s[pl.ds(e, 1), sl], 0.0)
            g_part = g_part + jnp.where(hit, a_s[pl.ds(e, 1), sl], 0.0)
        v = jnp.sum(v_part, axis=1, keepdims=True).astype(I32)
        g = jnp.sum(g_part, axis=1, keepdims=True)
        return jnp.where(lane_c == e, v, val_acc), jnp.where(lane_c == e, g, gate_acc)

    val, gate = lax.fori_loop(0, E, per_expert, (jnp.zeros((cap, 128), I32), jnp.zeros((cap, 128), F32)))
    val_ref[0] = val
    gate_ref[0] = gate


def _route(aff_t, cap, tt=COMB_TT):
    B, E, S = aff_t.shape
    assert S <= (1 << TOK_BITS) and S // tt < 128 and S % PFX_BLK == 0
    return pl.pallas_call(
        functools.partial(_route_kernel, cap=cap, tt=tt),
        out_shape=(
            jax.ShapeDtypeStruct((B, cap, 128), I32),
            jax.ShapeDtypeStruct((B, cap, 128), F32),
            jax.ShapeDtypeStruct((B, E, 128), I32),
            jax.ShapeDtypeStruct((B, 8, 128), I32),
        ),
        grid=(B,),
        in_specs=[pl.BlockSpec((1, E, S), lambda b: (b, 0, 0))],
        out_specs=(
            pl.BlockSpec((1, cap, 128), lambda b: (b, 0, 0)),
            pl.BlockSpec((1, cap, 128), lambda b: (b, 0, 0)),
            pl.BlockSpec((1, E, 128), lambda b: (b, 0, 0)),
            pl.BlockSpec((1, 8, 128), lambda b: (b, 0, 0)),
        ),
        scratch_shapes=[pltpu.VMEM((E, S), I32), pltpu.VMEM((E, S), F32), pltpu.VMEM((E, S), F32)],
        compiler_params=_cparams("parallel"),
        name="route",
    )(aff_t)


def _gather_kernel(rows_ref, h_hbm, g_ref, o_ref, buf, sem, *, gt):
    i = pl.program_id(0)
    slot = i % 2

    def issue_tile(t, s):
        def issue(r, _):
            pltpu.make_async_copy(h_hbm.at[pl.ds(rows_ref[t * gt + r], 1)], buf.at[s, pl.ds(r, 1)], sem.at[s]).start()
            return 0

        lax.fori_loop(0, gt, issue, 0, unroll=8)

    @pl.when(i == 0)
    def _():
        issue_tile(0, 0)

    @pl.when(i + 1 < pl.num_programs(0))
    def _():
        issue_tile(i + 1, 1 - slot)

    pltpu.make_async_copy(h_hbm.at[pl.ds(0, gt)], buf.at[slot], sem.at[slot]).wait()
    x = buf[slot]
    ms = jnp.mean(x * x, axis=-1, keepdims=True)
    o_ref[...] = (x * lax.rsqrt(ms + RMS_EPS) * g_ref[...]).astype(o_ref.dtype)


def _gather_norm(rows, h2, gain, gt=1024):
    n = rows.shape[0]
    gt = min(gt, n)
    return pl.pallas_call(
        functools.partial(_gather_kernel, gt=gt),
        out_shape=jax.ShapeDtypeStruct((n, D_MODEL), BF16),
        grid_spec=pltpu.PrefetchScalarGridSpec(
            num_scalar_prefetch=1,
            grid=(n // gt,),
            in_specs=[pl.BlockSpec(memory_space=pl.ANY), pl.BlockSpec((1, D_MODEL), lambda i, rows: (0, 0))],
            out_specs=pl.BlockSpec((gt, D_MODEL), lambda i, rows: (i, 0)),
            scratch_shapes=[pltpu.VMEM((2, gt, D_MODEL), F32), pltpu.SemaphoreType.DMA((2,))],
        ),
        compiler_params=_cparams("arbitrary", disable_bounds_checks=True),
        name="gather_norm",
    )(rows, h2, gain)


def _combine_kernel(dst_ref, starts_ref, kmax_ref, h_ref, ye_hbm, fg_ref, o_ref, planes, sem, *,
                    B, E, cap, tt, nt_pad, final_norm):
    b = pl.program_id(0)
    j = pl.program_id(1)
    k = kmax_ref[b * nt_pad + j]

    def zero(p, _):
        planes[pl.ds(pl.multiple_of(p * tt, tt), tt), :] = jnp.zeros((tt, D_MODEL), F32)
        return 0

    lax.fori_loop(0, k, zero, 0)

    def row_copy(slot, dst_row):
        return pltpu.make_async_copy(ye_hbm.at[pl.ds(slot, 1)], planes.at[pl.ds(dst_row, 1)], sem)

    def per_expert(e, n):
        sbase = (b * E + e) * nt_pad + j
        c0 = starts_ref[sbase]
        c1 = starts_ref[sbase + 1]
        lbase = (e * B + b) * cap

        def issue(c):
            row_copy(lbase + c, dst_ref[lbase + c]).start()

        def issue4(i, _):
            for u in range(4):
                issue(c0 + 4 * i + u)
            return 0

        def issue1(i, _):
            issue(c1 - 1 - i)
            return 0

        cnt = c1 - c0
        lax.fori_loop(0, cnt >> 2, issue4, 0)
        lax.fori_loop(0, cnt & 3, issue1, 0)
        return n + cnt

    n = lax.fori_loop(0, E, per_expert, 0)

    for bit in range((E * tt).bit_length()):
        @pl.when(((n >> bit) & 1) == 1)
        def _():
            rows = 1 << bit
            pltpu.make_async_copy(ye_hbm.at[pl.ds(0, rows)], planes.at[pl.ds(0, rows)], sem).wait()

    for cs in range(D_MODEL // 128):
        cols = slice(cs * 128, (cs + 1) * 128)

        def add(p, acc):
            return acc + planes[pl.ds(pl.multiple_of(p * tt, tt), tt), cols]

        o_ref[:, cols] = lax.fori_loop(0, k, add, h_ref[:, cols])

    if final_norm:
        x = o_ref[...]
        ms = jnp.mean(x * x, axis=-1, keepdims=True)
        o_ref[...] = x * lax.rsqrt(ms + RMS_EPS) * fg_ref[...]


def _combine(h2, ye, dst_rows, starts, kmax, final_gain, B, S, cap, tt=COMB_TT):
    E = N_EXPERTS
    nt = S // tt
    nt_pad = starts.shape[0] // (B * E)
    final_norm = final_gain is not None
    fg = final_gain if final_norm else jnp.ones((1, D_MODEL), F32)
    return pl.pallas_call(
        functools.partial(_combine_kernel, B=B, E=E, cap=cap, tt=tt, nt_pad=nt_pad, final_norm=final_norm),
        out_shape=jax.ShapeDtypeStruct(h2.shape, F32),
        grid_spec=pltpu.PrefetchScalarGridSpec(
            num_scalar_prefetch=3,
            grid=(B, nt),
            in_specs=[pl.BlockSpec((tt, D_MODEL), lambda b, j, *_: (b * nt + j, 0)),
                      pl.BlockSpec(memory_space=pl.ANY),
                      pl.BlockSpec((1, D_MODEL), lambda b, j, *_: (0, 0))],
            out_specs=pl.BlockSpec((tt, D_MODEL), lambda b, j, *_: (b * nt + j, 0)),
            scratch_shapes=[pltpu.VMEM((E * tt, D_MODEL), F32), pltpu.SemaphoreType.DMA(())],
        ),
        compiler_params=_cparams("arbitrary", "arbitrary", disable_bounds_checks=True),
        name="combine",
    )(dst_rows, starts, kmax, h2, ye, fg)


def _ffn_up_kernel(x_ref, wg_ref, wu_ref, o_ref):
    x = x_ref[...]
    a = jnp.dot(x, wg_ref[0, 0].astype(BF16), preferred_element_type=F32)
    u = jnp.dot(x, wu_ref[0, 0].astype(BF16), preferred_element_type=F32)
    o_ref[...] = (a * (1.0 / (1.0 + jnp.exp(-a))) * u).astype(o_ref.dtype)


def _ffn_down_kernel(hid_ref, wd_ref, gate_ref, o_ref):
    y = jnp.dot(hid_ref[...], wd_ref[0, 0].astype(BF16), preferred_element_type=F32)
    o_ref[...] = y * gate_ref[...]


def _experts(xe, wg, wu, wd, layer, gate, tf=256, tn=512):
    n_tok = xe.shape[0]
    per_e = n_tok // N_EXPERTS
    ff = wg.shape[-1]
    hid = pl.pallas_call(
        _ffn_up_kernel,
        out_shape=jax.ShapeDtypeStruct((n_tok, ff), BF16),
        grid=(N_EXPERTS, ff // tf),
        in_specs=[
            pl.BlockSpec((per_e, D_MODEL), lambda e, f: (e, 0)),
            pl.BlockSpec((1, 1, D_MODEL, tf), lambda e, f: (layer, e, 0, f)),
            pl.BlockSpec((1, 1, D_MODEL, tf), lambda e, f: (layer, e, 0, f)),
        ],
        out_specs=pl.BlockSpec((per_e, tf), lambda e, f: (e, f)),
        compiler_params=_cparams("parallel", "arbitrary"),
        name="ffn_up",
    )(xe, wg, wu)
    return pl.pallas_call(
        _ffn_down_kernel,
        out_shape=jax.ShapeDtypeStruct((n_tok, D_MODEL), F32),
        grid=(N_EXPERTS, D_MODEL // tn),
        in_specs=[
            pl.BlockSpec((per_e, ff), lambda e, n: (e, 0)),
            pl.BlockSpec((1, 1, ff, tn), lambda e, n: (layer, e, 0, n)),
            pl.BlockSpec((per_e, 1), lambda e, n: (e, 0)),
        ],
        out_specs=pl.BlockSpec((per_e, tn), lambda e, n: (e, n)),
        compiler_params=_cparams("parallel", "arbitrary"),
        name="ffn_down",
    )(hid, wd, gate)


def _moe_layer(h2, gain, w_router, wg, wu, wd, layer, final_gain, B, S):
    E = N_EXPERTS
    cap = CAPACITY_FACTOR * S // E
    nt = S // COMB_TT
    aff_t = _router(h2, gain, jnp.transpose(w_router), B, S)
    val, gate, starts, kmax = _route(aff_t, cap)
    val_e = jnp.transpose(val[:, :, :E], (2, 0, 1))
    gate_e = jnp.transpose(gate[:, :, :E], (2, 0, 1)).reshape(-1, 1)
    tok_e = val_e & ((1 << TOK_BITS) - 1)
    rows = (tok_e + (jnp.arange(B, dtype=I32) * S)[None, :, None]).reshape(-1)
    xe = _gather_norm(rows, h2, gain)
    ye = _experts(xe, wg, wu, wd, layer, gate_e)
    dst = ((val_e >> TOK_BITS) * COMB_TT + tok_e % COMB_TT).reshape(-1)
    starts_flat = starts[:, :, :nt + 1].reshape(-1)
    kmax_flat = kmax[:, 0, :nt + 1].reshape(-1)
    return _combine(h2, ye, dst, starts_flat, kmax_flat, final_gain, B, S, cap)


IN_OFFS = tuple(int(v) for v in np.cumsum([0, 3 * NA_WIDTH, MLA_Q_LORA, MLA_KV_LORA, MLA_ROPE,
                                             FNET_GROUPS * FNET_GROUP_DIM, N_BRANCHES * D_MODEL]))


def _win_relayout_kernel(w_ref, rot_ref, o_ref):
    o = IN_OFFS

    def src(a, b):
        return w_ref[0, :, a:b].astype(BF16)

    o_ref[:, COL_QKV:COL_QKV + 3 * NA_WIDTH] = src(o[0], o[1])
    o_ref[:, COL_UF:COL_UF + o[5] - o[4]] = src(o[4], o[5])
    o_ref[:, COL_LAT:COL_GATE] = jnp.zeros((o_ref.shape[0], LAT_W), BF16)
    o_ref[:, COL_LAT:COL_LAT + MLA_Q_LORA] = src(o[1], o[2])
    o_ref[:, COL_LAT + CQ_PAD:COL_LAT + CQ_PAD + MLA_KV_LORA] = src(o[2], o[3])
    kr0 = COL_LAT + CQ_PAD + CKV_PAD
    kr = src(o[3], o[4])
    o_ref[:, kr0:kr0 + MLA_ROPE] = kr
    o_ref[:, kr0 + 128:kr0 + 128 + MLA_ROPE] = jnp.dot(kr, rot_ref[...], preferred_element_type=F32).astype(BF16)
    o_ref[:, COL_GATE:PROJ_COLS] = src(o[5], o[6])


def _prep_in_weights(w_in_all, layer, b_gate, tr=256):
    half = MLA_ROPE // 2
    rot = np.zeros((MLA_ROPE, MLA_ROPE), np.float32)
    rot[np.arange(half) + half, np.arange(half)] = -1.0
    rot[np.arange(half), np.arange(half) + half] = 1.0
    w_all = pl.pallas_call(
        _win_relayout_kernel,
        out_shape=jax.ShapeDtypeStruct((D_MODEL, PROJ_COLS), BF16),
        grid=(D_MODEL // tr,),
        in_specs=[pl.BlockSpec((1, tr, IN_OFFS[-1]), lambda i: (layer, i, 0)),
                  pl.BlockSpec((MLA_ROPE, MLA_ROPE), lambda i: (0, 0))],
        out_specs=pl.BlockSpec((tr, PROJ_COLS), lambda i: (i, 0)),
        compiler_params=_cparams("parallel"),
        name="win_relayout",
    )(w_in_all, jnp.asarray(rot, BF16))
    b_all = jnp.concatenate([jnp.zeros((COL_GATE,), F32), b_gate]).reshape(1, PROJ_COLS)
    return w_all, b_all


def _prep_mla_weights(w_uq, q_norm, w_ukv, kv_norm):
    qd = MLA_NOPE + MLA_ROPE
    wq3 = w_uq.reshape(MLA_Q_LORA, MLA_HEADS, qd)
    nope, pe = wq3[:, :, :MLA_NOPE], wq3[:, :, MLA_NOPE:]
    z64 = jnp.zeros((MLA_Q_LORA, MLA_HEADS, 64), F32)
    wq = jnp.concatenate([nope, pe, z64], axis=2).reshape(MLA_Q_LORA, MLA_HEADS * 256)
    rot = jnp.concatenate([-pe[:, :, 32:], pe[:, :, :32], z64], axis=2).reshape(MLA_Q_LORA, MLA_HEADS * 128)
    rpad = ((0, CQ_PAD - MLA_Q_LORA), (0, 0))
    wq = jnp.transpose(jnp.pad(wq, rpad)).astype(BF16)
    rot = jnp.transpose(jnp.pad(rot, rpad)).astype(BF16)
    wkv3 = jnp.pad(w_ukv, ((0, CKV_PAD - MLA_KV_LORA), (0, 0))).reshape(CKV_PAD, MLA_HEADS, MLA_NOPE + MLA_V)
    wk = wkv3[:, :, :MLA_NOPE].reshape(CKV_PAD, MLA_HEADS * MLA_NOPE).astype(BF16)
    wvt = jnp.transpose(wkv3[:, :, MLA_NOPE:], (1, 2, 0)).reshape(MLA_HEADS * MLA_V, CKV_PAD).astype(BF16)
    qn = jnp.pad(q_norm, (0, CQ_PAD - MLA_Q_LORA)).reshape(1, CQ_PAD)
    kvn = jnp.pad(kv_norm, (0, CKV_PAD - MLA_KV_LORA)).reshape(1, CKV_PAD)
    return wq, rot, wk, wvt, qn, kvn


def _rope_tables128(S):
    pos = jnp.arange(S, dtype=F32)
    inv = 1.0 / (ROPE_THETA ** (jnp.arange(0, MLA_ROPE, 2, dtype=F32) / MLA_ROPE))
    ang = pos[:, None] * inv[None, :]
    z = jnp.zeros((S, 64), F32)
    cos, sin = jnp.cos(ang), jnp.sin(ang)
    return jnp.concatenate([cos, cos, z], axis=1), jnp.concatenate([sin, sin, z], axis=1)


def kernel(x, w_in, b_gate, w_uq, q_norm, w_ukv, kv_norm, na_rpb, w_branch, w_o, norm_mix, norm_moe,
           w_router, w_exp_gate, w_exp_up, w_exp_down, norm_final):
    B, S, D = x.shape
    T = B * S
    depth = w_in.shape[0]
    cos128, sin128 = _rope_tables128(S)
    cos_t, sin_t = jnp.transpose(cos128), jnp.transpose(sin128)
    w_cs, c_tab, sn_tab = _dft_tables(S)
    h = x.reshape(T, D)
    for l in range(depth):
        w_all, b_all = _prep_in_weights(w_in, l, b_gate[l])
        proj = _inproj(h, norm_mix[l].reshape(1, D), w_all, b_all)
        y_na = _na_attention(proj, _na_bias_tables(na_rpb[l]), B, S)
        wq, wrot, wk, wvt, qn, kvn = _prep_mla_weights(w_uq[l], q_norm[l], w_ukv[l], kv_norm[l])
        q, k, vt = _mla_up(proj, cos128, sin128, cos_t, sin_t, qn, kvn, wq, wrot, wk, wvt, B, S)
        y_mla = _mla_attn(q, k, vt, B, S)
        fa, fb = _fnet_channel(proj, w_cs, B, S)
        y_f = _fnet_position(c_tab, sn_tab, fa, fb, B, S).reshape(T, FNET_GROUPS * FNET_GROUP_DIM)
        merged = _merge(y_na, y_mla, y_f, w_branch, l, proj)
        h = _outproj(merged, w_o, l, h)
        final_gain = norm_final.reshape(1, D) if l == depth - 1 else None
        h = _moe_layer(h, norm_moe[l].reshape(1, D), w_router[l], w_exp_gate, w_exp_up, w_exp_down, l, final_gain,
                       B, S)
    return h.reshape(B, S, D)
```

```python
import functools

import numpy as np
import jax
import jax.numpy as jnp
from jax import lax
from jax.experimental import pallas as pl
from jax.experimental.pallas import tpu as pltpu

D_MODEL = 2048
GRID_W = 64
NA_HEADS = 16
NA_HEAD_DIM = 64
NA_WIDTH = NA_HEADS * NA_HEAD_DIM
NA_KH = 8
NA_KW = 16
MLA_HEADS = 8
MLA_NOPE = 128
MLA_ROPE = 64
MLA_V = 128
MLA_Q_LORA = 448
MLA_KV_LORA = 160
ROPE_THETA = 10000.0
FNET_GROUPS = 4
FNET_GROUP_DIM = 256
N_BRANCHES = 3
BRANCH_WIDTH = 1024
N_EXPERTS = 16
EXPERT_FF = 2048
CAPACITY_FACTOR = 2
RMS_EPS = 1e-6
NEG_INF = -1e30

F32 = jnp.float32
BF16 = jnp.bfloat16
I32 = jnp.int32

COL_QKV = 0
COL_UF = 3072
COL_LAT = 4096
COL_GATE = 5120
PROJ_COLS = COL_GATE + N_BRANCHES * D_MODEL
LAT_W = 1024
CQ_PAD = 512
CKV_PAD = 256
VT_ROWS = MLA_V + 16
LOG2E = 1.4426950408889634
TOK_BITS = 12
PFX_BLK = 512
COMB_TT = 256

VMEM_LIMIT = 56 * 1024 * 1024


def _cparams(*sem, **kw):
    return pltpu.CompilerParams(dimension_semantics=sem, vmem_limit_bytes=VMEM_LIMIT, **kw)


def _inproj_kernel(x_ref, g_ref, w_ref, b_ref, o_ref, xn_ref, *, gate_tile0):
    j = pl.program_id(1)

    @pl.when(j == 0)
    def _():
        x = x_ref[...]
        ms = jnp.mean(x * x, axis=-1, keepdims=True)
        xn_ref[...] = (x * lax.rsqrt(ms + RMS_EPS) * g_ref[...]).astype(BF16)

    acc = jnp.dot(xn_ref[...], w_ref[...], preferred_element_type=F32)

    @pl.when(j < gate_tile0)
    def _():
        o_ref[...] = acc.astype(o_ref.dtype)

    @pl.when(j >= gate_tile0)
    def _():
        z = acc + b_ref[...]
        o_ref[...] = (0.5 * jnp.tanh(0.5 * z) + 0.5).astype(o_ref.dtype)


def _inproj(h2, gain, w_all, b_all, tm=1024, tn=1024):
    T = h2.shape[0]
    return pl.pallas_call(
        functools.partial(_inproj_kernel, gate_tile0=COL_GATE // tn),
        out_shape=jax.ShapeDtypeStruct((T, PROJ_COLS), BF16),
        grid=(T // tm, PROJ_COLS // tn),
        in_specs=[
            pl.BlockSpec((tm, D_MODEL), lambda i, j: (i, 0)),
            pl.BlockSpec((1, D_MODEL), lambda i, j: (0, 0)),
            pl.BlockSpec((D_MODEL, tn), lambda i, j: (0, j)),
            pl.BlockSpec((1, tn), lambda i, j: (0, j)),
        ],
        out_specs=pl.BlockSpec((tm, tn), lambda i, j: (i, j)),
        scratch_shapes=[pltpu.VMEM((tm, D_MODEL), BF16)],
        compiler_params=_cparams("parallel", "arbitrary"),
        name="inproj",
    )(h2, gain, w_all, b_all)


def _na_kernel(q_ref, k_ref, v_ref, bias_a_ref, bias_b_ref, o_ref, *, rows):
    i = pl.program_id(1)
    nk = NA_KH * GRID_W
    lo = lax.broadcasted_iota(jnp.int32, (GRID_W, 128), 1) < NA_HEAD_DIM
    nt = (((1,), (1,)), ((), ()))
    row0 = []
    for rr in range(2):
        start = jnp.clip(2 * i + rr - NA_KH // 2, 0, rows - NA_KH)
        row0.append(pl.multiple_of(start * GRID_W, GRID_W))

    def scores(rr, hp):
        sl = slice(hp * 128, (hp + 1) * 128)
        q2 = q_ref[rr * GRID_W:(rr + 1) * GRID_W, sl]
        zero = jnp.zeros_like(q2)
        qq = jnp.concatenate([jnp.where(lo, q2, zero), jnp.where(lo, zero, q2)], axis=0)
        return lax.dot_general(qq, k_ref[pl.ds(row0[rr], nk), sl], nt, preferred_element_type=F32)

    chains = [(rr, hp) for hp in range(NA_HEADS // 2) for rr in range(2)]
    s_next = scores(*chains[0])
    for n, (rr, hp) in enumerate(chains):
        s = s_next
        if n + 1 < len(chains):
            s_next = scores(*chains[n + 1])
        sl = slice(hp * 128, (hp + 1) * 128)
        bias_ref = bias_a_ref if rr == 0 else bias_b_ref
        s = s * (NA_HEAD_DIM ** -0.5 * LOG2E) + bias_ref[0, hp]
        m = jnp.max(s, axis=-1, keepdims=True)
        p = jnp.exp2(s - m)
        l = jnp.sum(p, axis=-1, keepdims=True)
        o = jnp.dot(p.astype(BF16), v_ref[pl.ds(row0[rr], nk), sl], preferred_element_type=F32) / l
        o_ref[rr * GRID_W:(rr + 1) * GRID_W, sl] = jnp.where(lo, o[:GRID_W], o[GRID_W:]).astype(o_ref.dtype)


def _na_bias_kernel(toep_ref, o_ref):
    t = pl.program_id(0)
    hp = pl.program_id(1)
    for a in range(2):
        for i in range(NA_KH):
            o_ref[0, 0, a * GRID_W:(a + 1) * GRID_W, i * GRID_W:(i + 1) * GRID_W] = (
                toep_ref[2 * hp + a, i - t + NA_KH - 1])


def _na_bias_tables(rpb):
    cols = np.arange(GRID_W)
    col_start = np.clip(cols - NA_KW // 2, 0, GRID_W - NA_KW)
    col_valid = (cols[None, :] >= col_start[:, None]) & (cols[None, :] < col_start[:, None] + NA_KW)
    col_idx = np.clip(cols[None, :] - cols[:, None] + NA_KW - 1, 0, 2 * NA_KW - 2)
    onehot = (col_idx[None] == np.arange(2 * NA_KW - 1)[:, None, None]).astype(np.float32)
    toep = jnp.einsum("hdc,cqj->hdqj", rpb.astype(F32), jnp.asarray(onehot), precision=lax.Precision.HIGHEST)
    toep = jnp.where(col_valid[None, None], toep * LOG2E, NEG_INF * LOG2E)
    return pl.pallas_call(
        _na_bias_kernel,
        out_shape=jax.ShapeDtypeStruct((NA_KH, NA_HEADS // 2, 2 * GRID_W, NA_KH * GRID_W), F32),
        grid=(NA_KH, NA_HEADS // 2),
        in_specs=[pl.BlockSpec((NA_HEADS, 2 * NA_KH - 1, GRID_W, GRID_W), lambda t, hp: (0, 0, 0, 0))],
        out_specs=pl.BlockSpec((1, 1, 2 * GRID_W, NA_KH * GRID_W), lambda t, hp: (t, hp, 0, 0)),
        compiler_params=_cparams("parallel", "parallel"),
        name="na_bias",
    )(toep)


def _na_attention(proj, bias, B, S):
    rows = S // GRID_W
    assert rows >= NA_KH and rows % 2 == 0
    T = B * S
    half = rows // 2

    def bias_spec(rr):
        def bias_map(b, i):
            r = 2 * i + rr
            return (r - jnp.clip(r - NA_KH // 2, 0, rows - NA_KH), 0, 0, 0)
        return pl.BlockSpec((1, NA_HEADS // 2, 2 * GRID_W, NA_KH * GRID_W), bias_map)

    return pl.pallas_call(
        functools.partial(_na_kernel, rows=rows),
        out_shape=jax.ShapeDtypeStruct((T, NA_WIDTH), BF16),
        grid=(B, half),
        in_specs=[
            pl.BlockSpec((2 * GRID_W, NA_WIDTH), lambda b, i: (b * half + i, 0)),
            pl.BlockSpec((S, NA_WIDTH), lambda b, i: (b, 1)),
            pl.BlockSpec((S, NA_WIDTH), lambda b, i: (b, 2)),
            bias_spec(0), bias_spec(1),
        ],
        out_specs=pl.BlockSpec((2 * GRID_W, NA_WIDTH), lambda b, i: (b * half + i, 0)),
        compiler_params=_cparams("parallel", "arbitrary"),
        name="na_attn",
    )(proj, proj, proj, bias, bias)


def _mla_up_kernel(lat_ref, cos_ref, sin_ref, cost_ref, sint_ref, qn_ref, kvn_ref, wq_ref, wrot_ref, wk_ref,
                   wvt_ref, q_ref, k_ref, vt_ref):
    cq = lat_ref[:, 0:CQ_PAD].astype(F32)
    ms = jnp.sum(cq * cq, axis=-1, keepdims=True) * (1.0 / MLA_Q_LORA)
    xq = (cq * lax.rsqrt(ms + RMS_EPS) * qn_ref[...]).astype(BF16)
    ckv = lat_ref[:, CQ_PAD:CQ_PAD + CKV_PAD].astype(F32)
    ms2 = jnp.sum(ckv * ckv, axis=-1, keepdims=True) * (1.0 / MLA_KV_LORA)
    xkv = (ckv * lax.rsqrt(ms2 + RMS_EPS) * kvn_ref[...]).astype(BF16)
    cos = cos_ref[...]
    sin = sin_ref[...]
    kpe = (lat_ref[:, 768:896].astype(F32) * cos + lat_ref[:, 896:1024].astype(F32) * sin).astype(BF16)
    scale = (MLA_NOPE + MLA_ROPE) ** -0.5 * LOG2E
    ones = jnp.ones((VT_ROWS - MLA_V, lat_ref.shape[0]), BF16)
    nt = (((1,), (1,)), ((), ()))
    for h in range(MLA_HEADS):
        a = lax.dot_general(wq_ref[h * 256:(h + 1) * 256, :], xq, nt, preferred_element_type=F32)
        rt = lax.dot_general(wrot_ref[h * 128:(h + 1) * 128, :], xq, nt, preferred_element_type=F32)
        q_ref[0, h, 0:128, :] = (a[0:128] * scale).astype(BF16)
        q_ref[0, h, 128:256, :] = ((a[128:256] * cost_ref[...] + rt * sint_ref[...]) * scale).astype(BF16)
        kn = jnp.dot(xkv, wk_ref[:, h * 128:(h + 1) * 128], preferred_element_type=F32)
        k_ref[0, h, :, 0:128] = kn.astype(BF16)
        k_ref[0, h, :, 128:256] = kpe
        vt = lax.dot_general(wvt_ref[h * 128:(h + 1) * 128, :], xkv, nt, preferred_element_type=F32)
        vt_ref[0, h, 0:MLA_V, :] = vt.astype(BF16)
        vt_ref[0, h, MLA_V:VT_ROWS, :] = ones


def _mla_up(proj, cos128, sin128, cos_t, sin_t, qn, kvn, wq, wrot, wk, wvt, B, S, tm=512):
    nt = S // tm
    lat_blk = COL_LAT // LAT_W
    const = lambda b, i: (0, 0)
    return pl.pallas_call(
        _mla_up_kernel,
        out_shape=(
            jax.ShapeDtypeStruct((B, MLA_HEADS, 256, S), BF16),
            jax.ShapeDtypeStruct((B, MLA_HEADS, S, 256), BF16),
            jax.ShapeDtypeStruct((B, MLA_HEADS, VT_ROWS, S), BF16),
        ),
        grid=(B, nt),
        in_specs=[
            pl.BlockSpec((tm, LAT_W), lambda b, i: (b * nt + i, lat_blk)),
            pl.BlockSpec((tm, 128), lambda b, i: (i, 0)),
            pl.BlockSpec((tm, 128), lambda b, i: (i, 0)),
            pl.BlockSpec((128, tm), lambda b, i: (0, i)),
            pl.BlockSpec((128, tm), lambda b, i: (0, i)),
            pl.BlockSpec((1, CQ_PAD), const),
            pl.BlockSpec((1, CKV_PAD), const),
            pl.BlockSpec((MLA_HEADS * 256, CQ_PAD), const),
            pl.BlockSpec((MLA_HEADS * 128, CQ_PAD), const),
            pl.BlockSpec((CKV_PAD, MLA_HEADS * MLA_NOPE), const),
            pl.BlockSpec((MLA_HEADS * MLA_V, CKV_PAD), const),
        ],
        out_specs=(
            pl.BlockSpec((1, MLA_HEADS, 256, tm), lambda b, i: (b, 0, 0, i)),
            pl.BlockSpec((1, MLA_HEADS, tm, 256), lambda b, i: (b, 0, i, 0)),
            pl.BlockSpec((1, MLA_HEADS, VT_ROWS, tm), lambda b, i: (b, 0, 0, i)),
        ),
        compiler_params=_cparams("parallel", "parallel"),
        name="mla_up",
    )(proj, cos128, sin128, cos_t, sin_t, qn, kvn, wq, wrot, wk, wvt)


def _mla_attn_kernel(q_ref, k_ref, vt_ref, o_ref, *, ck):
    qt = q_ref[0, 0]
    n_chunks = k_ref.shape[2] // ck
    m = acc = None

    def scores(c):
        return jnp.dot(k_ref[0, 0, c * ck:(c + 1) * ck, :], qt, preferred_element_type=F32)

    s_next = scores(0)
    for c in range(n_chunks):
        s = s_next
        if c + 1 < n_chunks:
            s_next = scores(c + 1)
        mc = jnp.max(s, axis=0, keepdims=True)
        m_new = mc if c == 0 else jnp.maximum(m, mc)
        p = jnp.exp2(s - m_new).astype(BF16)
        pv = jnp.dot(vt_ref[0, 0, :, c * ck:(c + 1) * ck], p, preferred_element_type=F32)
        acc = pv if c == 0 else acc * jnp.exp2(m - m_new) + pv
        m = m_new
    o = acc[0:MLA_V] / acc[MLA_V:MLA_V + 1]
    o_ref[...] = o.T.astype(o_ref.dtype)


def _mla_attn(q, k, vt, B, S, tq=2048, ck=512):
    tq = min(tq, S)
    nq = S // tq
    return pl.pallas_call(
        functools.partial(_mla_attn_kernel, ck=ck),
        out_shape=jax.ShapeDtypeStruct((B * S, MLA_HEADS * MLA_V), BF16),
        grid=(B, MLA_HEADS, nq),
        in_specs=[
            pl.BlockSpec((1, 1, 256, tq), lambda b, h, i: (b, h, 0, i)),
            pl.BlockSpec((1, 1, S, 256), lambda b, h, i: (b, h, 0, 0)),
            pl.BlockSpec((1, 1, VT_ROWS, S), lambda b, h, i: (b, h, 0, 0)),
        ],
        out_specs=pl.BlockSpec((tq, MLA_V), lambda b, h, i: (b * nq + i, h)),
        compiler_params=_cparams("parallel", "parallel", "arbitrary"),
        name="mla_attn",
    )(q, k, vt)


def _fnet_ch_kernel(lo_ref, hi_ref, w_ref, a_ref, b_ref):
    gd = FNET_GROUP_DIM
    lo = lo_ref[...].astype(F32)
    hi = hi_ref[...].astype(F32)
    for p, u in enumerate(((lo + hi).astype(BF16), (lo - hi).astype(BF16))):
        for g in range(FNET_GROUPS):
            ab = jnp.dot(u[:, g * gd:(g + 1) * gd], w_ref[...], preferred_element_type=F32)
            a_ref[p, :, g * gd:(g + 1) * gd] = ab[:, :gd].astype(a_ref.dtype)
            b_ref[p, :, g * gd:(g + 1) * gd] = ab[:, gd:].astype(b_ref.dtype)


def _fnet_channel(proj, w_cs, B, S, tm=1024):
    W = FNET_GROUPS * FNET_GROUP_DIM
    half = S // 2
    tm = min(tm, half)
    nh = half // tm
    out = jax.ShapeDtypeStruct((2, B * half, W), BF16)
    return pl.pallas_call(
        _fnet_ch_kernel,
        out_shape=(out, out),
        grid=(B, nh),
        in_specs=[
            pl.BlockSpec((tm, W), lambda b, i: (b * 2 * nh + i, COL_UF // W)),
            pl.BlockSpec((tm, W), lambda b, i: (b * 2 * nh + nh + i, COL_UF // W)),
            pl.BlockSpec((FNET_GROUP_DIM, 2 * FNET_GROUP_DIM), lambda b, i: (0, 0)),
        ],
        out_specs=(pl.BlockSpec((2, tm, W), lambda b, i: (0, b * nh + i, 0)),
                   pl.BlockSpec((2, tm, W), lambda b, i: (0, b * nh + i, 0))),
        compiler_params=_cparams("parallel", "parallel"),
        name="fnet_channel",
    )(proj, proj, w_cs)


def _fnet_pos_kernel(c_ref, s_ref, a_ref, b_ref, o_ref, *, scale):
    y = (jnp.dot(c_ref[0], a_ref[0], preferred_element_type=F32)
         + jnp.dot(s_ref[0], b_ref[0], preferred_element_type=F32))
    o_ref[...] = (y * scale).astype(o_ref.dtype)


def _fnet_position(c_tab, sn_tab, a, b, B, S, tm=1024, tn=512):
    W = FNET_GROUPS * FNET_GROUP_DIM
    half = S // 2
    tm = min(tm, half)
    nm = half // tm
    nn = W // tn
    scale = float((S * FNET_GROUP_DIM) ** -0.5)
    return pl.pallas_call(
        functools.partial(_fnet_pos_kernel, scale=scale),
        out_shape=jax.ShapeDtypeStruct((B * half, 2 * W), BF16),
        grid=(2, nm, B, nn),
        in_specs=[
            pl.BlockSpec((1, tm, half), lambda p, m, bb, n: (p, m, 0)),
            pl.BlockSpec((1, tm, half), lambda p, m, bb, n: (p, m, 0)),
            pl.BlockSpec((1, half, tn), lambda p, m, bb, n: (p, bb, n)),
            pl.BlockSpec((1, half, tn), lambda p, m, bb, n: (p, bb, n)),
        ],
        out_specs=pl.BlockSpec((tm, tn), lambda p, m, bb, n: (bb * nm + m, p * nn + n)),
        compiler_params=_cparams("parallel", "parallel", "parallel", "parallel"),
        name="fnet_position",
    )(c_tab, sn_tab, a, b)


def _dft_table_kernel(chi_ref, shi_ref, clo_ref, slo_ref, cos_ref, nsin_ref, *, hb, nlo):
    cl, sl = clo_ref[0], slo_ref[0]
    for u in range(hb):
        ch, sh = chi_ref[u:u + 1, :], shi_ref[u:u + 1, :]
        cos_ref[0, u * nlo:(u + 1) * nlo, :] = (ch * cl - sh * sl).astype(cos_ref.dtype)
        nsin_ref[0, u * nlo:(u + 1) * nlo, :] = (-(sh * cl + ch * sl)).astype(nsin_ref.dtype)


def _dft_tables(S):
    gd = FNET_GROUP_DIM
    ck = (np.arange(gd)[:, None] * np.arange(gd)[None, :]) % gd
    ang = 2.0 * np.pi * ck / gd
    w_cs = jnp.asarray(np.concatenate([np.cos(ang), np.sin(ang)], axis=1), F32).astype(BF16)
    kb = 64
    half = S // 2
    n = lax.broadcasted_iota(jnp.int32, (1, half), 1)

    def thin(rows, period):
        ang_ = ((lax.broadcasted_iota(jnp.int32, (rows, 1), 0) * n) % period).astype(F32) * (2.0 * np.pi / period)
        return jnp.cos(ang_), jnp.sin(ang_)

    c_hi, s_hi = thin(S // kb, S // kb)
    c_lo, s_lo = thin(kb, S)
    split = lambda t: jnp.stack([t[0::2], t[1::2]])
    c_lo, s_lo = split(c_lo), split(s_lo)
    nlo = kb // 2
    hb = 8
    assert (S // kb) % hb == 0
    tab = jax.ShapeDtypeStruct((2, half, half), BF16)
    cos_t, nsin_t = pl.pallas_call(
        functools.partial(_dft_table_kernel, hb=hb, nlo=nlo),
        out_shape=(tab, tab),
        grid=(2, S // kb // hb),
        in_specs=[pl.BlockSpec((hb, half), lambda p, g: (g, 0)), pl.BlockSpec((hb, half), lambda p, g: (g, 0)),
                  pl.BlockSpec((1, nlo, half), lambda p, g: (p, 0, 0)),
                  pl.BlockSpec((1, nlo, half), lambda p, g: (p, 0, 0))],
        out_specs=(pl.BlockSpec((1, hb * nlo, half), lambda p, g: (p, g, 0)),
                   pl.BlockSpec((1, hb * nlo, half), lambda p, g: (p, g, 0))),
        compiler_params=_cparams("parallel", "parallel"),
        name="dft_tables",
    )(c_hi, s_hi, c_lo, s_lo)
    return w_cs, cos_t, nsin_t


def _merge_kernel(yn_ref, ym_ref, yf_ref, w_ref, g0_ref, g1_ref, g2_ref, o_ref, wbf_ref):
    @pl.when(pl.program_id(1) == 0)
    def _():
        wbf_ref[...] = w_ref[0].astype(BF16)

    acc = g0_ref[...].astype(F32) * jnp.dot(yn_ref[...], wbf_ref[0], preferred_element_type=F32)
    acc += g1_ref[...].astype(F32) * jnp.dot(ym_ref[...], wbf_ref[1], preferred_element_type=F32)
    acc += g2_ref[...].astype(F32) * jnp.dot(yf_ref[...], wbf_ref[2], preferred_element_type=F32)
    o_ref[...] = acc.astype(o_ref.dtype)


def _merge(y_na, y_mla, y_f, w_branch, layer, proj, tm=1024, tn=512):
    T = y_na.shape[0]
    ybs = pl.BlockSpec((tm, BRANCH_WIDTH), lambda j, i: (i, 0))

    def gate_spec(br):
        off = (COL_GATE + br * D_MODEL) // tn
        return pl.BlockSpec((tm, tn), lambda j, i: (i, off + j))

    return pl.pallas_call(
        _merge_kernel,
        out_shape=jax.ShapeDtypeStruct((T, D_MODEL), BF16),
        grid=(D_MODEL // tn, T // tm),
        in_specs=[ybs, ybs, ybs,
                  pl.BlockSpec((1, N_BRANCHES, BRANCH_WIDTH, tn), lambda j, i: (layer, 0, 0, j)),
                  gate_spec(0), gate_spec(1), gate_spec(2)],
        out_specs=pl.BlockSpec((tm, tn), lambda j, i: (i, j)),
        scratch_shapes=[pltpu.VMEM((N_BRANCHES, BRANCH_WIDTH, tn), BF16)],
        compiler_params=_cparams("parallel", "arbitrary"),
        name="merge",
    )(y_na, y_mla, y_f, w_branch, proj, proj, proj)


def _outproj_kernel(m_ref, w_ref, h_ref, o_ref, wbf_ref):
    @pl.when(pl.program_id(1) == 0)
    def _():
        wbf_ref[...] = w_ref[0].astype(BF16)

    o_ref[...] = h_ref[...] + jnp.dot(m_ref[...], wbf_ref[...], preferred_element_type=F32)


def _outproj(merged, w_o, layer, h2, tm=1024, tn=512):
    T = merged.shape[0]
    return pl.pallas_call(
        _outproj_kernel,
        out_shape=jax.ShapeDtypeStruct((T, D_MODEL), F32),
        grid=(D_MODEL // tn, T // tm),
        in_specs=[
            pl.BlockSpec((tm, D_MODEL), lambda j, i: (i, 0)),
            pl.BlockSpec((1, D_MODEL, tn), lambda j, i: (layer, 0, j)),
            pl.BlockSpec((tm, tn), lambda j, i: (i, j)),
        ],
        out_specs=pl.BlockSpec((tm, tn), lambda j, i: (i, j)),
        scratch_shapes=[pltpu.VMEM((D_MODEL, tn), BF16)],
        compiler_params=_cparams("parallel", "arbitrary"),
        name="outproj",
    )(merged, w_o, h2)


def _router_kernel(x_ref, g_ref, wrt_ref, aff_ref):
    x = x_ref[...]
    ms = jnp.mean(x * x, axis=-1, keepdims=True)
    xn = x * lax.rsqrt(ms + RMS_EPS) * g_ref[...]
    logits = lax.dot_general(wrt_ref[...], xn, (((1,), (1,)), ((), ())), preferred_element_type=F32,
                             precision=lax.Precision.HIGHEST)
    m = jnp.max(logits, axis=0, keepdims=True)
    e = jnp.exp(logits - m)
    aff_ref[0] = e / jnp.sum(e, axis=0, keepdims=True)


def _router(h2, gain, w_router_t, B, S, tm=1024):
    nt = S // tm
    return pl.pallas_call(
        _router_kernel,
        out_shape=jax.ShapeDtypeStruct((B, N_EXPERTS, S), F32),
        grid=(B, nt),
        in_specs=[
            pl.BlockSpec((tm, D_MODEL), lambda b, i: (b * nt + i, 0)),
            pl.BlockSpec((1, D_MODEL), lambda b, i: (0, 0)),
            pl.BlockSpec((N_EXPERTS, D_MODEL), lambda b, i: (0, 0)),
        ],
        out_specs=pl.BlockSpec((1, N_EXPERTS, tm), lambda b, i: (b, 0, i)),
        compiler_params=_cparams("parallel", "parallel"),
        name="router",
    )(h2, gain, w_router_t)


def _route_kernel(a_ref, val_ref, gate_ref, starts_ref, kmax_ref, pos_s, pv_s, a_s, *, cap, tt):
    E, S = a_ref.shape[1], a_ref.shape[2]
    a = a_ref[0]

    def as_float(b):
        return lax.bitcast_convert_type(b, F32)

    def bisect(_, carry):
        lo, hi = carry
        mid = lo + ((hi - lo + 1) >> 1)
        cnt = jnp.sum(jnp.where(a >= as_float(mid), 1.0, 0.0), axis=1, keepdims=True)
        ge = cnt >= cap
        return jnp.where(ge, mid, lo), jnp.where(ge, hi, mid - 1)

    lo0 = jnp.zeros((E, 1), I32)
    hi0 = jnp.full((E, 1), 0x7F7FFFFF, I32)
    thr_bits, _ = lax.fori_loop(0, 31, bisect, (lo0, hi0))
    thr, thr_up = as_float(thr_bits), as_float(thr_bits + 1)

    r_i = lax.broadcasted_iota(I32, (PFX_BLK, PFX_BLK), 0)
    c_i = lax.broadcasted_iota(I32, (PFX_BLK, PFX_BLK), 1)
    tri = jnp.where(r_i < c_i, 1.0, 0.0).astype(BF16)

    def excl_prefix(mask):
        x = jnp.where(mask, 1.0, 0.0).astype(BF16)
        carry = jnp.zeros((E, 1), F32)
        outs = []
        for j in range(S // PFX_BLK):
            blk = x[:, j * PFX_BLK:(j + 1) * PFX_BLK]
            outs.append(jnp.dot(blk, tri, preferred_element_type=F32) + carry)
            carry = carry + jnp.sum(blk.astype(F32), axis=1, keepdims=True)
        return jnp.concatenate(outs, axis=1)

    gt = a >= thr_up
    eq = (a >= thr) & (a < thr_up)
    need = cap - jnp.sum(jnp.where(gt, 1.0, 0.0), axis=1, keepdims=True)
    sel = gt | (eq & (excl_prefix(eq) < need))
    pos = excl_prefix(sel)

    self_ = jnp.where(sel, 1.0, 0.0)
    e_r = lax.broadcasted_iota(I32, (E, E), 0)
    e_c = lax.broadcasted_iota(I32, (E, E), 1)
    low = jnp.where(e_c < e_r, 1.0, 0.0).astype(BF16)
    rank = jnp.dot(low, self_.astype(BF16), preferred_element_type=F32)
    tok = lax.broadcasted_iota(I32, (E, S), 1)
    pos_s[...] = jnp.where(sel, pos.astype(I32), -1)
    pv_s[...] = (tok + (rank.astype(I32) << TOK_BITS)).astype(F32)
    a_s[...] = a

    lane = lax.broadcasted_iota(I32, (E, 128), 1)
    starts = jnp.full((E, 128), cap, I32)
    kcount = jnp.sum(self_, axis=0, keepdims=True)
    kmax = jnp.zeros((8, 128), I32)
    lane8 = lax.broadcasted_iota(I32, (8, 128), 1)
    posi = pos.astype(I32)
    for j in range(S // tt):
        starts = jnp.where(lane == j, posi[:, j * tt:j * tt + 1], starts)
        kj = jnp.max(kcount[:, j * tt:(j + 1) * tt], axis=1, keepdims=True).astype(I32)
        kmax = jnp.where(lane8 == j, kj, kmax)
    starts_ref[0] = starts
    kmax_ref[0] = kmax

    c_iota = lax.broadcasted_iota(I32, (cap, PFX_BLK), 0)
    lane_c = lax.broadcasted_iota(I32, (cap, 128), 1)

    def per_expert(e, carry):
        val_acc, gate_acc = carry
        v_part = jnp.zeros((cap, PFX_BLK), F32)
        g_part = jnp.zeros((cap, PFX_BLK), F32)
        for j in range(S // PFX_BLK):
            sl = pl.ds(j * PFX_BLK, PFX_BLK)
            hit = pos_s[pl.ds(e, 1), sl] == c_iota
            v_part = v_part + jnp.where(hit, pv_s[pl.ds(e, 1), sl], 0.0)
            g_part = g_part + jnp.where(hit, a_s[pl.ds(e, 1), sl], 0.0)
        v = jnp.sum(v_part, axis=1, keepdims=True).astype(I32)
        g = jnp.sum(g_part, axis=1, keepdims=True)
        return jnp.where(lane_c == e, v, val_acc), jnp.where(lane_c == e, g, gate_acc)

    val, gate = lax.fori_loop(0, E, per_expert, (jnp.zeros((cap, 128), I32), jnp.zeros((cap, 128), F32)))
    val_ref[0] = val
    gate_ref[0] = gate


def _route(aff_t, cap, tt=COMB_TT):
    B, E, S = aff_t.shape
    assert S <= (1 << TOK_BITS) and S // tt < 128 and S % PFX_BLK == 0
    return pl.pallas_call(
        functools.partial(_route_kernel, cap=cap, tt=tt),
        out_shape=(
            jax.ShapeDtypeStruct((B, cap, 128), I32),
            jax.ShapeDtypeStruct((B, cap, 128), F32),
            jax.ShapeDtypeStruct((B, E, 128), I32),
            jax.ShapeDtypeStruct((B, 8, 128), I32),
        ),
        grid=(B,),
        in_specs=[pl.BlockSpec((1, E, S), lambda b: (b, 0, 0))],
        out_specs=(
            pl.BlockSpec((1, cap, 128), lambda b: (b, 0, 0)),
            pl.BlockSpec((1, cap, 128), lambda b: (b, 0, 0)),
            pl.BlockSpec((1, E, 128), lambda b: (b, 0, 0)),
            pl.BlockSpec((1, 8, 128), lambda b: (b, 0, 0)),
        ),
        scratch_shapes=[pltpu.VMEM((E, S), I32), pltpu.VMEM((E, S), F32), pltpu.VMEM((E, S), F32)],
        compiler_params=_cparams("parallel"),
        name="route",
    )(aff_t)


def _gather_kernel(rows_ref, h_hbm, g_ref, o_ref, buf, sem, *, gt):
    i = pl.program_id(0)
    slot = i % 2

    def issue_tile(t, s):
        def issue(r, _):
            pltpu.make_async_copy(h_hbm.at[pl.ds(rows_ref[t * gt + r], 1)], buf.at[s, pl.ds(r, 1)], sem.at[s]).start()
            return 0

        lax.fori_loop(0, gt, issue, 0, unroll=8)

    @pl.when(i == 0)
    def _():
        issue_tile(0, 0)

    @pl.when(i + 1 < pl.num_programs(0))
    def _():
        issue_tile(i + 1, 1 - slot)

    pltpu.make_async_copy(h_hbm.at[pl.ds(0, gt)], buf.at[slot], sem.at[slot]).wait()
    x = buf[slot]
    ms = jnp.mean(x * x, axis=-1, keepdims=True)
    o_ref[...] = (x * lax.rsqrt(ms + RMS_EPS) * g_ref[...]).astype(o_ref.dtype)


def _gather_norm(rows, h2, gain, gt=1024):
    n = rows.shape[0]
    gt = min(gt, n)
    return pl.pallas_call(
        functools.partial(_gather_kernel, gt=gt),
        out_shape=jax.ShapeDtypeStruct((n, D_MODEL), BF16),
        grid_spec=pltpu.PrefetchScalarGridSpec(
            num_scalar_prefetch=1,
            grid=(n // gt,),
            in_specs=[pl.BlockSpec(memory_space=pl.ANY), pl.BlockSpec((1, D_MODEL), lambda i, rows: (0, 0))],
            out_specs=pl.BlockSpec((gt, D_MODEL), lambda i, rows: (i, 0)),
            scratch_shapes=[pltpu.VMEM((2, gt, D_MODEL), F32), pltpu.SemaphoreType.DMA((2,))],
        ),
        compiler_params=_cparams("arbitrary", disable_bounds_checks=True),
        name="gather_norm",
    )(rows, h2, gain)


def _combine_kernel(dst_ref, starts_ref, kmax_ref, h_ref, ye_hbm, fg_ref, o_ref, planes, sem, *,
                    B, E, cap, tt, nt_pad, final_norm):
    b = pl.program_id(0)
    j = pl.program_id(1)
    k = kmax_ref[b * nt_pad + j]

    def zero(p, _):
        planes[pl.ds(pl.multiple_of(p * tt, tt), tt), :] = jnp.zeros((tt, D_MODEL), F32)
        return 0

    lax.fori_loop(0, k, zero, 0)

    def row_copy(slot, dst_row):
        return pltpu.make_async_copy(ye_hbm.at[pl.ds(slot, 1)], planes.at[pl.ds(dst_row, 1)], sem)

    def per_expert(e, n):
        sbase = (b * E + e) * nt_pad + j
        c0 = starts_ref[sbase]
        c1 = starts_ref[sbase + 1]
        lbase = (e * B + b) * cap

        def issue(c):
            row_copy(lbase + c, dst_ref[lbase + c]).start()

        def issue4(i, _):
            for u in range(4):
                issue(c0 + 4 * i + u)
            return 0

        def issue1(i, _):
            issue(c1 - 1 - i)
            return 0

        cnt = c1 - c0
        lax.fori_loop(0, cnt >> 2, issue4, 0)
        lax.fori_loop(0, cnt & 3, issue1, 0)
        return n + cnt

    n = lax.fori_loop(0, E, per_expert, 0)

    for bit in range((E * tt).bit_length()):
        @pl.when(((n >> bit) & 1) == 1)
        def _():
            rows = 1 << bit
            pltpu.make_async_copy(ye_hbm.at[pl.ds(0, rows)], planes.at[pl.ds(0, rows)], sem).wait()

    for cs in range(D_MODEL // 128):
        cols = slice(cs * 128, (cs + 1) * 128)

        def add(p, acc):
            return acc + planes[pl.ds(pl.multiple_of(p * tt, tt), tt), cols]

        o_ref[:, cols] = lax.fori_loop(0, k, add, h_ref[:, cols])

    if final_norm:
        x = o_ref[...]
        ms = jnp.mean(x * x, axis=-1, keepdims=True)
        o_ref[...] = x * lax.rsqrt(ms + RMS_EPS) * fg_ref[...]


def _combine(h2, ye, dst_rows, starts, kmax, final_gain, B, S, cap, tt=COMB_TT):
    E = N_EXPERTS
    nt = S // tt
    nt_pad = starts.shape[0] // (B * E)
    final_norm = final_gain is not None
    fg = final_gain if final_norm else jnp.ones((1, D_MODEL), F32)
    return pl.pallas_call(
        functools.partial(_combine_kernel, B=B, E=E, cap=cap, tt=tt, nt_pad=nt_pad, final_norm=final_norm),
        out_shape=jax.ShapeDtypeStruct(h2.shape, F32),
        grid_spec=pltpu.PrefetchScalarGridSpec(
            num_scalar_prefetch=3,
            grid=(B, nt),
            in_specs=[pl.BlockSpec((tt, D_MODEL), lambda b, j, *_: (b * nt + j, 0)),
                      pl.BlockSpec(memory_space=pl.ANY),
                      pl.BlockSpec((1, D_MODEL), lambda b, j, *_: (0, 0))],
            out_specs=pl.BlockSpec((tt, D_MODEL), lambda b, j, *_: (b * nt + j, 0)),
            scratch_shapes=[pltpu.VMEM((E * tt, D_MODEL), F32), pltpu.SemaphoreType.DMA(())],
        ),
        compiler_params=_cparams("arbitrary", "arbitrary", disable_bounds_checks=True),
        name="combine",
    )(dst_rows, starts, kmax, h2, ye, fg)


def _ffn_up_kernel(x_ref, wg_ref, wu_ref, o_ref):
    x = x_ref[...]
    a = jnp.dot(x, wg_ref[0, 0].astype(BF16), preferred_element_type=F32)
    u = jnp.dot(x, wu_ref[0, 0].astype(BF16), preferred_element_type=F32)
    o_ref[...] = (a * (1.0 / (1.0 + jnp.exp(-a))) * u).astype(o_ref.dtype)


def _ffn_down_kernel(hid_ref, wd_ref, gate_ref, o_ref):
    y = jnp.dot(hid_ref[...], wd_ref[0, 0].astype(BF16), preferred_element_type=F32)
    o_ref[...] = y * gate_ref[...]


def _experts(xe, wg, wu, wd, layer, gate, tf=256, tn=512):
    n_tok = xe.shape[0]
    per_e = n_tok // N_EXPERTS
    ff = wg.shape[-1]
    hid = pl.pallas_call(
        _ffn_up_kernel,
        out_shape=jax.ShapeDtypeStruct((n_tok, ff), BF16),
        grid=(N_EXPERTS, ff // tf),
        in_specs=[
            pl.BlockSpec((per_e, D_MODEL), lambda e, f: (e, 0)),
            pl.BlockSpec((1, 1, D_MODEL, tf), lambda e, f: (layer, e, 0, f)),
            pl.BlockSpec((1, 1, D_MODEL, tf), lambda e, f: (layer, e, 0, f)),
        ],
        out_specs=pl.BlockSpec((per_e, tf), lambda e, f: (e, f)),
        compiler_params=_cparams("parallel", "arbitrary"),
        name="ffn_up",
    )(xe, wg, wu)
    return pl.pallas_call(
        _ffn_down_kernel,
        out_shape=jax.ShapeDtypeStruct((n_tok, D_MODEL), F32),
        grid=(N_EXPERTS, D_MODEL // tn),
        in_specs=[
            pl.BlockSpec((per_e, ff), lambda e, n: (e, 0)),
            pl.BlockSpec((1, 1, ff, tn), lambda e, n: (layer, e, 0, n)),
            pl.BlockSpec((per_e, 1), lambda e, n: (e, 0)),
        ],
        out_specs=pl.BlockSpec((per_e, tn), lambda e, n: (e, n)),
        compiler_params=_cparams("parallel", "arbitrary"),
        name="ffn_down",
    )(hid, wd, gate)


def _moe_layer(h2, gain, w_router, wg, wu, wd, layer, final_gain, B, S):
    E = N_EXPERTS
    cap = CAPACITY_FACTOR * S // E
    nt = S // COMB_TT
    aff_t = _router(h2, gain, jnp.transpose(w_router), B, S)
    val, gate, starts, kmax = _route(aff_t, cap)
    val_e = jnp.transpose(val[:, :, :E], (2, 0, 1))
    gate_e = jnp.transpose(gate[:, :, :E], (2, 0, 1)).reshape(-1, 1)
    tok_e = val_e & ((1 << TOK_BITS) - 1)
    rows = (tok_e + (jnp.arange(B, dtype=I32) * S)[None, :, None]).reshape(-1)
    xe = _gather_norm(rows, h2, gain)
    ye = _experts(xe, wg, wu, wd, layer, gate_e)
    dst = ((val_e >> TOK_BITS) * COMB_TT + tok_e % COMB_TT).reshape(-1)
    starts_flat = starts[:, :, :nt + 1].reshape(-1)
    kmax_flat = kmax[:, 0, :nt + 1].reshape(-1)
    return _combine(h2, ye, dst, starts_flat, kmax_flat, final_gain, B, S, cap)


IN_OFFS = tuple(int(v) for v in np.cumsum([0, 3 * NA_WIDTH, MLA_Q_LORA, MLA_KV_LORA, MLA_ROPE,
                                             FNET_GROUPS * FNET_GROUP_DIM, N_BRANCHES * D_MODEL]))


def _win_relayout_kernel(w_ref, rot_ref, o_ref):
    o = IN_OFFS

    def src(a, b):
        return w_ref[0, :, a:b].astype(BF16)

    o_ref[:, COL_QKV:COL_QKV + 3 * NA_WIDTH] = src(o[0], o[1])
    o_ref[:, COL_UF:COL_UF + o[5] - o[4]] = src(o[4], o[5])
    o_ref[:, COL_LAT:COL_GATE] = jnp.zeros((o_ref.shape[0], LAT_W), BF16)
    o_ref[:, COL_LAT:COL_LAT + MLA_Q_LORA] = src(o[1], o[2])
    o_ref[:, COL_LAT + CQ_PAD:COL_LAT + CQ_PAD + MLA_KV_LORA] = src(o[2], o[3])
    kr0 = COL_LAT + CQ_PAD + CKV_PAD
    kr = src(o[3], o[4])
    o_ref[:, kr0:kr0 + MLA_ROPE] = kr
    o_ref[:, kr0 + 128:kr0 + 128 + MLA_ROPE] = jnp.dot(kr, rot_ref[...], preferred_element_type=F32).astype(BF16)
    o_ref[:, COL_GATE:PROJ_COLS] = src(o[5], o[6])


def _prep_in_weights(w_in_all, layer, b_gate, tr=256):
    half = MLA_ROPE // 2
    rot = np.zeros((MLA_ROPE, MLA_ROPE), np.float32)
    rot[np.arange(half) + half, np.arange(half)] = -1.0
    rot[np.arange(half), np.arange(half) + half] = 1.0
    w_all = pl.pallas_call(
        _win_relayout_kernel,
        out_shape=jax.ShapeDtypeStruct((D_MODEL, PROJ_COLS), BF16),
        grid=(D_MODEL // tr,),
        in_specs=[pl.BlockSpec((1, tr, IN_OFFS[-1]), lambda i: (layer, i, 0)),
                  pl.BlockSpec((MLA_ROPE, MLA_ROPE), lambda i: (0, 0))],
        out_specs=pl.BlockSpec((tr, PROJ_COLS), lambda i: (i, 0)),
        compiler_params=_cparams("parallel"),
        name="win_relayout",
    )(w_in_all, jnp.asarray(rot, BF16))
    b_all = jnp.concatenate([jnp.zeros((COL_GATE,), F32), b_gate]).reshape(1, PROJ_COLS)
    return w_all, b_all


def _prep_mla_weights(w_uq, q_norm, w_ukv, kv_norm):
    qd = MLA_NOPE + MLA_ROPE
    wq3 = w_uq.reshape(MLA_Q_LORA, MLA_HEADS, qd)
    nope, pe = wq3[:, :, :MLA_NOPE], wq3[:, :, MLA_NOPE:]
    z64 = jnp.zeros((MLA_Q_LORA, MLA_HEADS, 64), F32)
    wq = jnp.concatenate([nope, pe, z64], axis=2).reshape(MLA_Q_LORA, MLA_HEADS * 256)
    rot = jnp.concatenate([-pe[:, :, 32:], pe[:, :, :32], z64], axis=2).reshape(MLA_Q_LORA, MLA_HEADS * 128)
    rpad = ((0, CQ_PAD - MLA_Q_LORA), (0, 0))
    wq = jnp.transpose(jnp.pad(wq, rpad)).astype(BF16)
    rot = jnp.transpose(jnp.pad(rot, rpad)).astype(BF16)
    wkv3 = jnp.pad(w_ukv, ((0, CKV_PAD - MLA_KV_LORA), (0, 0))).reshape(CKV_PAD, MLA_HEADS, MLA_NOPE + MLA_V)
    wk = wkv3[:, :, :MLA_NOPE].reshape(CKV_PAD, MLA_HEADS * MLA_NOPE).astype(BF16)
    wvt = jnp.transpose(wkv3[:, :, MLA_NOPE:], (1, 2, 0)).reshape(MLA_HEADS * MLA_V, CKV_PAD).astype(BF16)
    qn = jnp.pad(q_norm, (0, CQ_PAD - MLA_Q_LORA)).reshape(1, CQ_PAD)
    kvn = jnp.pad(kv_norm, (0, CKV_PAD - MLA_KV_LORA)).reshape(1, CKV_PAD)
    return wq, rot, wk, wvt, qn, kvn


def _rope_tables128(S):
    pos = jnp.arange(S, dtype=F32)
    inv = 1.0 / (ROPE_THETA ** (jnp.arange(0, MLA_ROPE, 2, dtype=F32) / MLA_ROPE))
    ang = pos[:, None] * inv[None, :]
    z = jnp.zeros((S, 64), F32)
    cos, sin = jnp.cos(ang), jnp.sin(ang)
    return jnp.concatenate([cos, cos, z], axis=1), jnp.concatenate([sin, sin, z], axis=1)


def kernel(x, w_in, b_gate, w_uq, q_norm, w_ukv, kv_norm, na_rpb, w_branch, w_o, norm_mix, norm_moe,
           w_router, w_exp_gate, w_exp_up, w_exp_down, norm_final):
    B, S, D = x.shape
    T = B * S
    depth = w_in.shape[0]
    cos128, sin128 = _rope_tables128(S)
    cos_t, sin_t = jnp.transpose(cos128), jnp.transpose(sin128)
    w_cs, c_tab, sn_tab = _dft_tables(S)
    h = x.reshape(T, D)
    for l in range(depth):
        w_all, b_all = _prep_in_weights(w_in, l, b_gate[l])
        proj = _inproj(h, norm_mix[l].reshape(1, D), w_all, b_all)
        y_na = _na_attention(proj, _na_bias_tables(na_rpb[l]), B, S)
        wq, wrot, wk, wvt, qn, kvn = _prep_mla_weights(w_uq[l], q_norm[l], w_ukv[l], kv_norm[l])
        q, k, vt = _mla_up(proj, cos128, sin128, cos_t, sin_t, qn, kvn, wq, wrot, wk, wvt, B, S)
        y_mla = _mla_attn(q, k, vt, B, S)
        fa, fb = _fnet_channel(proj, w_cs, B, S)
        y_f = _fnet_position(c_tab, sn_tab, fa, fb, B, S).reshape(T, FNET_GROUPS * FNET_GROUP_DIM)
        merged = _merge(y_na, y_mla, y_f, w_branch, l, proj)
        h = _outproj(merged, w_o, l, h)
        final_gain = norm_final.reshape(1, D) if l == depth - 1 else None
        h = _moe_layer(h, norm_moe[l].reshape(1, D), w_router[l], w_exp_gate, w_exp_up, w_exp_down, l, final_gain,
                       B, S)
    return h.reshape(B, S, D)
```
